```python
import math
import jax, jax.numpy as jnp
from jax import lax
import numpy as np

D_MODEL = 1024
BATCH = 32
SEQ = 256
DEPTH = 2
DEC_BATCH = 8
DEC_SEQ = 4096
PAST_LEN = 256

GRID_W = 64
HEAD_DIM = 64
A_HEADS = 8
A_KV_HEADS = 2
A_GROUP = A_HEADS // A_KV_HEADS
A_WIDTH = A_HEADS * HEAD_DIM
A_KV_WIDTH = A_KV_HEADS * HEAD_DIM
WINDOW = 128
ATTN_BLOCK = 128
ROPE_BASE = 10000.0
B_GROUPS = 4
B_GROUP_DIM = 64
B_WIDTH = B_GROUPS * B_GROUP_DIM
SGU_CHUNK = 128
C_HEADS = 4
C_HEAD_DIM = 64
C_WIDTH = C_HEADS * C_HEAD_DIM
CONV_K = 5
GDN_CHUNK = 64
N_DIRS = 2
D_MIX = A_WIDTH + B_WIDTH + C_WIDTH
IN_SPLITS = (A_WIDTH, A_KV_WIDTH, A_KV_WIDTH, A_WIDTH,
             B_WIDTH, B_WIDTH, B_WIDTH,
             C_WIDTH, C_WIDTH, C_WIDTH, N_DIRS * C_HEADS, N_DIRS * C_HEADS, C_WIDTH)
IN_COLS = sum(IN_SPLITS)
EPS = 1e-6
NEG_INF = -1e30

kernel_name = 'hybrid_flow_trunk_step'


def _rmsnorm(x, g):
    xf = x.astype(jnp.float32)
    y = xf * lax.rsqrt(jnp.mean(xf * xf, axis=-1, keepdims=True) + EPS)
    return (y * g.astype(jnp.float32)).astype(x.dtype)


def _layernorm(x, g, b):
    xf = x.astype(jnp.float32)
    xc = xf - jnp.mean(xf, axis=-1, keepdims=True)
    y = xc * lax.rsqrt(jnp.mean(xc * xc, axis=-1, keepdims=True) + EPS)
    return (y * g.astype(jnp.float32) + b.astype(jnp.float32)).astype(x.dtype)


def _l2norm(x):
    return x * lax.rsqrt(jnp.sum(x * x, axis=-1, keepdims=True) + EPS)


def _rope_2d(x):
    n = x.shape[1]
    rows = n // GRID_W
    row = jnp.repeat(jnp.arange(rows), GRID_W)
    col = jnp.tile(jnp.arange(GRID_W), rows)
    half = HEAD_DIM // 2
    nf = half // 2
    inv_freq = ROPE_BASE ** (-jnp.arange(nf, dtype=jnp.float32) / nf)
    xf = x.astype(jnp.float32)

    def rot(xa, pos):
        ang = pos.astype(jnp.float32)[:, None] * inv_freq[None, :]
        cos = jnp.cos(ang)[None, :, None, :]
        sin = jnp.sin(ang)[None, :, None, :]
        x1, x2 = xa[..., :nf], xa[..., nf:]
        return jnp.concatenate([x1 * cos - x2 * sin, x2 * cos + x1 * sin], axis=-1)

    out = jnp.concatenate([rot(xf[..., :half], row), rot(xf[..., half:], col)], axis=-1)
    return out.astype(x.dtype)


def _attend(q, k, v, mask, sink):
    s = jnp.einsum('bqhgd,bkhd->bhgqk', q, k).astype(jnp.float32) * (HEAD_DIM ** -0.5)
    s = jnp.where(mask, s, NEG_INF)
    sk = sink.astype(jnp.float32)[None, :, :, None, None]
    m = jnp.maximum(jnp.max(s, axis=-1, keepdims=True), sk)
    p = jnp.exp(s - m)
    p = p / (jnp.sum(p, axis=-1, keepdims=True) + jnp.exp(sk - m))
    return jnp.einsum('bhgqk,bkhd->bqhgd', p.astype(v.dtype), v)


def _context_attention(q, k, v, sink):
    b, s = q.shape[:2]
    nb = s // ATTN_BLOCK
    qb = jnp.moveaxis(q.reshape(b, nb, ATTN_BLOCK, A_KV_HEADS, A_GROUP, HEAD_DIM), 1, 0)
    mask = jnp.ones((ATTN_BLOCK, s), dtype=bool)
    o = lax.map(lambda qi: _attend(qi, k, v, mask, sink), qb)
    return jnp.moveaxis(o, 0, 1).reshape(b, s, A_WIDTH)


def _latent_attention(q, k, v, k_ctx, v_ctx, sink):
    b, n = q.shape[:2]
    nb = n // ATTN_BLOCK
    n_ctx = k_ctx.shape[1]
    pad = ((0, 0), (ATTN_BLOCK, ATTN_BLOCK), (0, 0), (0, 0))
    k_pad = jnp.pad(k, pad)
    v_pad = jnp.pad(v, pad)
    q_rel = jnp.arange(ATTN_BLOCK)
    k_rel = jnp.arange(3 * ATTN_BLOCK) - ATTN_BLOCK
    band = jnp.abs(q_rel[:, None] - k_rel[None, :]) <= WINDOW
    ctx_ok = jnp.ones((ATTN_BLOCK, n_ctx), dtype=bool)

    def block(i):
        start = i * ATTN_BLOCK
        qi = lax.dynamic_slice_in_dim(q, start, ATTN_BLOCK, axis=1)
        ki = lax.dynamic_slice_in_dim(k_pad, start, 3 * ATTN_BLOCK, axis=1)
        vi = lax.dynamic_slice_in_dim(v_pad, start, 3 * ATTN_BLOCK, axis=1)
        kpos = start + k_rel
        mask = band & ((kpos >= 0) & (kpos < n))[None, :]
        mask = jnp.concatenate([mask, ctx_ok], axis=1)
        keys = jnp.concatenate([ki, k_ctx.astype(ki.dtype)], axis=1)
        vals = jnp.concatenate([vi, v_ctx.astype(vi.dtype)], axis=1)
        return _attend(qi, keys, vals, mask, sink)

    o = lax.map(block, jnp.arange(nb))
    return jnp.moveaxis(o, 0, 1).reshape(b, n, A_WIDTH)


def _sgu(u, v, ln_g, ln_b, w_s, b_s):
    b, n = u.shape[:2]
    nc = n // SGU_CHUNK
    vg = _layernorm(v.reshape(b, n, B_GROUPS, B_GROUP_DIM),
                    ln_g.reshape(B_GROUPS, B_GROUP_DIM), ln_b.reshape(B_GROUPS, B_GROUP_DIM))
    vc = vg.reshape(b, nc, SGU_CHUNK, B_GROUPS, B_GROUP_DIM)
    mixed = jnp.einsum('gts,bnsgc->bntgc', w_s, vc) + jnp.swapaxes(b_s, 0, 1)[:, :, None]
    return u * mixed.reshape(b, n, B_WIDTH)


def _short_conv(x, w):
    ch = x.shape[-1]
    y = lax.conv_general_dilated(x, w.astype(x.dtype)[:, None, :], window_strides=(1,),
                                 padding=[(CONV_K // 2, CONV_K // 2)],
                                 dimension_numbers=('NWC', 'WIO', 'NWC'),
                                 feature_group_count=ch)
    return jax.nn.silu(y)


def _gdn_features(cq, ck, cv, ca, cb, conv_w, a_log, dt_bias):
    b, t = cq.shape[:2]
    qkv = _short_conv(jnp.concatenate([cq, ck, cv], axis=-1), conv_w).astype(jnp.float32)
    q, k, v = jnp.split(qkv, 3, axis=-1)
    q = _l2norm(q.reshape(b, t, C_HEADS, C_HEAD_DIM)) * (C_HEAD_DIM ** -0.5)
    k = _l2norm(k.reshape(b, t, C_HEADS, C_HEAD_DIM))
    v = v.reshape(b, t, C_HEADS, C_HEAD_DIM)
    a = ca.astype(jnp.float32).reshape(b, t, N_DIRS, C_HEADS)
    g = -jnp.exp(a_log.astype(jnp.float32)) * jax.nn.softplus(a + dt_bias.astype(jnp.float32))
    beta = jax.nn.sigmoid(cb.astype(jnp.float32).reshape(b, t, N_DIRS, C_HEADS))
    return q, k, v, g, beta


def _gdn_chunked(q, k, v, g, beta, s0):
    b, t, h, dk = q.shape
    dv = v.shape[-1]
    c = GDN_CHUNK
    n = t // c
    q = q.reshape(b, n, c, h, dk)
    k = k.reshape(b, n, c, h, dk)
    v = v.reshape(b, n, c, h, dv)
    beta = beta.reshape(b, n, c, h)
    gc = jnp.cumsum(g.reshape(b, n, c, h), axis=2)
    gch = jnp.moveaxis(gc, 3, 2)
    idx = jnp.arange(c)
    lower = idx[:, None] >= idx[None, :]
    strict = idx[:, None] > idx[None, :]
    decay = jnp.exp(jnp.where(lower, gch[..., :, None] - gch[..., None, :], NEG_INF))
    kb = k * beta[..., None]
    a_mat = jnp.where(strict, jnp.einsum('bnihd,bnjhd->bnhij', kb, k) * decay, 0.0)
    eye = jnp.eye(c, dtype=jnp.float32)
    t_mat = lax.linalg.triangular_solve(a_mat + eye, jnp.broadcast_to(eye, a_mat.shape),
                                        left_side=True, lower=True, unit_diagonal=True)
    u = jnp.einsum('bnhij,bnjhd->bnihd', t_mat, v * beta[..., None])
    w = jnp.einsum('bnhij,bnjhd->bnihd', t_mat, kb * jnp.exp(gc)[..., None])
    a_qk = jnp.einsum('bnihd,bnjhd->bnhij', q, k) * decay
    g_last = gc[:, :, -1]
    q_dec = q * jnp.exp(gc)[..., None]
    k_dec = k * jnp.exp(g_last[:, :, None, :] - gc)[..., None]

    def step(state, xs):
        qd, kd, uc, wc, aqk, gl = xs
        v_new = uc - jnp.einsum('bihk,bhkv->bihv', wc, state)
        o = jnp.einsum('bihk,bhkv->bihv', qd, state) + jnp.einsum('bhij,bjhv->bihv', aqk, v_new)
        state = state * jnp.exp(gl)[:, :, None, None] + jnp.einsum('bjhk,bjhv->bhkv', kd, v_new)
        return state, o

    xs = tuple(jnp.moveaxis(a, 1, 0) for a in (q_dec, k_dec, u, w, a_qk, g_last))
    s_fin, o = lax.scan(step, s0.astype(jnp.float32), xs)
    o = jnp.moveaxis(o, 0, 1).reshape(b, t, h, dv)
    return o, s_fin


def _gdn_bidir(q, k, v, g, beta, s0):
    flip = lambda a: jnp.flip(a, axis=1)
    o_f, s_f = _gdn_chunked(q, k, v, g[:, :, 0], beta[:, :, 0], s0[:, 0])
    o_b, s_b = _gdn_chunked(flip(q), flip(k), flip(v), flip(g[:, :, 1]), flip(beta[:, :, 1]), s0[:, 1])
    return o_f + flip(o_b), jnp.stack([s_f, s_b], axis=1)


def _gdn_output(o, norm_g, gate):
    b, t = o.shape[:2]
    o = _rmsnorm(o, norm_g).reshape(b, t, C_WIDTH).astype(gate.dtype)
    return o * jax.nn.silu(gate)


def _modulated_proj(x, cond, w_mod, b_mod, g_pre, w_in):
    mod = jax.nn.silu(cond) @ w_mod + b_mod
    shift, scale, gate = jnp.split(mod, 3, axis=-1)
    h = _rmsnorm(x, g_pre) * (1.0 + scale[:, None, :]) + shift[:, None, :]
    z = h @ w_in
    offs, acc = [], 0
    for width in IN_SPLITS[:-1]:
        acc += width
        offs.append(acc)
    return jnp.split(z, offs, axis=-1), gate


def _residual(x, outs, gate, w_out, g_post):
    mix = jnp.concatenate(outs, axis=-1) @ w_out
    return x + gate[:, None, :] * _rmsnorm(mix, g_post)


def _context_layer(x, c_ctx, prm):
    (w_mod, b_mod, g_pre, w_in, sink, ln_g, ln_b, sgu_w, sgu_b,
     conv_w, a_log, dt_bias, norm_g, g_post, w_out) = prm
    b, s = x.shape[:2]
    parts, gate = _modulated_proj(x, c_ctx[None, :], w_mod, b_mod, g_pre, w_in)
    aq, ak, av, ag, bu, bv, bg, cq, ck, cv, ca, cb, cg = parts
    q = aq.reshape(b, s, A_KV_HEADS, A_GROUP, HEAD_DIM)
    k = ak.reshape(b, s, A_KV_HEADS, HEAD_DIM)
    v = av.reshape(b, s, A_KV_HEADS, HEAD_DIM)
    o_a = _context_attention(q, k, v, sink.reshape(A_KV_HEADS, A_GROUP)) * jax.nn.silu(ag)
    o_b = _sgu(bu, bv, ln_g, ln_b, sgu_w, sgu_b) * jax.nn.silu(bg)
    q_c, k_c, v_c, g_c, beta_c = _gdn_features(cq, ck, cv, ca, cb, conv_w, a_log, dt_bias)
    s0 = jnp.zeros((b, N_DIRS, C_HEADS, C_HEAD_DIM, C_HEAD_DIM), jnp.float32)
    o_c, s_fin = _gdn_bidir(q_c, k_c, v_c, g_c, beta_c, s0)
    o_c = _gdn_output(o_c, norm_g, cg)
    return _residual(x, [o_a, o_b, o_c], gate, w_out, g_post), k, v, s_fin


def _latent_layer(x, c, prm, k_ctx, v_ctx, s_ctx):
    (w_mod, b_mod, g_pre, w_in, sink, ln_g, ln_b, sgu_w, sgu_b,
     conv_w, a_log, dt_bias, norm_g, g_post, w_out) = prm
    b, n = x.shape[:2]
    parts, gate = _modulated_proj(x, c, w_mod, b_mod, g_pre, w_in)
    aq, ak, av, ag, bu, bv, bg, cq, ck, cv, ca, cb, cg = parts
    q = _rope_2d(aq.reshape(b, n, A_HEADS, HEAD_DIM)).reshape(b, n, A_KV_HEADS, A_GROUP, HEAD_DIM)
    k = _rope_2d(ak.reshape(b, n, A_KV_HEADS, HEAD_DIM))
    v = av.reshape(b, n, A_KV_HEADS, HEAD_DIM)
    o_a = _latent_attention(q, k, v, k_ctx, v_ctx, sink.reshape(A_KV_HEADS, A_GROUP)) * jax.nn.silu(ag)
    o_b = _sgu(bu, bv, ln_g, ln_b, sgu_w, sgu_b) * jax.nn.silu(bg)
    q_c, k_c, v_c, g_c, beta_c = _gdn_features(cq, ck, cv, ca, cb, conv_w, a_log, dt_bias)
    o_c, _ = _gdn_bidir(q_c, k_c, v_c, g_c, beta_c, s_ctx)
    o_c = _gdn_output(o_c, norm_g, cg)
    return _residual(x, [o_a, o_b, o_c], gate, w_out, g_post)


def setup_inputs(seed: int = 0) -> dict:
    key = jax.random.key(seed)
    ks = jax.random.split(key, 24)
    f32 = jnp.float32

    def nrm(k, shape, s):
        return s * jax.random.normal(k, shape, f32)

    dt = jnp.exp(jax.random.uniform(ks[18], (DEPTH, N_DIRS, C_HEADS), f32,
                                    math.log(1e-3), math.log(1e-1)))
    return {
        'x_prompt': nrm(ks[0], (BATCH, SEQ, D_MODEL), 1.0),
        'x_sample': nrm(ks[1], (DEC_BATCH, DEC_SEQ, D_MODEL), 1.0),
        'cache_k': nrm(ks[2], (DEC_BATCH, DEPTH, PAST_LEN, A_KV_HEADS, HEAD_DIM), 1.0),
        'cache_v': nrm(ks[3], (DEC_BATCH, DEPTH, PAST_LEN, A_KV_HEADS, HEAD_DIM), 1.0),
        'state_delta': nrm(ks[4], (DEC_BATCH, DEPTH, N_DIRS, C_HEADS, C_HEAD_DIM, C_HEAD_DIM), 0.2),
        'c': nrm(ks[5], (DEC_BATCH, D_MODEL), 1.0),
        'c_ctx': nrm(ks[6], (D_MODEL,), 1.0),
        'w_mod': nrm(ks[7], (DEPTH, D_MODEL, 3 * D_MODEL), 0.5 * D_MODEL ** -0.5),
        'b_mod': nrm(ks[8], (DEPTH, 3 * D_MODEL), 0.01),
        'g_pre': 1.0 + nrm(ks[9], (DEPTH, D_MODEL), 0.05),
        'w_in': nrm(ks[10], (DEPTH, D_MODEL, IN_COLS), D_MODEL ** -0.5),
        'attn_sink': nrm(ks[11], (DEPTH, A_HEADS), 0.5),
        'sgu_ln_g': 1.0 + nrm(ks[12], (DEPTH, B_WIDTH), 0.05),
        'sgu_ln_b': nrm(ks[13], (DEPTH, B_WIDTH), 0.01),
        'sgu_w': nrm(ks[14], (DEPTH, B_GROUPS, SGU_CHUNK, SGU_CHUNK), 0.5 * SGU_CHUNK ** -0.5),
        'sgu_b': 1.0 + nrm(ks[15], (DEPTH, B_GROUPS, SGU_CHUNK), 0.05),
        'gdn_conv_w': nrm(ks[16], (DEPTH, CONV_K, 3 * C_WIDTH), CONV_K ** -0.5),
        'gdn_a_log': jnp.log(jax.random.uniform(ks[17], (DEPTH, N_DIRS, C_HEADS), f32, 1.0, 16.0)),
        'gdn_dt_bias': dt + jnp.log(-jnp.expm1(-dt)),
        'gdn_norm_g': 1.0 + nrm(ks[19], (DEPTH, C_HEAD_DIM), 0.05),
        'g_post': 1.0 + nrm(ks[20], (DEPTH, D_MODEL), 0.05),
        'w_out': nrm(ks[21], (DEPTH, D_MIX, D_MODEL), D_MIX ** -0.5),
    }


def reference(x_prompt, x_sample, cache_k, cache_v, state_delta, c, c_ctx,
              w_mod, b_mod, g_pre, w_in, attn_sink, sgu_ln_g, sgu_ln_b, sgu_w, sgu_b,
              gdn_conv_w, gdn_a_log, gdn_dt_bias, gdn_norm_g, g_post, w_out):
    xp = x_prompt
    xs = x_sample
    new_k, new_v, new_s = [], [], []
    for l in range(DEPTH):
        prm = (w_mod[l], b_mod[l], g_pre[l], w_in[l], attn_sink[l], sgu_ln_g[l], sgu_ln_b[l],
               sgu_w[l], sgu_b[l], gdn_conv_w[l], gdn_a_log[l], gdn_dt_bias[l], gdn_norm_g[l],
               g_post[l], w_out[l])
        xp, k_l, v_l, s_l = _context_layer(xp, c_ctx, prm)
        xs = _latent_layer(xs, c, prm, cache_k[:, l], cache_v[:, l], state_delta[:, l])
        new_k.append(k_l)
        new_v.append(v_l)
        new_s.append(s_l)
    new_cache_k = jnp.stack(new_k, axis=1)
    new_cache_v = jnp.stack(new_v, axis=1)
    new_state_delta = jnp.stack(new_s, axis=1)
    return (xp, xs, new_cache_k, new_cache_v, new_state_delta)
```

```python
import functools

import numpy as np
import jax
import jax.numpy as jnp
from jax import lax
from jax.experimental import pallas as pl
from jax.experimental.pallas import tpu as pltpu

F32 = jnp.float32
BF16 = jnp.bfloat16

GRID_W = 64
HEAD_DIM = 64
A_HEADS = 8
A_KV_HEADS = 2
A_GROUP = A_HEADS // A_KV_HEADS
A_WIDTH = A_HEADS * HEAD_DIM
A_KV_WIDTH = A_KV_HEADS * HEAD_DIM
ATTN_BLOCK = 128
ROPE_BASE = 10000.0
B_GROUPS = 4
B_GROUP_DIM = 64
B_WIDTH = B_GROUPS * B_GROUP_DIM
SGU_CHUNK = 128
C_HEADS = 4
C_HEAD_DIM = 64
C_WIDTH = C_HEADS * C_HEAD_DIM
CONV_K = 5
GDN_CHUNK = 64
N_DIRS = 2
N_GATES = N_DIRS * C_HEADS
EPS = 1e-6
NEG_INF = -1e30

LANES = 128
BF16_SUBLANES = 16
VMEM_LIMIT_BYTES = 56 * 1024 * 1024

_OFF_Q = 0
_OFF_K = _OFF_Q + A_WIDTH
_OFF_V = _OFF_K + A_KV_WIDTH
_OFF_AG = _OFF_V + A_KV_WIDTH
_OFF_BU = _OFF_AG + A_WIDTH
_OFF_BV = _OFF_BU + B_WIDTH
_OFF_BG = _OFF_BV + B_WIDTH
_OFF_CQKV = _OFF_BG + B_WIDTH
_OFF_CG = _OFF_CQKV + 3 * C_WIDTH
_OFF_AB = _OFF_CG + C_WIDTH
_W1_COLS = _OFF_AB + LANES

_G_GC = 0
_G_BETA = N_GATES
_G_BEGC = 2 * N_GATES
_G_EGC = 3 * N_GATES
_G_EGLGC = 4 * N_GATES
_G_EGL = 5 * N_GATES


def _dot(a, b):
    return jnp.dot(a, b, preferred_element_type=F32)


def _dot_nt(a, b):
    return lax.dot_general(a, b, (((1,), (1,)), ((), ())), preferred_element_type=F32)


def _dot_tn(a, b):
    return lax.dot_general(a, b, (((0,), (0,)), ((), ())), preferred_element_type=F32)


def _split3(x):
    hi = x.astype(BF16)
    r1 = x - hi.astype(F32)
    mid = r1.astype(BF16)
    lo = (r1 - mid.astype(F32)).astype(BF16)
    return hi, mid, lo


def _seg_sum(x, seg):
    hi, mid, lo = _split3(x)
    return _dot(hi, seg) + _dot(mid, seg) + _dot(lo, seg)


def _tri_sum(tri, x):
    hi, mid, lo = _split3(x)
    return _dot(tri, hi) + _dot(tri, mid) + _dot(tri, lo)


def _sigmoid(x):
    return 1.0 / (1.0 + jnp.exp(-x))


def _silu(x):
    return x * _sigmoid(x)


def _softplus(x):
    return jnp.maximum(x, 0.0) + jnp.log1p(jnp.exp(-jnp.abs(x)))


def _mod_body(cond_ref, w_ref, b_ref, o_ref):
    a = _silu(cond_ref[...]).astype(BF16)
    o_ref[...] = _dot(a, w_ref[...].astype(BF16)) + b_ref[...]


def _modulation(conds, w_mod, b_mod):
    depth, d, d3 = w_mod.shape
    rows = conds.shape[0]
    tn = d
    return pl.pallas_call(
        _mod_body,
        grid=(depth, d3 // tn),
        in_specs=[
            pl.BlockSpec((rows, d), lambda l, j: (0, 0)),
            pl.BlockSpec((None, d, tn), lambda l, j: (l, 0, j)),
            pl.BlockSpec((None, 1, tn), lambda l, j: (l, 0, j)),
        ],
        out_specs=pl.BlockSpec((None, rows, tn), lambda l, j: (l, 0, j)),
        out_shape=jax.ShapeDtypeStruct((depth, rows, d3), F32),
        compiler_params=pltpu.CompilerParams(dimension_semantics=("arbitrary", "arbitrary"),
                                             vmem_limit_bytes=VMEM_LIMIT_BYTES),
        name="modulation",
    )(conds, w_mod, b_mod.reshape(depth, 1, d3))


def _rope(z, c, sa, sb):
    w = z.shape[1]
    reps = w // LANES
    if reps > 1:
        c = jnp.concatenate([c] * reps, axis=1)
        sa = jnp.concatenate([sa] * reps, axis=1)
        sb = jnp.concatenate([sb] * reps, axis=1)
    half_pair = HEAD_DIM // 4
    return z * c + pltpu.roll(z, w - half_pair, 1) * sa + pltpu.roll(z, half_pair, 1) * sb


def _proj_body(*refs, rope, keep_f32_kv, tm):
    it = iter(refs)
    x_ref, mod_ref, gpre_ref, w_ref, seg_ref, lng_ref, lnb_ref, sw_ref, sbias_ref = (next(it) for _ in range(9))
    if rope:
        rc_ref, rsa_ref, rsb_ref = (next(it) for _ in range(3))
    q_ref, kv_ref, sag_ref, ob_ref, cqkv_ref, scg_ref, ab_ref = (next(it) for _ in range(7))
    if keep_f32_kv:
        kf_ref, vf_ref = (next(it) for _ in range(2))

    x = x_ref[...]
    ms = jnp.mean(x * x, axis=-1, keepdims=True)
    h = x * lax.rsqrt(ms + EPS) * gpre_ref[...]
    h = h * (1.0 + mod_ref[1:2, :]) + mod_ref[0:1, :]
    hb = h.astype(BF16)

    def proj(lo, width):
        return _dot(hb, w_ref[:, lo:lo + width])

    zq = proj(_OFF_Q, A_WIDTH)
    zk = proj(_OFF_K, A_KV_WIDTH)
    zv = proj(_OFF_V, A_KV_WIDTH)
    if keep_f32_kv:
        kf_ref[...] = zk
        vf_ref[...] = zv
    if rope:
        c, sa, sb = rc_ref[...], rsa_ref[...], rsb_ref[...]
        zq = _rope(zq, c, sa, sb)
        zk = _rope(zk, c, sa, sb)
    q_ref[...] = (zq * (HEAD_DIM ** -0.5)).astype(BF16)
    kv_ref[:, 0:A_KV_WIDTH] = zk.astype(BF16)
    kv_ref[:, A_KV_WIDTH:2 * A_KV_WIDTH] = zv.astype(BF16)
    sag_ref[...] = _silu(proj(_OFF_AG, A_WIDTH)).astype(BF16)

    zu = proj(_OFF_BU, B_WIDTH)
    zbv = proj(_OFF_BV, B_WIDTH)
    zbg = proj(_OFF_BG, B_WIDTH)
    seg = seg_ref[...]
    inv_n = 1.0 / B_GROUP_DIM
    mean = _seg_sum(zbv, seg) * inv_n
    xc = zbv - mean
    var = _seg_sum(xc * xc, seg) * inv_n
    vg = (xc * lax.rsqrt(var + EPS) * lng_ref[...] + lnb_ref[...]).astype(BF16)
    sbias = sbias_ref[...]
    mixed_chunks = []
    for ci in range(tm // SGU_CHUNK):
        rows = slice(ci * SGU_CHUNK, (ci + 1) * SGU_CHUNK)
        per_group = [_dot(sw_ref[g], vg[rows, g * B_GROUP_DIM:(g + 1) * B_GROUP_DIM]) for g in range(B_GROUPS)]
        mixed_chunks.append(jnp.concatenate(per_group, axis=1) + sbias)
    mixed = jnp.concatenate(mixed_chunks, axis=0) if len(mixed_chunks) > 1 else mixed_chunks[0]
    ob_ref[...] = (zu * mixed * _silu(zbg)).astype(BF16)

    cqkv_ref[...] = proj(_OFF_CQKV, 3 * C_WIDTH).astype(BF16)
    scg_ref[...] = _silu(proj(_OFF_CG, C_WIDTH)).astype(BF16)
    ab_ref[...] = proj(_OFF_AB, LANES)[:, 0:2 * N_GATES]


def _projection(x, mod3, gpre, w1, seg, lng, lnb, sw, sbias, rope_tabs, *, keep_f32_kv, tm):
    b, t, d = x.shape
    rope = rope_tabs is not None
    nmod = mod3.shape[0]
    mod_map = (lambda bi, i: (bi, 0, 0)) if nmod > 1 else (lambda bi, i: (0, 0, 0))
    const2 = lambda bi, i: (0, 0)
    in_specs = [
        pl.BlockSpec((None, tm, d), lambda bi, i: (bi, i, 0)),
        pl.BlockSpec((None, 3, d), mod_map),
        pl.BlockSpec((1, d), const2),
        pl.BlockSpec(w1.shape, const2),
        pl.BlockSpec(seg.shape, const2),
        pl.BlockSpec((1, B_WIDTH), const2),
        pl.BlockSpec((1, B_WIDTH), const2),
        pl.BlockSpec(sw.shape, lambda bi, i: (0, 0, 0)),
        pl.BlockSpec(sbias.shape, const2),
    ]
    args = [x, mod3, gpre, w1, seg, lng, lnb, sw, sbias]
    if rope:
        in_specs += [pl.BlockSpec((tm, LANES), lambda bi, i: (i, 0))] * 3
        args += list(rope_tabs)

    def tok(width, dtype):
        return (pl.BlockSpec((None, tm, width), lambda bi, i: (bi, i, 0)),
                jax.ShapeDtypeStruct((b, t, width), dtype))

    outs = [tok(A_WIDTH, BF16), tok(2 * A_KV_WIDTH, BF16), tok(A_WIDTH, BF16), tok(B_WIDTH, BF16),
            tok(3 * C_WIDTH, BF16), tok(C_WIDTH, BF16), tok(2 * N_GATES, F32)]
    if keep_f32_kv:
        outs += [tok(A_KV_WIDTH, F32), tok(A_KV_WIDTH, F32)]
    return pl.pallas_call(
        functools.partial(_proj_body, rope=rope, keep_f32_kv=keep_f32_kv, tm=tm),
        grid=(b, t // tm),
        in_specs=in_specs,
        out_specs=[o[0] for o in outs],
        out_shape=[o[1] for o in outs],
        compiler_params=pltpu.CompilerParams(dimension_semantics=("arbitrary", "arbitrary"),
                                             vmem_limit_bytes=VMEM_LIMIT_BYTES),
        name="projection_rope" if rope else "projection",
    )(*args)


def _pair_masks(n):
    rows = lax.broadcasted_iota(jnp.int32, (n, n), 0)
    cols = lax.broadcasted_iota(jnp.int32, (n, n), 1)
    masks = []
    level = 0
    while (1 << level) < n:
        same_pair = (rows >> (level + 1)) == (cols >> (level + 1))
        other_half = (rows >> level) != (cols >> level)
        masks.append(same_pair & other_half)
        level += 1
    return masks, rows == cols


def _unit_tri_inverse(a, masks, eye):
    t_mat = jnp.where(eye, 1.0, 0.0) - jnp.where(masks[0], a, 0.0)
    for mask in masks[1:]:
        tb = t_mat.astype(BF16)
        e = jnp.where(mask, a, 0.0).astype(BF16)
        t_mat = t_mat - _dot(tb, _dot(e, tb).astype(BF16))
    return t_mat


def _gdn_body(*refs, t, has_s0, emit_state):
    it = iter(refs)
    cqkv_ref, ab_ref, scg_ref, convw_ref, alog_ref, dtb_ref, normg_ref, seg_ref = (next(it) for _ in range(8))
    s0_ref = next(it) if has_s0 else None
    oc_ref = next(it)
    sfin_ref = next(it) if emit_state else None
    qkv_scr, gate_scr, grow_scr, od_scr, s_scr = (next(it) for _ in range(5))

    c = GDN_CHUNK
    nch = t // c
    halo = BF16_SUBLANES
    rows = lax.broadcasted_iota(jnp.int32, (c, c), 0)
    cols = lax.broadcasted_iota(jnp.int32, (c, c), 1)
    lower = rows >= cols
    upper = rows <= cols
    tri_l = jnp.where(lower, 1.0, 0.0).astype(BF16)
    tri_u = jnp.where(upper, 1.0, 0.0).astype(BF16)
    pair_masks, eye = _pair_masks(c)
    gate_lane = lax.broadcasted_iota(jnp.int32, (1, N_GATES), 1)
    is_fwd = gate_lane < C_HEADS
    seg = seg_ref[...]
    convw = convw_ref[...]
    neg_rate = -jnp.exp(alog_ref[...])
    dtb = dtb_ref[...]

    def prep_chunk(n, first, last):
        r0 = n * c
        if first:
            win = jnp.concatenate([jnp.zeros((halo, 3 * C_WIDTH), F32),
                                   cqkv_ref[0:c + halo, :].astype(F32)], axis=0)
        elif last:
            win = jnp.concatenate([cqkv_ref[t - c - halo:t, :].astype(F32),
                                   jnp.zeros((halo, 3 * C_WIDTH), F32)], axis=0)
        else:
            r0 = pl.multiple_of(r0, c)
            win = cqkv_ref[pl.ds(r0 - halo, c + 2 * halo), :].astype(F32)
        y = jnp.zeros((c, 3 * C_WIDTH), F32)
        for j in range(CONV_K):
            lo = halo - CONV_K // 2 + j
            y = y + win[lo:lo + c, :] * convw[j:j + 1, :]
        y = _silu(y)
        q = y[:, 0:C_WIDTH]
        k = y[:, C_WIDTH:2 * C_WIDTH]
        v = y[:, 2 * C_WIDTH:3 * C_WIDTH]
        q = q * lax.rsqrt(_seg_sum(q * q, seg) + EPS) * (C_HEAD_DIM ** -0.5)
        k = k * lax.rsqrt(_seg_sum(k * k, seg) + EPS)
        qkv_scr[pl.ds(r0, c), 0:C_WIDTH] = q
        qkv_scr[pl.ds(r0, c), C_WIDTH:2 * C_WIDTH] = k
        qkv_scr[pl.ds(r0, c), 2 * C_WIDTH:3 * C_WIDTH] = v

        ab = ab_ref[pl.ds(r0, c), :]
        g = neg_rate * _softplus(ab[:, 0:N_GATES] + dtb)
        beta = _sigmoid(ab[:, N_GATES:2 * N_GATES])
        gc_f = _tri_sum(tri_l, g)
        gc_b = _tri_sum(tri_u, g)
        gc = jnp.where(is_fwd, gc_f, gc_b)
        gl = jnp.where(is_fwd, gc_f[c - 1:c, :], gc_b[0:1, :])
        egc = jnp.exp(gc)
        gate_scr[pl.ds(r0, c), _G_GC:_G_GC + N_GATES] = gc
        gate_scr[pl.ds(r0, c), _G_BETA:_G_BETA + N_GATES] = beta
        gate_scr[pl.ds(r0, c), _G_BEGC:_G_BEGC + N_GATES] = beta * egc
        gate_scr[pl.ds(r0, c), _G_EGC:_G_EGC + N_GATES] = egc
        gate_scr[pl.ds(r0, c), _G_EGLGC:_G_EGLGC + N_GATES] = jnp.exp(gl - gc)
        gate_scr[pl.ds(r0, c), _G_EGL:_G_EGL + N_GATES] = jnp.broadcast_to(jnp.exp(gl), (c, N_GATES))
        grow_scr[n] = gc.T

    prep_chunk(0, True, False)
    if nch > 2:
        def prep_step(n, carry):
            prep_chunk(n, False, False)
            return carry
        lax.fori_loop(1, nch - 1, prep_step, 0)
    prep_chunk(nch - 1, False, True)

    for d in range(N_DIRS):
        for hh in range(C_HEADS):
            if has_s0:
                s_scr[d, hh] = s0_ref[d, hh]
            else:
                s_scr[d, hh] = jnp.zeros((C_HEAD_DIM, C_HEAD_DIM), F32)

    def scan_step(n, carry):
        for d in range(N_DIRS):
            m = n if d == 0 else nch - 1 - n
            r0 = pl.multiple_of(m * c, c)
            incl = lower if d == 0 else upper
            strict = (rows > cols) if d == 0 else (rows < cols)
            qc = qkv_scr[pl.ds(r0, c), 0:C_WIDTH]
            kc = qkv_scr[pl.ds(r0, c), C_WIDTH:2 * C_WIDTH]
            vc = qkv_scr[pl.ds(r0, c), 2 * C_WIDTH:3 * C_WIDTH]
            gates = gate_scr[pl.ds(r0, c), :]
            grow = grow_scr[m]
            outs = []
            for hh in range(C_HEADS):
                gi = d * C_HEADS + hh
                hs = slice(hh * C_HEAD_DIM, (hh + 1) * C_HEAD_DIM)
                q, k, v = qc[:, hs], kc[:, hs], vc[:, hs]

                def col(base):
                    return gates[:, base + gi:base + gi + 1]

                kb16 = k.astype(BF16)
                kk = _dot_nt(kb16, kb16)
                qk = _dot_nt(q.astype(BF16), kb16)
                decay = jnp.where(incl, jnp.exp(col(_G_GC) - grow[gi:gi + 1, :]), 0.0)
                beta = col(_G_BETA)
                a_mat = jnp.where(strict, kk * beta * decay, 0.0)
                t_mat = _unit_tri_inverse(a_mat, pair_masks, eye)
                rhs = jnp.concatenate([v * beta, k * col(_G_BEGC)], axis=1).astype(BF16)
                uw = _dot(t_mat.astype(BF16), rhs)
                u = uw[:, 0:C_HEAD_DIM]
                w = uw[:, C_HEAD_DIM:2 * C_HEAD_DIM]
                a_qk = (qk * decay).astype(BF16)
                q_dec = q * col(_G_EGC)
                k_dec = (k * col(_G_EGLGC)).astype(BF16)

                state = s_scr[d, hh]
                sb = state.astype(BF16)
                ws_qs = _dot(jnp.concatenate([w, q_dec], axis=0).astype(BF16), sb)
                v_new = u - ws_qs[0:c, :]
                vnb = v_new.astype(BF16)
                outs.append(ws_qs[c:2 * c, :] + _dot(a_qk, vnb))
                s_scr[d, hh] = state * col(_G_EGL) + _dot_tn(k_dec, vnb)
            od_scr[d, pl.ds(r0, c), :] = jnp.concatenate(outs, axis=1)
        return carry

    lax.fori_loop(0, nch, scan_step, 0)

    blk = 4 * c
    inv_n = 1.0 / C_HEAD_DIM

    def out_step(i, carry):
        r0 = pl.multiple_of(i * blk, blk)
        o = od_scr[0, pl.ds(r0, blk), :] + od_scr[1, pl.ds(r0, blk), :]
        ms = _seg_sum(o * o, seg) * inv_n
        on = o * lax.rsqrt(ms + EPS) * normg_ref[...]
        oc_ref[pl.ds(r0, blk), :] = (on * scg_ref[pl.ds(r0, blk), :].astype(F32)).astype(BF16)
        return carry

    lax.fori_loop(0, t // blk, out_step, 0)

    if emit_state:
        for d in range(N_DIRS):
            for hh in range(C_HEADS):
                sfin_ref[d, hh] = s_scr[d, hh]


def _gdn(cqkv, ab, scg, convw, alog, dtb, normg, seg, s0, *, emit_state):
    b, t, _ = cqkv.shape
    nch = t // GDN_CHUNK
    has_s0 = s0 is not None
    const2 = lambda bi: (0, 0)
    in_specs = [
        pl.BlockSpec((None, t, 3 * C_WIDTH), lambda bi: (bi, 0, 0)),
        pl.BlockSpec((None, t, 2 * N_GATES), lambda bi: (bi, 0, 0)),
        pl.BlockSpec((None, t, C_WIDTH), lambda bi: (bi, 0, 0)),
        pl.BlockSpec(convw.shape, const2),
        pl.BlockSpec((1, N_GATES), const2),
        pl.BlockSpec((1, N_GATES), const2),
        pl.BlockSpec((1, C_WIDTH), const2),
        pl.BlockSpec(seg.shape, const2),
    ]
    args = [cqkv, ab, scg, convw, alog, dtb, normg, seg]
    state_spec = pl.BlockSpec((None, N_DIRS, C_HEADS, C_HEAD_DIM, C_HEAD_DIM), lambda bi: (bi, 0, 0, 0, 0))
    if has_s0:
        in_specs.append(state_spec)
        args.append(s0)
    out_specs = [pl.BlockSpec((None, t, C_WIDTH), lambda bi: (bi, 0, 0))]
    out_shape = [jax.ShapeDtypeStruct((b, t, C_WIDTH), BF16)]
    if emit_state:
        out_specs.append(state_spec)
        out_shape.append(jax.ShapeDtypeStruct((b, N_DIRS, C_HEADS, C_HEAD_DIM, C_HEAD_DIM), F32))
    scratch = [
        pltpu.VMEM((t, 3 * C_WIDTH), F32),
        pltpu.VMEM((t, LANES), F32),
        pltpu.VMEM((nch, N_GATES, GDN_CHUNK), F32),
        pltpu.VMEM((N_DIRS, t, C_WIDTH), F32),
        pltpu.VMEM((N_DIRS, C_HEADS, C_HEAD_DIM, C_HEAD_DIM), F32),
    ]
    return pl.pallas_call(
        functools.partial(_gdn_body, t=t, has_s0=has_s0, emit_state=emit_state),
        grid=(b,),
        in_specs=in_specs,
        out_specs=out_specs,
        out_shape=out_shape,
        scratch_shapes=scratch,
        compiler_params=pltpu.CompilerParams(dimension_semantics=("arbitrary",),
                                             vmem_limit_bytes=VMEM_LIMIT_BYTES),
        name="gdn_state" if emit_state else "gdn",
    )(*args)


def _attn_body(*refs, banded, tq, nblk_total):
    it = iter(refs)
    x_ref, q_ref, kv_ref = (next(it) for _ in range(3))
    if banded:
        kvl_ref, kvr_ref, band_ref = (next(it) for _ in range(3))
        ck_ref, cv_ref = (next(it) for _ in range(2))
    sink_ref, sag_ref, ob_ref, oc_ref, wout_ref, gpost_ref, gate_ref, y_ref = (next(it) for _ in range(8))

    nb = tq // ATTN_BLOCK
    rows_q = A_GROUP * ATTN_BLOCK
    lane = lax.broadcasted_iota(jnp.int32, (1, A_KV_WIDTH), 1)
    head_mask = [lane < HEAD_DIM, lane >= HEAD_DIM]

    def split_heads(a):
        zero = jnp.zeros_like(a)
        return [jnp.where(head_mask[h], a, zero) for h in range(A_KV_HEADS)]

    if banded:
        kv_ext = jnp.concatenate([kvl_ref[...], kv_ref[...], kvr_ref[...]], axis=0)
        k_own = split_heads(kv_ext[:, 0:A_KV_WIDTH])
        v_own = split_heads(kv_ext[:, A_KV_WIDTH:2 * A_KV_WIDTH])
        k_ctx = split_heads(ck_ref[...].astype(BF16))
        v_ctx = split_heads(cv_ref[...].astype(BF16))
        band = band_ref[...]
        first_blk = pl.program_id(1) * nb
        col = lax.broadcasted_iota(jnp.int32, (1, 3 * ATTN_BLOCK), 1)
    else:
        kv_all = kv_ref[...]
        k_ctx = split_heads(kv_all[:, 0:A_KV_WIDTH])
        v_ctx = split_heads(kv_all[:, A_KV_WIDTH:2 * A_KV_WIDTH])

    o_blocks = []
    for j in range(nb):
        qrows = slice(j * ATTN_BLOCK, (j + 1) * ATTN_BLOCK)
        q2 = jnp.concatenate([q_ref[qrows, g * A_KV_WIDTH:(g + 1) * A_KV_WIDTH] for g in range(A_GROUP)], axis=0)
        if banded:
            blk = first_blk + j
            left_ok = jnp.where(blk > 0, 1.0, 0.0)
            right_ok = jnp.where(blk < nblk_total - 1, 1.0, 0.0)
            col_ok = jnp.where(col < ATTN_BLOCK, left_ok, jnp.where(col >= 2 * ATTN_BLOCK, right_ok, 1.0))
            visible = (band * col_ok) > 0.5
            krows = slice(j * ATTN_BLOCK, (j + 3) * ATTN_BLOCK)
        o_blk = jnp.zeros((rows_q, A_KV_WIDTH), F32)
        for h in range(A_KV_HEADS):
            sink = sink_ref[h]
            s_ctx = _dot_nt(q2, k_ctx[h])
            m = jnp.maximum(jnp.max(s_ctx, axis=-1, keepdims=True), sink)
            if banded:
                s_own = jnp.where(visible, _dot_nt(q2, k_own[h][krows]), NEG_INF)
                m = jnp.maximum(m, jnp.max(s_own, axis=-1, keepdims=True))
                p_own = jnp.exp(s_own - m)
            p_ctx = jnp.exp(s_ctx - m)
            denom = jnp.sum(p_ctx, axis=-1, keepdims=True) + jnp.exp(sink - m)
            acc = _dot(p_ctx.astype(BF16), v_ctx[h])
            if banded:
                denom = denom + jnp.sum(p_own, axis=-1, keepdims=True)
                acc = acc + _dot(p_own.astype(BF16), v_own[h][krows])
            o_blk = o_blk + acc * (1.0 / denom)
        o_blocks.append(jnp.concatenate([o_blk[g * ATTN_BLOCK:(g + 1) * ATTN_BLOCK] for g in range(A_GROUP)], axis=1))
    o_a = jnp.concatenate(o_blocks, axis=0) if nb > 1 else o_blocks[0]

    mix_a = (o_a * sag_ref[...].astype(F32)).astype(BF16)
    mix = (_dot(mix_a, wout_ref[0:A_WIDTH, :])
           + _dot(ob_ref[...], wout_ref[A_WIDTH:A_WIDTH + B_WIDTH, :])
           + _dot(oc_ref[...], wout_ref[A_WIDTH + B_WIDTH:A_WIDTH + B_WIDTH + C_WIDTH, :]))
    ms = jnp.mean(mix * mix, axis=-1, keepdims=True)
    y_ref[...] = x_ref[...] + gate_ref[...] * (mix * lax.rsqrt(ms + EPS) * gpost_ref[...])


def _attention_out(x, q, kv, sag, ob, oc, sink_rows, wout, gpost, gate, band, ctx_k, ctx_v, layer, *, tq):
    b, t, d = x.shape
    banded = band is not None
    nblk_total = t // ATTN_BLOCK
    per_q = tq // ATTN_BLOCK
    ngate = gate.shape[0]
    gate_map = (lambda bi, i: (bi, 0, 0)) if ngate > 1 else (lambda bi, i: (0, 0, 0))
    const2 = lambda bi, i: (0, 0)

    def tok(width):
        return pl.BlockSpec((None, tq, width), lambda bi, i: (bi, i, 0))

    in_specs = [tok(d), tok(A_WIDTH)]
    args = [x, q]
    if banded:
        in_specs.append(tok(2 * A_KV_WIDTH))
        args.append(kv)
        in_specs.append(pl.BlockSpec((None, ATTN_BLOCK, 2 * A_KV_WIDTH),
                                     lambda bi, i: (bi, jnp.maximum(i * per_q - 1, 0), 0)))
        in_specs.append(pl.BlockSpec((None, ATTN_BLOCK, 2 * A_KV_WIDTH),
                                     lambda bi, i: (bi, jnp.minimum((i + 1) * per_q, nblk_total - 1), 0)))
        in_specs.append(pl.BlockSpec(band.shape, const2))
        args += [kv, kv, band]
        past = ctx_k.shape[2]
        in_specs += [pl.BlockSpec((None, None, past, A_KV_WIDTH), lambda bi, i: (bi, layer, 0, 0))] * 2
        args += [ctx_k, ctx_v]
    else:
        in_specs.append(pl.BlockSpec((None, t, 2 * A_KV_WIDTH), lambda bi, i: (bi, 0, 0)))
        args.append(kv)
    in_specs += [
        pl.BlockSpec(sink_rows.shape, lambda bi, i: (0, 0, 0)),
        tok(A_WIDTH), tok(B_WIDTH), tok(C_WIDTH),
        pl.BlockSpec(wout.shape, const2),
        pl.BlockSpec((1, d), const2),
        pl.BlockSpec((None, 1, d), gate_map),
    ]
    args += [sink_rows, sag, ob, oc, wout, gpost, gate]
    return pl.pallas_call(
        functools.partial(_attn_body, banded=banded, tq=tq, nblk_total=nblk_total),
        grid=(b, t // tq),
        in_specs=in_specs,
        out_specs=tok(d),
        out_shape=jax.ShapeDtypeStruct((b, t, d), F32),
        compiler_params=pltpu.CompilerParams(dimension_semantics=("arbitrary", "arbitrary"),
                                             vmem_limit_bytes=VMEM_LIMIT_BYTES),
        name="attention_banded" if banded else "attention_full",
    )(*args)


def _block_ones(width, block):
    idx = np.arange(width) // block
    return jnp.asarray((idx[:, None] == idx[None, :]).astype(np.float32), dtype=BF16)


def _rope_tables(n):
    half = HEAD_DIM // 2
    nf = half // 2
    pos = np.arange(n)
    row = (pos // GRID_W).astype(np.float32)
    colp = (pos % GRID_W).astype(np.float32)
    inv_freq = jnp.asarray(ROPE_BASE, F32) ** (-jnp.arange(nf, dtype=F32) / nf)
    lane = np.arange(LANES) % HEAD_DIM
    freq_idx = lane % nf
    quarter = lane // nf
    p = jnp.where(jnp.asarray(quarter < 2)[None, :], jnp.asarray(row)[:, None], jnp.asarray(colp)[:, None])
    ang = p * inv_freq[freq_idx][None, :]
    cos, sin = jnp.cos(ang), jnp.sin(ang)
    first = jnp.asarray(quarter % 2 == 0)[None, :]
    sa = jnp.where(first, -sin, 0.0)
    sb = jnp.where(first, 0.0, sin)
    return cos.astype(F32), sa.astype(F32), sb.astype(F32)


def _band_table():
    q_rel = np.arange(ATTN_BLOCK)
    k_rel = np.arange(3 * ATTN_BLOCK) - ATTN_BLOCK
    band = (np.abs(q_rel[:, None] - k_rel[None, :]) <= ATTN_BLOCK).astype(np.float32)
    return jnp.asarray(np.tile(band, (A_GROUP, 1)))


def _head_perm():
    idx = np.arange(A_WIDTH).reshape(A_KV_HEADS, A_GROUP, HEAD_DIM)
    return np.transpose(idx, (1, 0, 2)).reshape(-1)


def _layout_w_in(w_in_l):
    perm = _head_perm()
    o = 0
    sec = {}
    for name, width in (("aq", A_WIDTH), ("ak", A_KV_WIDTH), ("av", A_KV_WIDTH), ("ag", A_WIDTH),
                        ("bu", B_WIDTH), ("bv", B_WIDTH), ("bg", B_WIDTH),
                        ("cq", C_WIDTH), ("ck", C_WIDTH), ("cv", C_WIDTH), ("ca", N_GATES), ("cb", N_GATES),
                        ("cg", C_WIDTH)):
        sec[name] = w_in_l[:, o:o + width]
        o += width
    d = w_in_l.shape[0]
    pad = jnp.zeros((d, LANES - 2 * N_GATES), w_in_l.dtype)
    cols = [sec["aq"][:, perm], sec["ak"], sec["av"], sec["ag"][:, perm], sec["bu"], sec["bv"], sec["bg"],
            sec["cq"], sec["ck"], sec["cv"], sec["cg"], sec["ca"], sec["cb"], pad]
    return jnp.concatenate(cols, axis=1).astype(BF16)


def _row_tile(t, target):
    tile = min(t, target)
    while t % tile:
        tile -= ATTN_BLOCK
    return tile


def kernel(x_prompt, x_sample, cache_k, cache_v, state_delta, c, c_ctx, w_mod, b_mod, g_pre, w_in, attn_sink,
           sgu_ln_g, sgu_ln_b, sgu_w, sgu_b, gdn_conv_w, gdn_a_log, gdn_dt_bias, gdn_norm_g, g_post, w_out):
    depth, d, _ = w_mod.shape
    b_ctx, t_ctx, _ = x_prompt.shape
    b_lat, t_lat, _ = x_sample.shape
    past = cache_k.shape[2]
    assert t_ctx % SGU_CHUNK == 0 and t_lat % SGU_CHUNK == 0 and t_lat % GRID_W == 0
    assert t_ctx // GDN_CHUNK >= 2 and t_lat // GDN_CHUNK >= 2
    assert w_in.shape[2] == 2 * A_WIDTH + 2 * A_KV_WIDTH + 3 * B_WIDTH + 4 * C_WIDTH + 2 * N_GATES

    n_cond = b_lat + 1
    rows = -(-n_cond // 8) * 8
    conds = jnp.concatenate([c, c_ctx[None, :], jnp.zeros((rows - n_cond, d), c.dtype)], axis=0)
    mod = _modulation(conds, w_mod, b_mod).reshape(depth, rows, 3, d)

    seg_b = _block_ones(B_WIDTH, B_GROUP_DIM)
    seg_c = _block_ones(C_WIDTH, C_HEAD_DIM)
    rope_tabs = _rope_tables(t_lat)
    band = _band_table()
    perm = _head_perm()
    ck = cache_k.reshape(b_lat, depth, past, A_KV_WIDTH)
    cv = cache_v.reshape(b_lat, depth, past, A_KV_WIDTH)
    tm_ctx, tm_lat = _row_tile(t_ctx, 512), _row_tile(t_lat, 512)

    xp, xs = x_prompt, x_sample
    new_k, new_v, new_s = [], [], []
    for l in range(depth):
        w1 = _layout_w_in(w_in[l])
        wout = jnp.concatenate([w_out[l][0:A_WIDTH][perm], w_out[l][A_WIDTH:]], axis=0).astype(BF16)
        gpre = g_pre[l].reshape(1, d)
        gpost = g_post[l].reshape(1, d)
        lng = sgu_ln_g[l].reshape(1, B_WIDTH)
        lnb = sgu_ln_b[l].reshape(1, B_WIDTH)
        sw = sgu_w[l].astype(BF16)
        sbias = jnp.repeat(jnp.swapaxes(sgu_b[l], 0, 1), B_GROUP_DIM, axis=1)
        sink = attn_sink[l].reshape(A_KV_HEADS, A_GROUP)
        sink_rows = jnp.repeat(sink, ATTN_BLOCK, axis=1)[:, :, None]
        alog = gdn_a_log[l].reshape(1, N_GATES)
        dtb = gdn_dt_bias[l].reshape(1, N_GATES)
        normg = jnp.tile(gdn_norm_g[l], C_HEADS).reshape(1, C_WIDTH)
        convw = gdn_conv_w[l]
        mod_lat = mod[l, 0:b_lat]
        mod_ctx = mod[l, b_lat:b_lat + 1]

        q, kv, sag, ob, cqkv, scg, ab, kf, vf = _projection(
            xp, mod_ctx, gpre, w1, seg_b, lng, lnb, sw, sbias, None, keep_f32_kv=True, tm=tm_ctx)
        oc, s_fin = _gdn(cqkv, ab, scg, convw, alog, dtb, normg, seg_c, None, emit_state=True)
        xp = _attention_out(xp, q, kv, sag, ob, oc, sink_rows, wout, gpost, mod_ctx[:, 2:3], None, None, None, l,
                            tq=t_ctx)
        new_k.append(kf.reshape(b_ctx, t_ctx, A_KV_HEADS, HEAD_DIM))
        new_v.append(vf.reshape(b_ctx, t_ctx, A_KV_HEADS, HEAD_DIM))
        new_s.append(s_fin)

        q, kv, sag, ob, cqkv, scg, ab = _projection(
            xs, mod_lat, gpre, w1, seg_b, lng, lnb, sw, sbias, rope_tabs, keep_f32_kv=False, tm=tm_lat)
        (oc,) = _gdn(cqkv, ab, scg, convw, alog, dtb, normg, seg_c, state_delta[:, l], emit_state=False)
        xs = _attention_out(xs, q, kv, sag, ob, oc, sink_rows, wout, gpost, mod_lat[:, 2:3], band, ck, cv, l,
                            tq=_row_tile(t_lat, 512))

    return (xp, xs, jnp.stack(new_k, axis=1), jnp.stack(new_v, axis=1), jnp.stack(new_s, axis=1))
```

```python
import functools

import numpy as np
import jax
import jax.numpy as jnp
from jax import lax
from jax.experimental import pallas as pl
from jax.experimental.pallas import tpu as pltpu

F32 = jnp.float32
BF16 = jnp.bfloat16

GRID_W = 64
HEAD_DIM = 64
A_HEADS = 8
A_KV_HEADS = 2
A_GROUP = A_HEADS // A_KV_HEADS
A_WIDTH = A_HEADS * HEAD_DIM
A_KV_WIDTH = A_KV_HEADS * HEAD_DIM
ATTN_BLOCK = 128
ROPE_BASE = 10000.0
B_GROUPS = 4
B_GROUP_DIM = 64
B_WIDTH = B_GROUPS * B_GROUP_DIM
SGU_CHUNK = 128
C_HEADS = 4
C_HEAD_DIM = 64
C_WIDTH = C_HEADS * C_HEAD_DIM
CONV_K = 5
GDN_CHUNK = 64
N_DIRS = 2
N_GATES = N_DIRS * C_HEADS
EPS = 1e-6
NEG_INF = -1e30

LANES = 128
BF16_SUBLANES = 16
VMEM_LIMIT_BYTES = 56 * 1024 * 1024

_OFF_Q = 0
_OFF_K = _OFF_Q + A_WIDTH
_OFF_V = _OFF_K + A_KV_WIDTH
_OFF_AG = _OFF_V + A_KV_WIDTH
_OFF_BU = _OFF_AG + A_WIDTH
_OFF_BV = _OFF_BU + B_WIDTH
_OFF_BG = _OFF_BV + B_WIDTH
_OFF_CQKV = _OFF_BG + B_WIDTH
_OFF_CG = _OFF_CQKV + 3 * C_WIDTH
_OFF_AB = _OFF_CG + C_WIDTH
_W1_COLS = _OFF_AB + LANES

_G_GC = 0
_G_BETA = N_GATES
_G_BEGC = 2 * N_GATES
_G_EGC = 3 * N_GATES
_G_EGLGC = 4 * N_GATES
_G_EGL = 5 * N_GATES


def _dot(a, b):
    return jnp.dot(a, b, preferred_element_type=F32)


def _dot_nt(a, b):
    return lax.dot_general(a, b, (((1,), (1,)), ((), ())), preferred_element_type=F32)


def _dot_tn(a, b):
    return lax.dot_general(a, b, (((0,), (0,)), ((), ())), preferred_element_type=F32)


def _split3(x):
    hi = x.astype(BF16)
    r1 = x - hi.astype(F32)
    mid = r1.astype(BF16)
    lo = (r1 - mid.astype(F32)).astype(BF16)
    return hi, mid, lo


def _seg_sum(x, seg):
    hi, mid, lo = _split3(x)
    return _dot(hi, seg) + _dot(mid, seg) + _dot(lo, seg)


def _tri_sum(tri, x):
    hi, mid, lo = _split3(x)
    return _dot(tri, hi) + _dot(tri, mid) + _dot(tri, lo)


def _sigmoid(x):
    return 1.0 / (1.0 + jnp.exp(-x))


def _silu(x):
    return x * _sigmoid(x)


def _softplus(x):
    return jnp.maximum(x, 0.0) + jnp.log1p(jnp.exp(-jnp.abs(x)))


def _mod_body(cond_ref, w_ref, b_ref, o_ref):
    a = _silu(cond_ref[...]).astype(BF16)
    o_ref[...] = _dot(a, w_ref[...].astype(BF16)) + b_ref[...]


def _modulation(conds, w_mod, b_mod):
    depth, d, d3 = w_mod.shape
    rows = conds.shape[0]
    tn = d
    return pl.pallas_call(
        _mod_body,
        grid=(depth, d3 // tn),
        in_specs=[
            pl.BlockSpec((rows, d), lambda l, j: (0, 0)),
            pl.BlockSpec((None, d, tn), lambda l, j: (l, 0, j)),
            pl.BlockSpec((None, 1, tn), lambda l, j: (l, 0, j)),
        ],
        out_specs=pl.BlockSpec((None, rows, tn), lambda l, j: (l, 0, j)),
        out_shape=jax.ShapeDtypeStruct((depth, rows, d3), F32),
        compiler_params=pltpu.CompilerParams(dimension_semantics=("arbitrary", "arbitrary"),
                                             vmem_limit_bytes=VMEM_LIMIT_BYTES),
        name="modulation",
    )(conds, w_mod, b_mod.reshape(depth, 1, d3))


def _rope(z, c, sa, sb):
    w = z.shape[1]
    reps = w // LANES
    if reps > 1:
        c = jnp.concatenate([c] * reps, axis=1)
        sa = jnp.concatenate([sa] * reps, axis=1)
        sb = jnp.concatenate([sb] * reps, axis=1)
    half_pair = HEAD_DIM // 4
    return z * c + pltpu.roll(z, w - half_pair, 1) * sa + pltpu.roll(z, half_pair, 1) * sb


def _proj_body(*refs, rope, keep_f32_kv, tm):
    it = iter(refs)
    x_ref, mod_ref, gpre_ref, w_ref, seg_ref, lng_ref, lnb_ref, sw_ref, sbias_ref = (next(it) for _ in range(9))
    if rope:
        rc_ref, rsa_ref, rsb_ref = (next(it) for _ in range(3))
    q_ref, kv_ref, sag_ref, ob_ref, cqkv_ref, scg_ref, ab_ref = (next(it) for _ in range(7))
    if keep_f32_kv:
        kf_ref, vf_ref = (next(it) for _ in range(2))

    x = x_ref[...]
    ms = jnp.mean(x * x, axis=-1, keepdims=True)
    h = x * lax.rsqrt(ms + EPS) * gpre_ref[...]
    h = h * (1.0 + mod_ref[1:2, :]) + mod_ref[0:1, :]
    hb = h.astype(BF16)

    def proj(lo, width):
        return _dot(hb, w_ref[:, lo:lo + width])

    zq = proj(_OFF_Q, A_WIDTH)
    zk = proj(_OFF_K, A_KV_WIDTH)
    zv = proj(_OFF_V, A_KV_WIDTH)
    if keep_f32_kv:
        kf_ref[...] = zk
        vf_ref[...] = zv
    if rope:
        c, sa, sb = rc_ref[...], rsa_ref[...], rsb_ref[...]
        zq = _rope(zq, c, sa, sb)
        zk = _rope(zk, c, sa, sb)
    q_ref[...] = (zq * (HEAD_DIM ** -0.5)).astype(BF16)
    kv_ref[:, 0:A_KV_WIDTH] = zk.astype(BF16)
    kv_ref[:, A_KV_WIDTH:2 * A_KV_WIDTH] = zv.astype(BF16)
    sag_ref[...] = _silu(proj(_OFF_AG, A_WIDTH)).astype(BF16)

    zu = proj(_OFF_BU, B_WIDTH)
    zbv = proj(_OFF_BV, B_WIDTH)
    zbg = proj(_OFF_BG, B_WIDTH)
    seg = seg_ref[...]
    inv_n = 1.0 / B_GROUP_DIM
    mean = _seg_sum(zbv, seg) * inv_n
    xc = zbv - mean
    var = _seg_sum(xc * xc, seg) * inv_n
    vg = (xc * lax.rsqrt(var + EPS) * lng_ref[...] + lnb_ref[...]).astype(BF16)
    sbias = sbias_ref[...]
    mixed_chunks = []
    for ci in range(tm // SGU_CHUNK):
        rows = slice(ci * SGU_CHUNK, (ci + 1) * SGU_CHUNK)
        per_group = [_dot(sw_ref[g], vg[rows, g * B_GROUP_DIM:(g + 1) * B_GROUP_DIM]) for g in range(B_GROUPS)]
        mixed_chunks.append(jnp.concatenate(per_group, axis=1) + sbias)
    mixed = jnp.concatenate(mixed_chunks, axis=0) if len(mixed_chunks) > 1 else mixed_chunks[0]
    ob_ref[...] = (zu * mixed * _silu(zbg)).astype(BF16)

    cqkv_ref[...] = proj(_OFF_CQKV, 3 * C_WIDTH).astype(BF16)
    scg_ref[...] = _silu(proj(_OFF_CG, C_WIDTH)).astype(BF16)
    ab_ref[...] = proj(_OFF_AB, LANES)[:, 0:2 * N_GATES]


def _projection(x, mod3, gpre, w1, seg, lng, lnb, sw, sbias, rope_tabs, *, keep_f32_kv, tm):
    b, t, d = x.shape
    rope = rope_tabs is not None
    nmod = mod3.shape[0]
    mod_map = (lambda bi, i: (bi, 0, 0)) if nmod > 1 else (lambda bi, i: (0, 0, 0))
    const2 = lambda bi, i: (0, 0)
    in_specs = [
        pl.BlockSpec((None, tm, d), lambda bi, i: (bi, i, 0)),
        pl.BlockSpec((None, 3, d), mod_map),
        pl.BlockSpec((1, d), const2),
        pl.BlockSpec(w1.shape, const2),
        pl.BlockSpec(seg.shape, const2),
        pl.BlockSpec((1, B_WIDTH), const2),
        pl.BlockSpec((1, B_WIDTH), const2),
        pl.BlockSpec(sw.shape, lambda bi, i: (0, 0, 0)),
        pl.BlockSpec(sbias.shape, const2),
    ]
    args = [x, mod3, gpre, w1, seg, lng, lnb, sw, sbias]
    if rope:
        in_specs += [pl.BlockSpec((tm, LANES), lambda bi, i: (i, 0))] * 3
        args += list(rope_tabs)

    def tok(width, dtype):
        return (pl.BlockSpec((None, tm, width), lambda bi, i: (bi, i, 0)),
                jax.ShapeDtypeStruct((b, t, width), dtype))

    outs = [tok(A_WIDTH, BF16), tok(2 * A_KV_WIDTH, BF16), tok(A_WIDTH, BF16), tok(B_WIDTH, BF16),
            tok(3 * C_WIDTH, BF16), tok(C_WIDTH, BF16), tok(2 * N_GATES, F32)]
    if keep_f32_kv:
        outs += [tok(A_KV_WIDTH, F32), tok(A_KV_WIDTH, F32)]
    return pl.pallas_call(
        functools.partial(_proj_body, rope=rope, keep_f32_kv=keep_f32_kv, tm=tm),
        grid=(b, t // tm),
        in_specs=in_specs,
        out_specs=[o[0] for o in outs],
        out_shape=[o[1] for o in outs],
        compiler_params=pltpu.CompilerParams(dimension_semantics=("arbitrary", "arbitrary"),
                                             vmem_limit_bytes=VMEM_LIMIT_BYTES),
        name="projection_rope" if rope else "projection",
    )(*args)


def _pair_masks(n):
    rows = lax.broadcasted_iota(jnp.int32, (n, n), 0)
    cols = lax.broadcasted_iota(jnp.int32, (n, n), 1)
    masks = []
    level = 0
    while (1 << level) < n:
        same_pair = (rows >> (level + 1)) == (cols >> (level + 1))
        other_half = (rows >> level) != (cols >> level)
        masks.append(same_pair & other_half)
        level += 1
    return masks, rows == cols


def _gdn_body(*refs, t, has_s0, emit_state):
    it = iter(refs)
    cqkv_ref, ab_ref, scg_ref, convw_ref, alog_ref, dtb_ref, normg_ref, seg_ref = (next(it) for _ in range(8))
    s0_ref = next(it) if has_s0 else None
    oc_ref = next(it)
    sfin_ref = next(it) if emit_state else None
    qkv_scr, gate_scr, grow_scr, od_scr, s_scr = (next(it) for _ in range(5))

    c = GDN_CHUNK
    nch = t // c
    halo = BF16_SUBLANES
    rows = lax.broadcasted_iota(jnp.int32, (c, c), 0)
    cols = lax.broadcasted_iota(jnp.int32, (c, c), 1)
    lower = rows >= cols
    upper = rows <= cols
    tri_l = jnp.where(lower, 1.0, 0.0).astype(BF16)
    tri_u = jnp.where(upper, 1.0, 0.0).astype(BF16)
    pair_masks, eye = _pair_masks(c)
    gate_lane = lax.broadcasted_iota(jnp.int32, (1, N_GATES), 1)
    is_fwd = gate_lane < C_HEADS
    seg = seg_ref[...]
    convw = convw_ref[...]
    neg_rate = -jnp.exp(alog_ref[...])
    dtb = dtb_ref[...]

    def prep_chunk(n, first, last):
        r0 = n * c
        if first:
            win = jnp.concatenate([jnp.zeros((halo, 3 * C_WIDTH), F32),
                                   cqkv_ref[0:c + halo, :].astype(F32)], axis=0)
        elif last:
            win = jnp.concatenate([cqkv_ref[t - c - halo:t, :].astype(F32),
                                   jnp.zeros((halo, 3 * C_WIDTH), F32)], axis=0)
        else:
            r0 = pl.multiple_of(r0, c)
            win = cqkv_ref[pl.ds(r0 - halo, c + 2 * halo), :].astype(F32)
        y = jnp.zeros((c, 3 * C_WIDTH), F32)
        for j in range(CONV_K):
            lo = halo - CONV_K // 2 + j
            y = y + win[lo:lo + c, :] * convw[j:j + 1, :]
        y = _silu(y)
        q = y[:, 0:C_WIDTH]
        k = y[:, C_WIDTH:2 * C_WIDTH]
        v = y[:, 2 * C_WIDTH:3 * C_WIDTH]
        q = q * lax.rsqrt(_seg_sum(q * q, seg) + EPS) * (C_HEAD_DIM ** -0.5)
        k = k * lax.rsqrt(_seg_sum(k * k, seg) + EPS)
        qkv_scr[pl.ds(r0, c), 0:C_WIDTH] = q
        qkv_scr[pl.ds(r0, c), C_WIDTH:2 * C_WIDTH] = k
        qkv_scr[pl.ds(r0, c), 2 * C_WIDTH:3 * C_WIDTH] = v

        ab = ab_ref[pl.ds(r0, c), :]
        g = neg_rate * _softplus(ab[:, 0:N_GATES] + dtb)
        beta = _sigmoid(ab[:, N_GATES:2 * N_GATES])
        gc_f = _tri_sum(tri_l, g)
        gc_b = _tri_sum(tri_u, g)
        gc = jnp.where(is_fwd, gc_f, gc_b)
        gl = jnp.where(is_fwd, gc_f[c - 1:c, :], gc_b[0:1, :])
        egc = jnp.exp(gc)
        gate_scr[pl.ds(r0, c), _G_GC:_G_GC + N_GATES] = gc
        gate_scr[pl.ds(r0, c), _G_BETA:_G_BETA + N_GATES] = beta
        gate_scr[pl.ds(r0, c), _G_BEGC:_G_BEGC + N_GATES] = beta * egc
        gate_scr[pl.ds(r0, c), _G_EGC:_G_EGC + N_GATES] = egc
        gate_scr[pl.ds(r0, c), _G_EGLGC:_G_EGLGC + N_GATES] = jnp.exp(gl - gc)
        gate_scr[pl.ds(r0, c), _G_EGL:_G_EGL + N_GATES] = jnp.broadcast_to(jnp.exp(gl), (c, N_GATES))
        grow_scr[n] = gc.T

    prep_chunk(0, True, False)
    if nch > 2:
        def prep_step(n, carry):
            prep_chunk(n, False, False)
            return carry
        lax.fori_loop(1, nch - 1, prep_step, 0)
    prep_chunk(nch - 1, False, True)

    for d in range(N_DIRS):
        for hh in range(C_HEADS):
            if has_s0:
                s_scr[d, hh] = s0_ref[d, hh]
            else:
                s_scr[d, hh] = jnp.zeros((C_HEAD_DIM, C_HEAD_DIM), F32)

    chains = [(d, hh) for d in range(N_DIRS) for hh in range(C_HEADS)]
    strict_masks = (rows > cols, rows < cols)
    incl_masks = (lower, upper)

    def scan_step(n, carry):
        r0s, qcs, kcs, vcs, gts, grs = [], [], [], [], [], []
        for d in range(N_DIRS):
            m = n if d == 0 else nch - 1 - n
            r0 = pl.multiple_of(m * c, c)
            r0s.append(r0)
            qcs.append(qkv_scr[pl.ds(r0, c), 0:C_WIDTH])
            kcs.append(qkv_scr[pl.ds(r0, c), C_WIDTH:2 * C_WIDTH])
            vcs.append(qkv_scr[pl.ds(r0, c), 2 * C_WIDTH:3 * C_WIDTH])
            gts.append(gate_scr[pl.ds(r0, c), :])
            grs.append(grow_scr[m])

        def col(i, base):
            d, hh = chains[i]
            gi = d * C_HEADS + hh
            return gts[d][:, base + gi:base + gi + 1]

        def head(arrs, i):
            d, hh = chains[i]
            return arrs[d][:, hh * C_HEAD_DIM:(hh + 1) * C_HEAD_DIM]

        idx = range(len(chains))
        q = [head(qcs, i) for i in idx]
        k = [head(kcs, i) for i in idx]
        v = [head(vcs, i) for i in idx]
        kb16 = [k[i].astype(BF16) for i in idx]
        kk = [_dot_nt(kb16[i], kb16[i]) for i in idx]
        qk = [_dot_nt(q[i].astype(BF16), kb16[i]) for i in idx]
        decay, a_mat = [], []
        for i in idx:
            d, hh = chains[i]
            gi = d * C_HEADS + hh
            dec = jnp.where(incl_masks[d], jnp.exp(col(i, _G_GC) - grs[d][gi:gi + 1, :]), 0.0)
            decay.append(dec)
            a_mat.append(jnp.where(strict_masks[d], kk[i] * col(i, _G_BETA) * dec, 0.0))

        t_mat = [jnp.where(eye, 1.0, 0.0) - jnp.where(pair_masks[0], a_mat[i], 0.0) for i in idx]
        for mask in pair_masks[1:]:
            tb = [t_mat[i].astype(BF16) for i in idx]
            et = [_dot(jnp.where(mask, a_mat[i], 0.0).astype(BF16), tb[i]).astype(BF16) for i in idx]
            t_mat = [t_mat[i] - _dot(tb[i], et[i]) for i in idx]

        rhs = [jnp.concatenate([v[i] * col(i, _G_BETA), k[i] * col(i, _G_BEGC)], axis=1).astype(BF16) for i in idx]
        uw = [_dot(t_mat[i].astype(BF16), rhs[i]) for i in idx]
        state = [s_scr[chains[i][0], chains[i][1]] for i in idx]
        wq = [jnp.concatenate([uw[i][:, C_HEAD_DIM:2 * C_HEAD_DIM], q[i] * col(i, _G_EGC)], axis=0).astype(BF16)
              for i in idx]
        ws_qs = [_dot(wq[i], state[i].astype(BF16)) for i in idx]
        vnb = [(uw[i][:, 0:C_HEAD_DIM] - ws_qs[i][0:c, :]).astype(BF16) for i in idx]
        o = [ws_qs[i][c:2 * c, :] + _dot((qk[i] * decay[i]).astype(BF16), vnb[i]) for i in idx]
        for i in idx:
            d, hh = chains[i]
            k_dec = (k[i] * col(i, _G_EGLGC)).astype(BF16)
            s_scr[d, hh] = state[i] * col(i, _G_EGL) + _dot_tn(k_dec, vnb[i])
        for d in range(N_DIRS):
            od_scr[d, pl.ds(r0s[d], c), :] = jnp.concatenate(o[d * C_HEADS:(d + 1) * C_HEADS], axis=1)
        return carry

    lax.fori_loop(0, nch, scan_step, 0)

    blk = 4 * c
    inv_n = 1.0 / C_HEAD_DIM

    def out_step(i, carry):
        r0 = pl.multiple_of(i * blk, blk)
        o = od_scr[0, pl.ds(r0, blk), :] + od_scr[1, pl.ds(r0, blk), :]
        ms = _seg_sum(o * o, seg) * inv_n
        on = o * lax.rsqrt(ms + EPS) * normg_ref[...]
        oc_ref[pl.ds(r0, blk), :] = (on * scg_ref[pl.ds(r0, blk), :].astype(F32)).astype(BF16)
        return carry

    lax.fori_loop(0, t // blk, out_step, 0)

    if emit_state:
        for d in range(N_DIRS):
            for hh in range(C_HEADS):
                sfin_ref[d, hh] = s_scr[d, hh]


def _gdn(cqkv, ab, scg, convw, alog, dtb, normg, seg, s0, *, emit_state):
    b, t, _ = cqkv.shape
    nch = t // GDN_CHUNK
    has_s0 = s0 is not None
    const2 = lambda bi: (0, 0)
    in_specs = [
        pl.BlockSpec((None, t, 3 * C_WIDTH), lambda bi: (bi, 0, 0)),
        pl.BlockSpec((None, t, 2 * N_GATES), lambda bi: (bi, 0, 0)),
        pl.BlockSpec((None, t, C_WIDTH), lambda bi: (bi, 0, 0)),
        pl.BlockSpec(convw.shape, const2),
        pl.BlockSpec((1, N_GATES), const2),
        pl.BlockSpec((1, N_GATES), const2),
        pl.BlockSpec((1, C_WIDTH), const2),
        pl.BlockSpec(seg.shape, const2),
    ]
    args = [cqkv, ab, scg, convw, alog, dtb, normg, seg]
    state_spec = pl.BlockSpec((None, N_DIRS, C_HEADS, C_HEAD_DIM, C_HEAD_DIM), lambda bi: (bi, 0, 0, 0, 0))
    if has_s0:
        in_specs.append(state_spec)
        args.append(s0)
    out_specs = [pl.BlockSpec((None, t, C_WIDTH), lambda bi: (bi, 0, 0))]
    out_shape = [jax.ShapeDtypeStruct((b, t, C_WIDTH), BF16)]
    if emit_state:
        out_specs.append(state_spec)
        out_shape.append(jax.ShapeDtypeStruct((b, N_DIRS, C_HEADS, C_HEAD_DIM, C_HEAD_DIM), F32))
    scratch = [
        pltpu.VMEM((t, 3 * C_WIDTH), F32),
        pltpu.VMEM((t, LANES), F32),
        pltpu.VMEM((nch, N_GATES, GDN_CHUNK), F32),
        pltpu.VMEM((N_DIRS, t, C_WIDTH), F32),
        pltpu.VMEM((N_DIRS, C_HEADS, C_HEAD_DIM, C_HEAD_DIM), F32),
    ]
    return pl.pallas_call(
        functools.partial(_gdn_body, t=t, has_s0=has_s0, emit_state=emit_state),
        grid=(b,),
        in_specs=in_specs,
        out_specs=out_specs,
        out_shape=out_shape,
        scratch_shapes=scratch,
        compiler_params=pltpu.CompilerParams(dimension_semantics=("arbitrary",),
                                             vmem_limit_bytes=VMEM_LIMIT_BYTES),
        name="gdn_state" if emit_state else "gdn",
    )(*args)


def _attn_body(*refs, banded, tq, nblk_total):
    it = iter(refs)
    x_ref, q_ref, kv_ref = (next(it) for _ in range(3))
    if banded:
        kvl_ref, kvr_ref, band_ref = (next(it) for _ in range(3))
        ck_ref, cv_ref = (next(it) for _ in range(2))
    sink_ref, sag_ref, ob_ref, oc_ref, wout_ref, gpost_ref, gate_ref, y_ref = (next(it) for _ in range(8))

    nb = tq // ATTN_BLOCK
    rows_q = A_GROUP * ATTN_BLOCK
    lane = lax.broadcasted_iota(jnp.int32, (1, A_KV_WIDTH), 1)
    head_mask = [lane < HEAD_DIM, lane >= HEAD_DIM]

    def split_heads(a):
        zero = jnp.zeros_like(a)
        return [jnp.where(head_mask[h], a, zero) for h in range(A_KV_HEADS)]

    if banded:
        kv_ext = jnp.concatenate([kvl_ref[...], kv_ref[...], kvr_ref[...]], axis=0)
        k_own = split_heads(kv_ext[:, 0:A_KV_WIDTH])
        v_own = split_heads(kv_ext[:, A_KV_WIDTH:2 * A_KV_WIDTH])
        k_ctx = split_heads(ck_ref[...].astype(BF16))
        v_ctx = split_heads(cv_ref[...].astype(BF16))
        band = band_ref[...]
        first_blk = pl.program_id(1) * nb
        col = lax.broadcasted_iota(jnp.int32, (1, 3 * ATTN_BLOCK), 1)
    else:
        kv_all = kv_ref[...]
        k_ctx = split_heads(kv_all[:, 0:A_KV_WIDTH])
        v_ctx = split_heads(kv_all[:, A_KV_WIDTH:2 * A_KV_WIDTH])

    o_blocks = []
    for j in range(nb):
        qrows = slice(j * ATTN_BLOCK, (j + 1) * ATTN_BLOCK)
        q2 = jnp.concatenate([q_ref[qrows, g * A_KV_WIDTH:(g + 1) * A_KV_WIDTH] for g in range(A_GROUP)], axis=0)
        if banded:
            blk = first_blk + j
            left_ok = jnp.where(blk > 0, 1.0, 0.0)
            right_ok = jnp.where(blk < nblk_total - 1, 1.0, 0.0)
            col_ok = jnp.where(col < ATTN_BLOCK, left_ok, jnp.where(col >= 2 * ATTN_BLOCK, right_ok, 1.0))
            visible = (band * col_ok) > 0.5
            krows = slice(j * ATTN_BLOCK, (j + 3) * ATTN_BLOCK)
        o_blk = jnp.zeros((rows_q, A_KV_WIDTH), F32)
        for h in range(A_KV_HEADS):
            sink = sink_ref[h]
            s_ctx = _dot_nt(q2, k_ctx[h])
            m = jnp.maximum(jnp.max(s_ctx, axis=-1, keepdims=True), sink)
            if banded:
                s_own = jnp.where(visible, _dot_nt(q2, k_own[h][krows]), NEG_INF)
                m = jnp.maximum(m, jnp.max(s_own, axis=-1, keepdims=True))
                p_own = jnp.exp(s_own - m)
            p_ctx = jnp.exp(s_ctx - m)
            denom = jnp.sum(p_ctx, axis=-1, keepdims=True) + jnp.exp(sink - m)
            acc = _dot(p_ctx.astype(BF16), v_ctx[h])
            if banded:
                denom = denom + jnp.sum(p_own, axis=-1, keepdims=True)
                acc = acc + _dot(p_own.astype(BF16), v_own[h][krows])
            o_blk = o_blk + acc * (1.0 / denom)
        o_blocks.append(jnp.concatenate([o_blk[g * ATTN_BLOCK:(g + 1) * ATTN_BLOCK] for g in range(A_GROUP)], axis=1))
    o_a = jnp.concatenate(o_blocks, axis=0) if nb > 1 else o_blocks[0]

    mix_a = (o_a * sag_ref[...].astype(F32)).astype(BF16)
    mix = (_dot(mix_a, wout_ref[0:A_WIDTH, :])
           + _dot(ob_ref[...], wout_ref[A_WIDTH:A_WIDTH + B_WIDTH, :])
           + _dot(oc_ref[...], wout_ref[A_WIDTH + B_WIDTH:A_WIDTH + B_WIDTH + C_WIDTH, :]))
    ms = jnp.mean(mix * mix, axis=-1, keepdims=True)
    y_ref[...] = x_ref[...] + gate_ref[...] * (mix * lax.rsqrt(ms + EPS) * gpost_ref[...])


def _attention_out(x, q, kv, sag, ob, oc, sink_rows, wout, gpost, gate, band, ctx_k, ctx_v, layer, *, tq):
    b, t, d = x.shape
    banded = band is not None
    nblk_total = t // ATTN_BLOCK
    per_q = tq // ATTN_BLOCK
    ngate = gate.shape[0]
    gate_map = (lambda bi, i: (bi, 0, 0)) if ngate > 1 else (lambda bi, i: (0, 0, 0))
    const2 = lambda bi, i: (0, 0)

    def tok(width):
        return pl.BlockSpec((None, tq, width), lambda bi, i: (bi, i, 0))

    in_specs = [tok(d), tok(A_WIDTH)]
    args = [x, q]
    if banded:
        in_specs.append(tok(2 * A_KV_WIDTH))
        args.append(kv)
        in_specs.append(pl.BlockSpec((None, ATTN_BLOCK, 2 * A_KV_WIDTH),
                                     lambda bi, i: (bi, jnp.maximum(i * per_q - 1, 0), 0)))
        in_specs.append(pl.BlockSpec((None, ATTN_BLOCK, 2 * A_KV_WIDTH),
                                     lambda bi, i: (bi, jnp.minimum((i + 1) * per_q, nblk_total - 1), 0)))
        in_specs.append(pl.BlockSpec(band.shape, const2))
        args += [kv, kv, band]
        past = ctx_k.shape[2]
        in_specs += [pl.BlockSpec((None, None, past, A_KV_WIDTH), lambda bi, i: (bi, layer, 0, 0))] * 2
        args += [ctx_k, ctx_v]
    else:
        in_specs.append(pl.BlockSpec((None, t, 2 * A_KV_WIDTH), lambda bi, i: (bi, 0, 0)))
        args.append(kv)
    in_specs += [
        pl.BlockSpec(sink_rows.shape, lambda bi, i: (0, 0, 0)),
        tok(A_WIDTH), tok(B_WIDTH), tok(C_WIDTH),
        pl.BlockSpec(wout.shape, const2),
        pl.BlockSpec((1, d), const2),
        pl.BlockSpec((None, 1, d), gate_map),
    ]
    args += [sink_rows, sag, ob, oc, wout, gpost, gate]
    return pl.pallas_call(
        functools.partial(_attn_body, banded=banded, tq=tq, nblk_total=nblk_total),
        grid=(b, t // tq),
        in_specs=in_specs,
        out_specs=tok(d),
        out_shape=jax.ShapeDtypeStruct((b, t, d), F32),
        compiler_params=pltpu.CompilerParams(dimension_semantics=("arbitrary", "arbitrary"),
                                             vmem_limit_bytes=VMEM_LIMIT_BYTES),
        name="attention_banded" if banded else "attention_full",
    )(*args)


def _block_ones(width, block):
    idx = np.arange(width) // block
    return jnp.asarray((idx[:, None] == idx[None, :]).astype(np.float32), dtype=BF16)


def _rope_tables(n):
    half = HEAD_DIM // 2
    nf = half // 2
    pos = np.arange(n)
    row = (pos // GRID_W).astype(np.float32)
    colp = (pos % GRID_W).astype(np.float32)
    inv_freq = jnp.asarray(ROPE_BASE, F32) ** (-jnp.arange(nf, dtype=F32) / nf)
    lane = np.arange(LANES) % HEAD_DIM
    freq_idx = lane % nf
    quarter = lane // nf
    p = jnp.where(jnp.asarray(quarter < 2)[None, :], jnp.asarray(row)[:, None], jnp.asarray(colp)[:, None])
    ang = p * inv_freq[freq_idx][None, :]
    cos, sin = jnp.cos(ang), jnp.sin(ang)
    first = jnp.asarray(quarter % 2 == 0)[None, :]
    sa = jnp.where(first, -sin, 0.0)
    sb = jnp.where(first, 0.0, sin)
    return cos.astype(F32), sa.astype(F32), sb.astype(F32)


def _band_table():
    q_rel = np.arange(ATTN_BLOCK)
    k_rel = np.arange(3 * ATTN_BLOCK) - ATTN_BLOCK
    band = (np.abs(q_rel[:, None] - k_rel[None, :]) <= ATTN_BLOCK).astype(np.float32)
    return jnp.asarray(np.tile(band, (A_GROUP, 1)))


def _head_perm():
    idx = np.arange(A_WIDTH).reshape(A_KV_HEADS, A_GROUP, HEAD_DIM)
    return np.transpose(idx, (1, 0, 2)).reshape(-1)


def _layout_w_in(w_in_l):
    perm = _head_perm()
    o = 0
    sec = {}
    for name, width in (("aq", A_WIDTH), ("ak", A_KV_WIDTH), ("av", A_KV_WIDTH), ("ag", A_WIDTH),
                        ("bu", B_WIDTH), ("bv", B_WIDTH), ("bg", B_WIDTH),
                        ("cq", C_WIDTH), ("ck", C_WIDTH), ("cv", C_WIDTH), ("ca", N_GATES), ("cb", N_GATES),
                        ("cg", C_WIDTH)):
        sec[name] = w_in_l[:, o:o + width]
        o += width
    d = w_in_l.shape[0]
    pad = jnp.zeros((d, LANES - 2 * N_GATES), w_in_l.dtype)
    cols = [sec["aq"][:, perm], sec["ak"], sec["av"], sec["ag"][:, perm], sec["bu"], sec["bv"], sec["bg"],
            sec["cq"], sec["ck"], sec["cv"], sec["cg"], sec["ca"], sec["cb"], pad]
    return jnp.concatenate(cols, axis=1).astype(BF16)


def _row_tile(t, target):
    tile = min(t, target)
    while t % tile:
        tile -= ATTN_BLOCK
    return tile


def kernel(x_prompt, x_sample, cache_k, cache_v, state_delta, c, c_ctx, w_mod, b_mod, g_pre, w_in, attn_sink,
           sgu_ln_g, sgu_ln_b, sgu_w, sgu_b, gdn_conv_w, gdn_a_log, gdn_dt_bias, gdn_norm_g, g_post, w_out):
    depth, d, _ = w_mod.shape
    b_ctx, t_ctx, _ = x_prompt.shape
    b_lat, t_lat, _ = x_sample.shape
    past = cache_k.shape[2]
    assert t_ctx % SGU_CHUNK == 0 and t_lat % SGU_CHUNK == 0 and t_lat % GRID_W == 0
    assert t_ctx // GDN_CHUNK >= 2 and t_lat // GDN_CHUNK >= 2
    assert w_in.shape[2] == 2 * A_WIDTH + 2 * A_KV_WIDTH + 3 * B_WIDTH + 4 * C_WIDTH + 2 * N_GATES

    n_cond = b_lat + 1
    rows = -(-n_cond // 8) * 8
    conds = jnp.concatenate([c, c_ctx[None, :], jnp.zeros((rows - n_cond, d), c.dtype)], axis=0)
    mod = _modulation(conds, w_mod, b_mod).reshape(depth, rows, 3, d)

    seg_b = _block_ones(B_WIDTH, B_GROUP_DIM)
    seg_c = _block_ones(C_WIDTH, C_HEAD_DIM)
    rope_tabs = _rope_tables(t_lat)
    band = _band_table()
    perm = _head_perm()
    ck = cache_k.reshape(b_lat, depth, past, A_KV_WIDTH)
    cv = cache_v.reshape(b_lat, depth, past, A_KV_WIDTH)
    tm_ctx, tm_lat = _row_tile(t_ctx, 512), _row_tile(t_lat, 512)

    xp, xs = x_prompt, x_sample
    new_k, new_v, new_s = [], [], []
    for l in range(depth):
        w1 = _layout_w_in(w_in[l])
        wout = jnp.concatenate([w_out[l][0:A_WIDTH][perm], w_out[l][A_WIDTH:]], axis=0).astype(BF16)
        gpre = g_pre[l].reshape(1, d)
        gpost = g_post[l].reshape(1, d)
        lng = sgu_ln_g[l].reshape(1, B_WIDTH)
        lnb = sgu_ln_b[l].reshape(1, B_WIDTH)
        sw = sgu_w[l].astype(BF16)
        sbias = jnp.repeat(jnp.swapaxes(sgu_b[l], 0, 1), B_GROUP_DIM, axis=1)
        sink = attn_sink[l].reshape(A_KV_HEADS, A_GROUP)
        sink_rows = jnp.repeat(sink, ATTN_BLOCK, axis=1)[:, :, None]
        alog = gdn_a_log[l].reshape(1, N_GATES)
        dtb = gdn_dt_bias[l].reshape(1, N_GATES)
        normg = jnp.tile(gdn_norm_g[l], C_HEADS).reshape(1, C_WIDTH)
        convw = gdn_conv_w[l]
        mod_lat = mod[l, 0:b_lat]
        mod_ctx = mod[l, b_lat:b_lat + 1]

        q, kv, sag, ob, cqkv, scg, ab, kf, vf = _projection(
            xp, mod_ctx, gpre, w1, seg_b, lng, lnb, sw, sbias, None, keep_f32_kv=True, tm=tm_ctx)
        oc, s_fin = _gdn(cqkv, ab, scg, convw, alog, dtb, normg, seg_c, None, emit_state=True)
        xp = _attention_out(xp, q, kv, sag, ob, oc, sink_rows, wout, gpost, mod_ctx[:, 2:3], None, None, None, l,
                            tq=t_ctx)
        new_k.append(kf.reshape(b_ctx, t_ctx, A_KV_HEADS, HEAD_DIM))
        new_v.append(vf.reshape(b_ctx, t_ctx, A_KV_HEADS, HEAD_DIM))
        new_s.append(s_fin)

        q, kv, sag, ob, cqkv, scg, ab = _projection(
            xs, mod_lat, gpre, w1, seg_b, lng, lnb, sw, sbias, rope_tabs, keep_f32_kv=False, tm=tm_lat)
        (oc,) = _gdn(cqkv, ab, scg, convw, alog, dtb, normg, seg_c, state_delta[:, l], emit_state=False)
        xs = _attention_out(xs, q, kv, sag, ob, oc, sink_rows, wout, gpost, mod_lat[:, 2:3], band, ck, cv, l,
                            tq=_row_tile(t_lat, 512))

    return (xp, xs, jnp.stack(new_k, axis=1), jnp.stack(new_v, axis=1), jnp.stack(new_s, axis=1))
```

```python
import functools

import numpy as np
import jax
import jax.numpy as jnp
from jax import lax
from jax.experimental import pallas as pl
from jax.experimental.pallas import tpu as pltpu

F32 = jnp.float32
BF16 = jnp.bfloat16

GRID_W = 64
HEAD_DIM = 64
A_HEADS = 8
A_KV_HEADS = 2
A_GROUP = A_HEADS // A_KV_HEADS
A_WIDTH = A_HEADS * HEAD_DIM
A_KV_WIDTH = A_KV_HEADS * HEAD_DIM
ATTN_BLOCK = 128
ROPE_BASE = 10000.0
B_GROUPS = 4
B_GROUP_DIM = 64
B_WIDTH = B_GROUPS * B_GROUP_DIM
SGU_CHUNK = 128
C_HEADS = 4
C_HEAD_DIM = 64
C_WIDTH = C_HEADS * C_HEAD_DIM
CONV_K = 5
GDN_CHUNK = 64
N_DIRS = 2
N_GATES = N_DIRS * C_HEADS
EPS = 1e-6
NEG_INF = -1e30
LOG2_E = 1.4426950408889634

LANES = 128
BF16_SUBLANES = 16
VMEM_LIMIT_BYTES = 56 * 1024 * 1024

GDN_A_CHUNKS = 2

_OFF_Q = 0
_OFF_K = _OFF_Q + A_WIDTH
_OFF_V = _OFF_K + A_KV_WIDTH
_OFF_AG = _OFF_V + A_KV_WIDTH
_OFF_BU = _OFF_AG + A_WIDTH
_OFF_BV = _OFF_BU + B_WIDTH
_OFF_BG = _OFF_BV + B_WIDTH
_OFF_CQKV = _OFF_BG + B_WIDTH
_OFF_CG = _OFF_CQKV + 3 * C_WIDTH
_OFF_AB = _OFF_CG + C_WIDTH
_W1_COLS = _OFF_AB + LANES


def _dot(a, b):
    return jnp.dot(a, b, preferred_element_type=F32)


def _dot_nt(a, b):
    return lax.dot_general(a, b, (((1,), (1,)), ((), ())), preferred_element_type=F32)


def _dot_tn(a, b):
    return lax.dot_general(a, b, (((0,), (0,)), ((), ())), preferred_element_type=F32)


def _split3(x):
    hi = x.astype(BF16)
    r1 = x - hi.astype(F32)
    mid = r1.astype(BF16)
    lo = (r1 - mid.astype(F32)).astype(BF16)
    return hi, mid, lo


def _seg_sum(x, seg):
    hi, mid, lo = _split3(x)
    return _dot(hi, seg) + _dot(mid, seg) + _dot(lo, seg)


def _tri_sum(tri, x):
    hi, mid, lo = _split3(x)
    return _dot(tri, hi) + _dot(tri, mid) + _dot(tri, lo)


def _sigmoid(x):
    return 1.0 / (1.0 + jnp.exp(-x))


def _silu(x):
    return x * _sigmoid(x)


def _softplus(x):
    return jnp.maximum(x, 0.0) + jnp.log1p(jnp.exp(-jnp.abs(x)))


def _mod_body(cond_ref, w_ref, b_ref, o_ref):
    a = _silu(cond_ref[...]).astype(BF16)
    o_ref[...] = _dot(a, w_ref[...].astype(BF16)) + b_ref[...]


def _modulation(conds, w_mod, b_mod):
    depth, d, d3 = w_mod.shape
    rows = conds.shape[0]
    tn = d
    return pl.pallas_call(
        _mod_body,
        grid=(depth, d3 // tn),
        in_specs=[
            pl.BlockSpec((rows, d), lambda l, j: (0, 0)),
            pl.BlockSpec((None, d, tn), lambda l, j: (l, 0, j)),
            pl.BlockSpec((None, 1, tn), lambda l, j: (l, 0, j)),
        ],
        out_specs=pl.BlockSpec((None, rows, tn), lambda l, j: (l, 0, j)),
        out_shape=jax.ShapeDtypeStruct((depth, rows, d3), F32),
        compiler_params=pltpu.CompilerParams(dimension_semantics=("arbitrary", "arbitrary"),
                                             vmem_limit_bytes=VMEM_LIMIT_BYTES),
        name="modulation",
    )(conds, w_mod, b_mod.reshape(depth, 1, d3))


def _rope(z, c, sa, sb):
    w = z.shape[1]
    reps = w // LANES
    if reps > 1:
        c = jnp.concatenate([c] * reps, axis=1)
        sa = jnp.concatenate([sa] * reps, axis=1)
        sb = jnp.concatenate([sb] * reps, axis=1)
    half_pair = HEAD_DIM // 4
    return z * c + pltpu.roll(z, w - half_pair, 1) * sa + pltpu.roll(z, half_pair, 1) * sb


def _proj_body(*refs, rope, keep_f32_kv, tm):
    it = iter(refs)
    x_ref, mod_ref, gpre_ref, w_ref, seg_ref, lng_ref, lnb_ref, sw_ref, sbias_ref = (next(it) for _ in range(9))
    if rope:
        rc_ref, rsa_ref, rsb_ref = (next(it) for _ in range(3))
    q_ref, kv_ref, sag_ref, ob_ref, cqkv_ref, scg_ref, ab_ref = (next(it) for _ in range(7))
    if keep_f32_kv:
        kf_ref, vf_ref = (next(it) for _ in range(2))

    x = x_ref[...]
    ms = jnp.mean(x * x, axis=-1, keepdims=True)
    h = x * lax.rsqrt(ms + EPS) * gpre_ref[...]
    h = h * (1.0 + mod_ref[1:2, :]) + mod_ref[0:1, :]
    hb = h.astype(BF16)

    def proj(lo, width):
        return _dot(hb, w_ref[:, lo:lo + width])

    zq = proj(_OFF_Q, A_WIDTH)
    zk = proj(_OFF_K, A_KV_WIDTH)
    zv = proj(_OFF_V, A_KV_WIDTH)
    if keep_f32_kv:
        kf_ref[...] = zk
        vf_ref[...] = zv
    if rope:
        c, sa, sb = rc_ref[...], rsa_ref[...], rsb_ref[...]
        zq = _rope(zq, c, sa, sb)
        zk = _rope(zk, c, sa, sb)
    q_ref[...] = (zq * (HEAD_DIM ** -0.5 * LOG2_E)).astype(BF16)
    kv_ref[:, 0:A_KV_WIDTH] = zk.astype(BF16)
    kv_ref[:, A_KV_WIDTH:2 * A_KV_WIDTH] = zv.astype(BF16)
    sag_ref[...] = _silu(proj(_OFF_AG, A_WIDTH)).astype(BF16)

    zu = proj(_OFF_BU, B_WIDTH)
    zbv = proj(_OFF_BV, B_WIDTH)
    zbg = proj(_OFF_BG, B_WIDTH)
    seg = seg_ref[...]
    inv_n = 1.0 / B_GROUP_DIM
    mean = _seg_sum(zbv, seg) * inv_n
    xc = zbv - mean
    var = _seg_sum(xc * xc, seg) * inv_n
    vg = (xc * lax.rsqrt(var + EPS) * lng_ref[...] + lnb_ref[...]).astype(BF16)
    sbias = sbias_ref[...]
    mixed_chunks = []
    for ci in range(tm // SGU_CHUNK):
        rows = slice(ci * SGU_CHUNK, (ci + 1) * SGU_CHUNK)
        per_group = [_dot(sw_ref[g], vg[rows, g * B_GROUP_DIM:(g + 1) * B_GROUP_DIM]) for g in range(B_GROUPS)]
        mixed_chunks.append(jnp.concatenate(per_group, axis=1) + sbias)
    mixed = jnp.concatenate(mixed_chunks, axis=0) if len(mixed_chunks) > 1 else mixed_chunks[0]
    ob_ref[...] = (zu * mixed * _silu(zbg)).astype(BF16)

    cqkv_ref[...] = proj(_OFF_CQKV, 3 * C_WIDTH).astype(BF16)
    scg_ref[...] = _silu(proj(_OFF_CG, C_WIDTH)).astype(BF16)
    ab_ref[...] = proj(_OFF_AB, LANES)[:, 0:2 * N_GATES]


def _projection(x, mod3, gpre, w1, seg, lng, lnb, sw, sbias, rope_tabs, *, keep_f32_kv, tm):
    b, t, d = x.shape
    rope = rope_tabs is not None
    nmod = mod3.shape[0]
    mod_map = (lambda bi, i: (bi, 0, 0)) if nmod > 1 else (lambda bi, i: (0, 0, 0))
    const2 = lambda bi, i: (0, 0)
    in_specs = [
        pl.BlockSpec((None, tm, d), lambda bi, i: (bi, i, 0)),
        pl.BlockSpec((None, 3, d), mod_map),
        pl.BlockSpec((1, d), const2),
        pl.BlockSpec(w1.shape, const2),
        pl.BlockSpec(seg.shape, const2),
        pl.BlockSpec((1, B_WIDTH), const2),
        pl.BlockSpec((1, B_WIDTH), const2),
        pl.BlockSpec(sw.shape, lambda bi, i: (0, 0, 0)),
        pl.BlockSpec(sbias.shape, const2),
    ]
    args = [x, mod3, gpre, w1, seg, lng, lnb, sw, sbias]
    if rope:
        in_specs += [pl.BlockSpec((tm, LANES), lambda bi, i: (i, 0))] * 3
        args += list(rope_tabs)

    def tok(width, dtype):
        return (pl.BlockSpec((None, tm, width), lambda bi, i: (bi, i, 0)),
                jax.ShapeDtypeStruct((b, t, width), dtype))

    outs = [tok(A_WIDTH, BF16), tok(2 * A_KV_WIDTH, BF16), tok(A_WIDTH, BF16), tok(B_WIDTH, BF16),
            tok(3 * C_WIDTH, BF16), tok(C_WIDTH, BF16), tok(2 * N_GATES, F32)]
    if keep_f32_kv:
        outs += [tok(A_KV_WIDTH, F32), tok(A_KV_WIDTH, F32)]
    return pl.pallas_call(
        functools.partial(_proj_body, rope=rope, keep_f32_kv=keep_f32_kv, tm=tm),
        grid=(b, t // tm),
        in_specs=in_specs,
        out_specs=[o[0] for o in outs],
        out_shape=[o[1] for o in outs],
        compiler_params=pltpu.CompilerParams(dimension_semantics=("arbitrary", "arbitrary"),
                                             vmem_limit_bytes=VMEM_LIMIT_BYTES),
        name="projection_rope" if rope else "projection",
    )(*args)


def _pair_masks(n):
    rows = lax.broadcasted_iota(jnp.int32, (n, n), 0)
    cols = lax.broadcasted_iota(jnp.int32, (n, n), 1)
    masks = []
    level = 0
    while (1 << level) < n:
        same_pair = (rows >> (level + 1)) == (cols >> (level + 1))
        other_half = (rows >> level) != (cols >> level)
        masks.append(same_pair & other_half)
        level += 1
    return masks, rows == cols


def _gdn_body(*refs, t, has_s0, emit_state, ach):
    it = iter(refs)
    cqkv_ref, ab_ref, scg_ref, convw_ref, alog_ref, dtb_ref, normg_ref, seg_ref = (next(it) for _ in range(8))
    s0_ref = next(it) if has_s0 else None
    oc_ref = next(it)
    sfin_ref = next(it) if emit_state else None
    wq_scr, u_scr, aqk_scr, kd_scr, egl_scr, s_scr = (next(it) for _ in range(6))

    c = GDN_CHUNK
    hd = C_HEAD_DIM
    pw = 2 * hd
    n_pairs = C_HEADS // 2
    nch = t // c
    span = ach * c
    halo = BF16_SUBLANES
    rows = lax.broadcasted_iota(jnp.int32, (c, c), 0)
    cols = lax.broadcasted_iota(jnp.int32, (c, c), 1)
    incl_masks = (rows >= cols, rows <= cols)
    strict_masks = (rows > cols, rows < cols)
    tri_l = jnp.where(incl_masks[0], 1.0, 0.0).astype(BF16)
    tri_u = jnp.where(incl_masks[1], 1.0, 0.0).astype(BF16)
    pair_masks, eye = _pair_masks(c)
    gate_lane = lax.broadcasted_iota(jnp.int32, (1, N_GATES), 1)
    is_fwd = gate_lane < C_HEADS
    seg = seg_ref[...]
    convw = convw_ref[...]
    neg_rate = -jnp.exp(alog_ref[...])
    dtb = dtb_ref[...]

    def a_step(j, first, last):
        if first:
            r0 = 0
            win = jnp.concatenate([jnp.zeros((halo, 3 * C_WIDTH), F32),
                                   cqkv_ref[0:span + halo, :].astype(F32)], axis=0)
        elif last:
            r0 = t - span
            win = jnp.concatenate([cqkv_ref[t - span - halo:t, :].astype(F32),
                                   jnp.zeros((halo, 3 * C_WIDTH), F32)], axis=0)
        else:
            r0 = pl.multiple_of(j * span, span)
            win = cqkv_ref[pl.ds(r0 - halo, span + 2 * halo), :].astype(F32)
        y = jnp.zeros((span, 3 * C_WIDTH), F32)
        for tap in range(CONV_K):
            lo = halo - CONV_K // 2 + tap
            y = y + win[lo:lo + span, :] * convw[tap:tap + 1, :]
        y = _silu(y)
        q_all = y[:, 0:C_WIDTH]
        k_all = y[:, C_WIDTH:2 * C_WIDTH]
        v_all = y[:, 2 * C_WIDTH:3 * C_WIDTH]
        q_all = q_all * lax.rsqrt(_seg_sum(q_all * q_all, seg) + EPS) * (hd ** -0.5)
        k_all = k_all * lax.rsqrt(_seg_sum(k_all * k_all, seg) + EPS)

        ab = ab_ref[pl.ds(r0, span), :]
        g_all = neg_rate * _softplus(ab[:, 0:N_GATES] + dtb)
        beta_all = _sigmoid(ab[:, N_GATES:2 * N_GATES])

        gc, grow, beta, begc, egc, eglgc, egl = [], [], [], [], [], [], []
        for ci in range(ach):
            rs = slice(ci * c, (ci + 1) * c)
            gc_f = _tri_sum(tri_l, g_all[rs])
            gc_b = _tri_sum(tri_u, g_all[rs])
            gcc = jnp.where(is_fwd, gc_f, gc_b)
            gl = jnp.where(is_fwd, gc_f[c - 1:c, :], gc_b[0:1, :])
            e = jnp.exp(gcc)
            gc.append(gcc)
            grow.append(gcc.T)
            beta.append(beta_all[rs])
            begc.append(beta_all[rs] * e)
            egc.append(e)
            eglgc.append(jnp.exp(gl - gcc))
            egl.append(jnp.exp(gl))

        heads = [(ci, hh) for ci in range(ach) for hh in range(C_HEADS)]
        chains = [(ci, d, hh) for ci in range(ach) for d in range(N_DIRS) for hh in range(C_HEADS)]

        def hslice(arr, ci, hh):
            return arr[ci * c:(ci + 1) * c, hh * hd:(hh + 1) * hd]

        def col(vals, ci, d, hh):
            gi = d * C_HEADS + hh
            return vals[ci][:, gi:gi + 1]

        q = {key: hslice(q_all, *key) for key in heads}
        k = {key: hslice(k_all, *key) for key in heads}
        v = {key: hslice(v_all, *key) for key in heads}
        kb16 = {key: k[key].astype(BF16) for key in heads}
        kk = {key: _dot_nt(kb16[key], kb16[key]) for key in heads}
        qk = {key: _dot_nt(q[key].astype(BF16), kb16[key]) for key in heads}

        decay, a_mat = {}, {}
        for ci, d, hh in chains:
            gi = d * C_HEADS + hh
            dec = jnp.where(incl_masks[d], jnp.exp(col(gc, ci, d, hh) - grow[ci][gi:gi + 1, :]), 0.0)
            decay[ci, d, hh] = dec
            a_mat[ci, d, hh] = jnp.where(strict_masks[d], kk[ci, hh] * col(beta, ci, d, hh) * dec, 0.0)

        t_mat = {ch: jnp.where(eye, 1.0, 0.0) - jnp.where(pair_masks[0], a_mat[ch], 0.0) for ch in chains}
        for mask in pair_masks[1:]:
            tb = {ch: t_mat[ch].astype(BF16) for ch in chains}
            et = {ch: _dot(jnp.where(mask, a_mat[ch], 0.0).astype(BF16), tb[ch]).astype(BF16) for ch in chains}
            t_mat = {ch: t_mat[ch] - _dot(tb[ch], et[ch]) for ch in chains}

        uw = {}
        for ci, d, hh in chains:
            rhs = jnp.concatenate([v[ci, hh] * col(beta, ci, d, hh), k[ci, hh] * col(begc, ci, d, hh)], axis=1)
            uw[ci, d, hh] = _dot(t_mat[ci, d, hh].astype(BF16), rhs.astype(BF16))

        for ci in range(ach):
            m = (r0 // c) + ci
            for d in range(N_DIRS):
                hs = range(C_HEADS)
                u_d = jnp.concatenate([uw[ci, d, hh][:, 0:hd] for hh in hs], axis=1)
                w_d = jnp.concatenate([uw[ci, d, hh][:, hd:2 * hd] for hh in hs], axis=1)
                qd_d = jnp.concatenate([q[ci, hh] * col(egc, ci, d, hh) for hh in hs], axis=1)
                aqk_d = jnp.concatenate([qk[ci, hh] * decay[ci, d, hh] for hh in hs], axis=1)
                kd_d = jnp.concatenate([k[ci, hh] * col(eglgc, ci, d, hh) for hh in hs], axis=1)
                egl_d = jnp.concatenate([jnp.broadcast_to(col(egl, ci, d, hh), (1, hd)) for hh in hs], axis=1)
                wq_scr[m, d] = jnp.concatenate([w_d, qd_d], axis=0).astype(BF16)
                u_scr[m, d] = u_d.astype(BF16)
                aqk_scr[m, d] = aqk_d.astype(BF16)
                kd_scr[m, d] = kd_d.astype(BF16)
                egl_scr[m, d] = egl_d

    n_steps = nch // ach
    a_step(0, True, False)
    if n_steps > 2:
        def a_loop(j, carry):
            a_step(j, False, False)
            return carry
        lax.fori_loop(1, n_steps - 1, a_loop, 0)
    a_step(n_steps - 1, False, True)

    zeros_h = jnp.zeros((hd, hd), F32)
    for d in range(N_DIRS):
        for p in range(n_pairs):
            if has_s0:
                top = jnp.concatenate([s0_ref[d, 2 * p], zeros_h], axis=1)
                bot = jnp.concatenate([zeros_h, s0_ref[d, 2 * p + 1]], axis=1)
                s_scr[d, p] = jnp.concatenate([top, bot], axis=0)
            else:
                s_scr[d, p] = jnp.zeros((pw, pw), F32)

    prow = lax.broadcasted_iota(jnp.int32, (pw, pw), 0)
    pcol = lax.broadcasted_iota(jnp.int32, (pw, pw), 1)
    same_head = (prow >= hd) == (pcol >= hd)
    first_head_lanes = lax.broadcasted_iota(jnp.int32, (1, pw), 1) < hd
    pairs = [(d, p) for d in range(N_DIRS) for p in range(n_pairs)]

    def b_step(n, carry):
        ms = (n, nch - 1 - n)
        ls = {(d, p): slice(p * pw, (p + 1) * pw) for d, p in pairs}
        state = {(d, p): s_scr[d, p] for d, p in pairs}
        r = {(d, p): _dot(wq_scr[ms[d], d, :, ls[d, p]], state[d, p].astype(BF16)) for d, p in pairs}
        vnb = {}
        for d, p in pairs:
            v_new = u_scr[ms[d], d, :, ls[d, p]].astype(F32) - r[d, p][0:c, :]
            vnb[d, p] = v_new.astype(BF16)
        for d, p in pairs:
            zero = jnp.zeros_like(vnb[d, p])
            vn_bd = jnp.concatenate([jnp.where(first_head_lanes, vnb[d, p], zero),
                                     jnp.where(first_head_lanes, zero, vnb[d, p])], axis=0)
            o = r[d, p][c:2 * c, :] + _dot(aqk_scr[ms[d], d, :, ls[d, p]], vn_bd)
            kv = _dot_tn(kd_scr[ms[d], d, :, ls[d, p]], vnb[d, p])
            s_scr[d, p] = state[d, p] * egl_scr[ms[d], d, :, ls[d, p]] + jnp.where(same_head, kv, 0.0)
            u_scr[ms[d], d, :, ls[d, p]] = o.astype(BF16)
        return carry

    lax.fori_loop(0, nch, b_step, 0)

    inv_n = 1.0 / hd

    def c_step(m, carry):
        r0 = pl.multiple_of(m * c, c)
        o = u_scr[m, 0].astype(F32) + u_scr[m, 1].astype(F32)
        ms = _seg_sum(o * o, seg) * inv_n
        on = o * lax.rsqrt(ms + EPS) * normg_ref[...]
        oc_ref[pl.ds(r0, c), :] = (on * scg_ref[pl.ds(r0, c), :].astype(F32)).astype(BF16)
        return carry

    lax.fori_loop(0, nch, c_step, 0)

    if emit_state:
        for d in range(N_DIRS):
            for p in range(n_pairs):
                s_pair = s_scr[d, p]
                sfin_ref[d, 2 * p] = s_pair[0:hd, 0:hd]
                sfin_ref[d, 2 * p + 1] = s_pair[hd:pw, hd:pw]


def _gdn(cqkv, ab, scg, convw, alog, dtb, normg, seg, s0, *, emit_state, ach):
    b, t, _ = cqkv.shape
    nch = t // GDN_CHUNK
    assert nch % ach == 0 and nch // ach >= 2
    has_s0 = s0 is not None
    const2 = lambda bi: (0, 0)
    in_specs = [
        pl.BlockSpec((None, t, 3 * C_WIDTH), lambda bi: (bi, 0, 0)),
        pl.BlockSpec((None, t, 2 * N_GATES), lambda bi: (bi, 0, 0)),
        pl.BlockSpec((None, t, C_WIDTH), lambda bi: (bi, 0, 0)),
        pl.BlockSpec(convw.shape, const2),
        pl.BlockSpec((1, N_GATES), const2),
        pl.BlockSpec((1, N_GATES), const2),
        pl.BlockSpec((1, C_WIDTH), const2),
        pl.BlockSpec(seg.shape, const2),
    ]
    args = [cqkv, ab, scg, convw, alog, dtb, normg, seg]
    state_spec = pl.BlockSpec((None, N_DIRS, C_HEADS, C_HEAD_DIM, C_HEAD_DIM), lambda bi: (bi, 0, 0, 0, 0))
    if has_s0:
        in_specs.append(state_spec)
        args.append(s0)
    out_specs = [pl.BlockSpec((None, t, C_WIDTH), lambda bi: (bi, 0, 0))]
    out_shape = [jax.ShapeDtypeStruct((b, t, C_WIDTH), BF16)]
    if emit_state:
        out_specs.append(state_spec)
        out_shape.append(jax.ShapeDtypeStruct((b, N_DIRS, C_HEADS, C_HEAD_DIM, C_HEAD_DIM), F32))
    pair_w = 2 * C_HEAD_DIM
    scratch = [
        pltpu.VMEM((nch, N_DIRS, 2 * GDN_CHUNK, C_WIDTH), BF16),
        pltpu.VMEM((nch, N_DIRS, GDN_CHUNK, C_WIDTH), BF16),
        pltpu.VMEM((nch, N_DIRS, GDN_CHUNK, C_WIDTH), BF16),
        pltpu.VMEM((nch, N_DIRS, GDN_CHUNK, C_WIDTH), BF16),
        pltpu.VMEM((nch, N_DIRS, 1, C_WIDTH), F32),
        pltpu.VMEM((N_DIRS, C_HEADS // 2, pair_w, pair_w), F32),
    ]
    return pl.pallas_call(
        functools.partial(_gdn_body, t=t, has_s0=has_s0, emit_state=emit_state, ach=ach),
        grid=(b,),
        in_specs=in_specs,
        out_specs=out_specs,
        out_shape=out_shape,
        scratch_shapes=scratch,
        compiler_params=pltpu.CompilerParams(dimension_semantics=("arbitrary",),
                                             vmem_limit_bytes=VMEM_LIMIT_BYTES),
        name="gdn_state" if emit_state else "gdn",
    )(*args)


def _attn_body(*refs, banded, tq, nblk_total):
    it = iter(refs)
    x_ref, q_ref, kv_ref = (next(it) for _ in range(3))
    if banded:
        kvl_ref, kvr_ref, band_ref = (next(it) for _ in range(3))
        ck_ref, cv_ref = (next(it) for _ in range(2))
    sink_ref, sag_ref, ob_ref, oc_ref, wout_ref, gpost_ref, gate_ref, y_ref = (next(it) for _ in range(8))

    nb = tq // ATTN_BLOCK
    rows_q = A_GROUP * ATTN_BLOCK
    lane = lax.broadcasted_iota(jnp.int32, (1, A_KV_WIDTH), 1)
    head_mask = [lane < HEAD_DIM, lane >= HEAD_DIM]

    def split_heads(a):
        zero = jnp.zeros_like(a)
        return [jnp.where(head_mask[h], a, zero) for h in range(A_KV_HEADS)]

    if banded:
        kv_ext = jnp.concatenate([kvl_ref[...], kv_ref[...], kvr_ref[...]], axis=0)
        k_own = split_heads(kv_ext[:, 0:A_KV_WIDTH])
        v_own = split_heads(kv_ext[:, A_KV_WIDTH:2 * A_KV_WIDTH])
        k_ctx = split_heads(ck_ref[...].astype(BF16))
        v_ctx = split_heads(cv_ref[...].astype(BF16))
        band_l = band_ref[:, 0:ATTN_BLOCK]
        band_r = band_ref[:, 2 * ATTN_BLOCK:3 * ATTN_BLOCK]
        first_blk = pl.program_id(1) * nb
    else:
        kv_all = kv_ref[...]
        k_ctx = split_heads(kv_all[:, 0:A_KV_WIDTH])
        v_ctx = split_heads(kv_all[:, A_KV_WIDTH:2 * A_KV_WIDTH])

    units = [(j, h) for j in range(nb) for h in range(A_KV_HEADS)]
    q2 = [jnp.concatenate([q_ref[j * ATTN_BLOCK:(j + 1) * ATTN_BLOCK, g * A_KV_WIDTH:(g + 1) * A_KV_WIDTH]
                           for g in range(A_GROUP)], axis=0) for j in range(nb)]

    def scores(j, h):
        s_ctx = _dot_nt(q2[j], k_ctx[h])
        if not banded:
            return s_ctx, None
        s_own = _dot_nt(q2[j], k_own[h][j * ATTN_BLOCK:(j + 3) * ATTN_BLOCK])
        blk = first_blk + j
        thr_l = jnp.where(blk > 0, 0.5, 2.0)
        thr_r = jnp.where(blk < nblk_total - 1, 0.5, 2.0)
        left = jnp.where(band_l > thr_l, s_own[:, 0:ATTN_BLOCK], NEG_INF)
        right = jnp.where(band_r > thr_r, s_own[:, 2 * ATTN_BLOCK:3 * ATTN_BLOCK], NEG_INF)
        return s_ctx, jnp.concatenate([left, s_own[:, ATTN_BLOCK:2 * ATTN_BLOCK], right], axis=1)

    o_acc = [jnp.zeros((rows_q, A_KV_WIDTH), F32) for _ in range(nb)]
    pending = scores(*units[0])
    for ui, (j, h) in enumerate(units):
        s_ctx, s_own = pending
        if ui + 1 < len(units):
            pending = scores(*units[ui + 1])
        sink = sink_ref[h] * LOG2_E
        s_all = jnp.concatenate([s_ctx, s_own], axis=1) if banded else s_ctx
        m = jnp.maximum(jnp.max(s_all, axis=-1, keepdims=True), sink)
        p = jnp.exp2(s_all - m)
        denom = jnp.sum(p, axis=-1, keepdims=True) + jnp.exp2(sink - m)
        pb = p.astype(BF16)
        n_ctx = s_ctx.shape[1]
        acc = _dot(pb[:, 0:n_ctx], v_ctx[h])
        if banded:
            acc = acc + _dot(pb[:, n_ctx:], v_own[h][j * ATTN_BLOCK:(j + 3) * ATTN_BLOCK])
        o_acc[j] = o_acc[j] + acc * (1.0 / denom)
    o_blocks = [jnp.concatenate([o_acc[j][g * ATTN_BLOCK:(g + 1) * ATTN_BLOCK] for g in range(A_GROUP)], axis=1)
                for j in range(nb)]
    o_a = jnp.concatenate(o_blocks, axis=0) if nb > 1 else o_blocks[0]

    mix_a = (o_a * sag_ref[...].astype(F32)).astype(BF16)
    mix = (_dot(mix_a, wout_ref[0:A_WIDTH, :])
           + _dot(ob_ref[...], wout_ref[A_WIDTH:A_WIDTH + B_WIDTH, :])
           + _dot(oc_ref[...], wout_ref[A_WIDTH + B_WIDTH:A_WIDTH + B_WIDTH + C_WIDTH, :]))
    ms = jnp.mean(mix * mix, axis=-1, keepdims=True)
    y_ref[...] = x_ref[...] + gate_ref[...] * (mix * lax.rsqrt(ms + EPS) * gpost_ref[...])


def _attention_out(x, q, kv, sag, ob, oc, sink_rows, wout, gpost, gate, band, ctx_k, ctx_v, layer, *, tq):
    b, t, d = x.shape
    banded = band is not None
    nblk_total = t // ATTN_BLOCK
    per_q = tq // ATTN_BLOCK
    ngate = gate.shape[0]
    gate_map = (lambda bi, i: (bi, 0, 0)) if ngate > 1 else (lambda bi, i: (0, 0, 0))
    const2 = lambda bi, i: (0, 0)

    def tok(width):
        return pl.BlockSpec((None, tq, width), lambda bi, i: (bi, i, 0))

    in_specs = [tok(d), tok(A_WIDTH)]
    args = [x, q]
    if banded:
        in_specs.append(tok(2 * A_KV_WIDTH))
        args.append(kv)
        in_specs.append(pl.BlockSpec((None, ATTN_BLOCK, 2 * A_KV_WIDTH),
                                     lambda bi, i: (bi, jnp.maximum(i * per_q - 1, 0), 0)))
        in_specs.append(pl.BlockSpec((None, ATTN_BLOCK, 2 * A_KV_WIDTH),
                                     lambda bi, i: (bi, jnp.minimum((i + 1) * per_q, nblk_total - 1), 0)))
        in_specs.append(pl.BlockSpec(band.shape, const2))
        args += [kv, kv, band]
        past = ctx_k.shape[2]
        in_specs += [pl.BlockSpec((None, None, past, A_KV_WIDTH), lambda bi, i: (bi, layer, 0, 0))] * 2
        args += [ctx_k, ctx_v]
    else:
        in_specs.append(pl.BlockSpec((None, t, 2 * A_KV_WIDTH), lambda bi, i: (bi, 0, 0)))
        args.append(kv)
    in_specs += [
        pl.BlockSpec(sink_rows.shape, lambda bi, i: (0, 0, 0)),
        tok(A_WIDTH), tok(B_WIDTH), tok(C_WIDTH),
        pl.BlockSpec(wout.shape, const2),
        pl.BlockSpec((1, d), const2),
        pl.BlockSpec((None, 1, d), gate_map),
    ]
    args += [sink_rows, sag, ob, oc, wout, gpost, gate]
    return pl.pallas_call(
        functools.partial(_attn_body, banded=banded, tq=tq, nblk_total=nblk_total),
        grid=(b, t // tq),
        in_specs=in_specs,
        out_specs=tok(d),
        out_shape=jax.ShapeDtypeStruct((b, t, d), F32),
        compiler_params=pltpu.CompilerParams(dimension_semantics=("arbitrary", "arbitrary"),
                                             vmem_limit_bytes=VMEM_LIMIT_BYTES),
        name="attention_banded" if banded else "attention_full",
    )(*args)


def _block_ones(width, block):
    idx = np.arange(width) // block
    return jnp.asarray((idx[:, None] == idx[None, :]).astype(np.float32), dtype=BF16)


def _rope_tables(n):
    half = HEAD_DIM // 2
    nf = half // 2
    pos = np.arange(n)
    row = (pos // GRID_W).astype(np.float32)
    colp = (pos % GRID_W).astype(np.float32)
    inv_freq = jnp.asarray(ROPE_BASE, F32) ** (-jnp.arange(nf, dtype=F32) / nf)
    lane = np.arange(LANES) % HEAD_DIM
    freq_idx = lane % nf
    quarter = lane // nf
    p = jnp.where(jnp.asarray(quarter < 2)[None, :], jnp.asarray(row)[:, None], jnp.asarray(colp)[:, None])
    ang = p * inv_freq[freq_idx][None, :]
    cos, sin = jnp.cos(ang), jnp.sin(ang)
    first = jnp.asarray(quarter % 2 == 0)[None, :]
    sa = jnp.where(first, -sin, 0.0)
    sb = jnp.where(first, 0.0, sin)
    return cos.astype(F32), sa.astype(F32), sb.astype(F32)


def _band_table():
    q_rel = np.arange(ATTN_BLOCK)
    k_rel = np.arange(3 * ATTN_BLOCK) - ATTN_BLOCK
    band = (np.abs(q_rel[:, None] - k_rel[None, :]) <= ATTN_BLOCK).astype(np.float32)
    return jnp.asarray(np.tile(band, (A_GROUP, 1)))


def _head_perm():
    idx = np.arange(A_WIDTH).reshape(A_KV_HEADS, A_GROUP, HEAD_DIM)
    return np.transpose(idx, (1, 0, 2)).reshape(-1)


def _layout_w_in(w_in_l):
    perm = _head_perm()
    o = 0
    sec = {}
    for name, width in (("aq", A_WIDTH), ("ak", A_KV_WIDTH), ("av", A_KV_WIDTH), ("ag", A_WIDTH),
                        ("bu", B_WIDTH), ("bv", B_WIDTH), ("bg", B_WIDTH),
                        ("cq", C_WIDTH), ("ck", C_WIDTH), ("cv", C_WIDTH), ("ca", N_GATES), ("cb", N_GATES),
                        ("cg", C_WIDTH)):
        sec[name] = w_in_l[:, o:o + width]
        o += width
    d = w_in_l.shape[0]
    pad = jnp.zeros((d, LANES - 2 * N_GATES), w_in_l.dtype)
    cols = [sec["aq"][:, perm], sec["ak"], sec["av"], sec["ag"][:, perm], sec["bu"], sec["bv"], sec["bg"],
            sec["cq"], sec["ck"], sec["cv"], sec["cg"], sec["ca"], sec["cb"], pad]
    return jnp.concatenate(cols, axis=1).astype(BF16)


def _row_tile(t, target):
    tile = min(t, target)
    while t % tile:
        tile -= ATTN_BLOCK
    return tile


def kernel(x_prompt, x_sample, cache_k, cache_v, state_delta, c, c_ctx, w_mod, b_mod, g_pre, w_in, attn_sink,
           sgu_ln_g, sgu_ln_b, sgu_w, sgu_b, gdn_conv_w, gdn_a_log, gdn_dt_bias, gdn_norm_g, g_post, w_out):
    depth, d, _ = w_mod.shape
    b_ctx, t_ctx, _ = x_prompt.shape
    b_lat, t_lat, _ = x_sample.shape
    past = cache_k.shape[2]
    assert t_ctx % SGU_CHUNK == 0 and t_lat % SGU_CHUNK == 0 and t_lat % GRID_W == 0
    assert t_ctx // GDN_CHUNK >= 2 and t_lat // GDN_CHUNK >= 2
    assert w_in.shape[2] == 2 * A_WIDTH + 2 * A_KV_WIDTH + 3 * B_WIDTH + 4 * C_WIDTH + 2 * N_GATES

    n_cond = b_lat + 1
    rows = -(-n_cond // 8) * 8
    conds = jnp.concatenate([c, c_ctx[None, :], jnp.zeros((rows - n_cond, d), c.dtype)], axis=0)
    mod = _modulation(conds, w_mod, b_mod).reshape(depth, rows, 3, d)

    seg_b = _block_ones(B_WIDTH, B_GROUP_DIM)
    seg_c = _block_ones(C_WIDTH, C_HEAD_DIM)
    rope_tabs = _rope_tables(t_lat)
    band = _band_table()
    perm = _head_perm()
    ck = cache_k.reshape(b_lat, depth, past, A_KV_WIDTH)
    cv = cache_v.reshape(b_lat, depth, past, A_KV_WIDTH)
    tm_ctx, tm_lat = _row_tile(t_ctx, 512), _row_tile(t_lat, 512)

    xp, xs = x_prompt, x_sample
    new_k, new_v, new_s = [], [], []
    for l in range(depth):
        w1 = _layout_w_in(w_in[l])
        wout = jnp.concatenate([w_out[l][0:A_WIDTH][perm], w_out[l][A_WIDTH:]], axis=0).astype(BF16)
        gpre = g_pre[l].reshape(1, d)
        gpost = g_post[l].reshape(1, d)
        lng = sgu_ln_g[l].reshape(1, B_WIDTH)
        lnb = sgu_ln_b[l].reshape(1, B_WIDTH)
        sw = sgu_w[l].astype(BF16)
        sbias = jnp.repeat(jnp.swapaxes(sgu_b[l], 0, 1), B_GROUP_DIM, axis=1)
        sink = attn_sink[l].reshape(A_KV_HEADS, A_GROUP)
        sink_rows = jnp.repeat(sink, ATTN_BLOCK, axis=1)[:, :, None]
        alog = gdn_a_log[l].reshape(1, N_GATES)
        dtb = gdn_dt_bias[l].reshape(1, N_GATES)
        normg = jnp.tile(gdn_norm_g[l], C_HEADS).reshape(1, C_WIDTH)
        convw = gdn_conv_w[l]
        mod_lat = mod[l, 0:b_lat]
        mod_ctx = mod[l, b_lat:b_lat + 1]

        q, kv, sag, ob, cqkv, scg, ab, kf, vf = _projection(
            xp, mod_ctx, gpre, w1, seg_b, lng, lnb, sw, sbias, None, keep_f32_kv=True, tm=tm_ctx)
        oc, s_fin = _gdn(cqkv, ab, scg, convw, alog, dtb, normg, seg_c, None, emit_state=True, ach=GDN_A_CHUNKS)
        xp = _attention_out(xp, q, kv, sag, ob, oc, sink_rows, wout, gpost, mod_ctx[:, 2:3], None, None, None, l,
                            tq=t_ctx)
        new_k.append(kf.reshape(b_ctx, t_ctx, A_KV_HEADS, HEAD_DIM))
        new_v.append(vf.reshape(b_ctx, t_ctx, A_KV_HEADS, HEAD_DIM))
        new_s.append(s_fin)

        q, kv, sag, ob, cqkv, scg, ab = _projection(
            xs, mod_lat, gpre, w1, seg_b, lng, lnb, sw, sbias, rope_tabs, keep_f32_kv=False, tm=tm_lat)
        (oc,) = _gdn(cqkv, ab, scg, convw, alog, dtb, normg, seg_c, state_delta[:, l], emit_state=False,
                     ach=GDN_A_CHUNKS)
        xs = _attention_out(xs, q, kv, sag, ob, oc, sink_rows, wout, gpost, mod_lat[:, 2:3], band, ck, cv, l,
                            tq=_row_tile(t_lat, 512))

    return (xp, xs, jnp.stack(new_k, axis=1), jnp.stack(new_v, axis=1), jnp.stack(new_s, axis=1))
```

```python
import functools

import numpy as np
import jax
import jax.numpy as jnp
from jax import lax
from jax.experimental import pallas as pl
from jax.experimental.pallas import tpu as pltpu

F32 = jnp.float32
BF16 = jnp.bfloat16

GRID_W = 64
HEAD_DIM = 64
A_HEADS = 8
A_KV_HEADS = 2
A_GROUP = A_HEADS // A_KV_HEADS
A_WIDTH = A_HEADS * HEAD_DIM
A_KV_WIDTH = A_KV_HEADS * HEAD_DIM
ATTN_BLOCK = 128
ROPE_BASE = 10000.0
B_GROUPS = 4
B_GROUP_DIM = 64
B_WIDTH = B_GROUPS * B_GROUP_DIM
SGU_CHUNK = 128
C_HEADS = 4
C_HEAD_DIM = 64
C_WIDTH = C_HEADS * C_HEAD_DIM
CONV_K = 5
GDN_CHUNK = 64
N_DIRS = 2
N_GATES = N_DIRS * C_HEADS
EPS = 1e-6
NEG_INF = -1e30
LOG2_E = 1.4426950408889634

LANES = 128
BF16_SUBLANES = 16
VMEM_LIMIT_BYTES = 56 * 1024 * 1024

GDN_A_CHUNKS = 4

_OFF_Q = 0
_OFF_K = _OFF_Q + A_WIDTH
_OFF_V = _OFF_K + A_KV_WIDTH
_OFF_AG = _OFF_V + A_KV_WIDTH
_OFF_BU = _OFF_AG + A_WIDTH
_OFF_BV = _OFF_BU + B_WIDTH
_OFF_BG = _OFF_BV + B_WIDTH
_OFF_CQKV = _OFF_BG + B_WIDTH
_OFF_CG = _OFF_CQKV + 3 * C_WIDTH
_OFF_AB = _OFF_CG + C_WIDTH
_W1_COLS = _OFF_AB + LANES


def _dot(a, b):
    return jnp.dot(a, b, preferred_element_type=F32)


def _dot_nt(a, b):
    return lax.dot_general(a, b, (((1,), (1,)), ((), ())), preferred_element_type=F32)


def _dot_tn(a, b):
    return lax.dot_general(a, b, (((0,), (0,)), ((), ())), preferred_element_type=F32)


def _split3(x):
    hi = x.astype(BF16)
    r1 = x - hi.astype(F32)
    mid = r1.astype(BF16)
    lo = (r1 - mid.astype(F32)).astype(BF16)
    return hi, mid, lo


def _seg_sum(x, seg, terms=3):
    hi, mid, lo = _split3(x)
    out = _dot(hi, seg) + _dot(mid, seg)
    return out + _dot(lo, seg) if terms == 3 else out


def _tri_sum(tri, x):
    hi, mid, lo = _split3(x)
    return _dot(tri, hi) + _dot(tri, mid) + _dot(tri, lo)


def _sigmoid(x):
    return 1.0 / (1.0 + jnp.exp(-x))


def _silu(x):
    return x * _sigmoid(x)


def _softplus(x):
    return jnp.maximum(x, 0.0) + jnp.log1p(jnp.exp(-jnp.abs(x)))


def _mod_body(cond_ref, w_ref, b_ref, o_ref):
    a = _silu(cond_ref[...]).astype(BF16)
    o_ref[...] = _dot(a, w_ref[...].astype(BF16)) + b_ref[...]


def _modulation(conds, w_mod, b_mod):
    depth, d, d3 = w_mod.shape
    rows = conds.shape[0]
    tn = d
    return pl.pallas_call(
        _mod_body,
        grid=(depth, d3 // tn),
        in_specs=[
            pl.BlockSpec((rows, d), lambda l, j: (0, 0)),
            pl.BlockSpec((None, d, tn), lambda l, j: (l, 0, j)),
            pl.BlockSpec((None, 1, tn), lambda l, j: (l, 0, j)),
        ],
        out_specs=pl.BlockSpec((None, rows, tn), lambda l, j: (l, 0, j)),
        out_shape=jax.ShapeDtypeStruct((depth, rows, d3), F32),
        compiler_params=pltpu.CompilerParams(dimension_semantics=("arbitrary", "arbitrary"),
                                             vmem_limit_bytes=VMEM_LIMIT_BYTES),
        name="modulation",
    )(conds, w_mod, b_mod.reshape(depth, 1, d3))


def _rope(z, c, sa, sb):
    w = z.shape[1]
    reps = w // LANES
    if reps > 1:
        c = jnp.concatenate([c] * reps, axis=1)
        sa = jnp.concatenate([sa] * reps, axis=1)
        sb = jnp.concatenate([sb] * reps, axis=1)
    half_pair = HEAD_DIM // 4
    return z * c + pltpu.roll(z, w - half_pair, 1) * sa + pltpu.roll(z, half_pair, 1) * sb


def _proj_body(*refs, rope, keep_f32_kv, tm):
    it = iter(refs)
    x_ref, mod_ref, gpre_ref, w_ref, seg_ref, lng_ref, lnb_ref, sw_ref, sbias_ref = (next(it) for _ in range(9))
    if rope:
        rc_ref, rsa_ref, rsb_ref = (next(it) for _ in range(3))
    q_ref, kv_ref, sag_ref, ob_ref, cqkv_ref, scg_ref, ab_ref = (next(it) for _ in range(7))
    if keep_f32_kv:
        kf_ref, vf_ref = (next(it) for _ in range(2))

    x = x_ref[...]
    ms = jnp.mean(x * x, axis=-1, keepdims=True)
    h = x * lax.rsqrt(ms + EPS) * gpre_ref[...]
    h = h * (1.0 + mod_ref[1:2, :]) + mod_ref[0:1, :]
    hb = h.astype(BF16)

    def proj(lo, width):
        return _dot(hb, w_ref[:, lo:lo + width])

    zq = proj(_OFF_Q, A_WIDTH)
    zk = proj(_OFF_K, A_KV_WIDTH)
    zv = proj(_OFF_V, A_KV_WIDTH)
    if keep_f32_kv:
        kf_ref[...] = zk
        vf_ref[...] = zv
    if rope:
        c, sa, sb = rc_ref[...], rsa_ref[...], rsb_ref[...]
        zq = _rope(zq, c, sa, sb)
        zk = _rope(zk, c, sa, sb)
    q_ref[...] = (zq * (HEAD_DIM ** -0.5 * LOG2_E)).astype(BF16)
    kv_ref[:, 0:A_KV_WIDTH] = zk.astype(BF16)
    kv_ref[:, A_KV_WIDTH:2 * A_KV_WIDTH] = zv.astype(BF16)
    sag_ref[...] = _silu(proj(_OFF_AG, A_WIDTH)).astype(BF16)

    zu = proj(_OFF_BU, B_WIDTH)
    zbv = proj(_OFF_BV, B_WIDTH)
    zbg = proj(_OFF_BG, B_WIDTH)
    seg = seg_ref[...]
    inv_n = 1.0 / B_GROUP_DIM
    mean = _seg_sum(zbv, seg) * inv_n
    xc = zbv - mean
    var = _seg_sum(xc * xc, seg) * inv_n
    vg = (xc * lax.rsqrt(var + EPS) * lng_ref[...] + lnb_ref[...]).astype(BF16)
    sbias = sbias_ref[...]
    mixed_chunks = []
    for ci in range(tm // SGU_CHUNK):
        rows = slice(ci * SGU_CHUNK, (ci + 1) * SGU_CHUNK)
        per_group = [_dot(sw_ref[g], vg[rows, g * B_GROUP_DIM:(g + 1) * B_GROUP_DIM]) for g in range(B_GROUPS)]
        mixed_chunks.append(jnp.concatenate(per_group, axis=1) + sbias)
    mixed = jnp.concatenate(mixed_chunks, axis=0) if len(mixed_chunks) > 1 else mixed_chunks[0]
    ob_ref[...] = (zu * mixed * _silu(zbg)).astype(BF16)

    cqkv_ref[...] = proj(_OFF_CQKV, 3 * C_WIDTH).astype(BF16)
    scg_ref[...] = _silu(proj(_OFF_CG, C_WIDTH)).astype(BF16)
    ab_ref[...] = proj(_OFF_AB, LANES)[:, 0:2 * N_GATES]


def _projection(x, mod3, gpre, w1, seg, lng, lnb, sw, sbias, rope_tabs, *, keep_f32_kv, tm):
    b, t, d = x.shape
    rope = rope_tabs is not None
    nmod = mod3.shape[0]
    mod_map = (lambda bi, i: (bi, 0, 0)) if nmod > 1 else (lambda bi, i: (0, 0, 0))
    const2 = lambda bi, i: (0, 0)
    in_specs = [
        pl.BlockSpec((None, tm, d), lambda bi, i: (bi, i, 0)),
        pl.BlockSpec((None, 3, d), mod_map),
        pl.BlockSpec((1, d), const2),
        pl.BlockSpec(w1.shape, const2),
        pl.BlockSpec(seg.shape, const2),
        pl.BlockSpec((1, B_WIDTH), const2),
        pl.BlockSpec((1, B_WIDTH), const2),
        pl.BlockSpec(sw.shape, lambda bi, i: (0, 0, 0)),
        pl.BlockSpec(sbias.shape, const2),
    ]
    args = [x, mod3, gpre, w1, seg, lng, lnb, sw, sbias]
    if rope:
        in_specs += [pl.BlockSpec((tm, LANES), lambda bi, i: (i, 0))] * 3
        args += list(rope_tabs)

    def tok(width, dtype):
        return (pl.BlockSpec((None, tm, width), lambda bi, i: (bi, i, 0)),
                jax.ShapeDtypeStruct((b, t, width), dtype))

    outs = [tok(A_WIDTH, BF16), tok(2 * A_KV_WIDTH, BF16), tok(A_WIDTH, BF16), tok(B_WIDTH, BF16),
            tok(3 * C_WIDTH, BF16), tok(C_WIDTH, BF16), tok(2 * N_GATES, F32)]
    if keep_f32_kv:
        outs += [tok(A_KV_WIDTH, F32), tok(A_KV_WIDTH, F32)]
    return pl.pallas_call(
        functools.partial(_proj_body, rope=rope, keep_f32_kv=keep_f32_kv, tm=tm),
        grid=(b, t // tm),
        in_specs=in_specs,
        out_specs=[o[0] for o in outs],
        out_shape=[o[1] for o in outs],
        compiler_params=pltpu.CompilerParams(dimension_semantics=("arbitrary", "arbitrary"),
                                             vmem_limit_bytes=VMEM_LIMIT_BYTES),
        name="projection_rope" if rope else "projection",
    )(*args)


def _tri_masks(n, reps):
    rows = lax.broadcasted_iota(jnp.int32, (n, reps * n), 0)
    cols = lax.broadcasted_iota(jnp.int32, (n, reps * n), 1) & (n - 1)
    masks = []
    level = 0
    while (1 << level) < n:
        same_pair = (rows >> (level + 1)) == (cols >> (level + 1))
        other_half = (rows >> level) != (cols >> level)
        masks.append(same_pair & other_half)
        level += 1
    return rows, cols, rows == cols, masks


def _gdn_body(*refs, t, has_s0, emit_state, ach):
    it = iter(refs)
    cqkv_ref, ab_ref, scg_ref, convw_ref, alog_ref, dtb_ref, normg_ref, seg_ref, expand_ref = (
        next(it) for _ in range(9))
    s0_ref = next(it) if has_s0 else None
    oc_ref = next(it)
    sfin_ref = next(it) if emit_state else None
    wq_scr, u_scr, aqk_scr, kd_scr, egl_scr, s_scr, win_scr = (next(it) for _ in range(7))

    c = GDN_CHUNK
    hd = C_HEAD_DIM
    pw = 2 * hd
    n_pairs = C_HEADS // 2
    nch = t // c
    span = ach * c
    halo = BF16_SUBLANES
    rows, cols, _, _ = _tri_masks(c, 1)
    tri_l = jnp.where(rows >= cols, 1.0, 0.0).astype(BF16)
    tri_u = jnp.where(rows <= cols, 1.0, 0.0).astype(BF16)
    prow_, pcol_, eye_pw, level_pw = _tri_masks(c, 2)
    incl_pw = (prow_ >= pcol_, prow_ <= pcol_)
    strict_pw = (prow_ > pcol_, prow_ < pcol_)
    gate_lane = lax.broadcasted_iota(jnp.int32, (1, N_GATES), 1)
    is_fwd = gate_lane < C_HEADS
    seg = seg_ref[...]
    convw = convw_ref[...]
    neg_rate = -jnp.exp(alog_ref[...])
    dtb = dtb_ref[...]

    lane_pw = lax.broadcasted_iota(jnp.int32, (1, pw), 1)
    first_lanes = lane_pw < hd

    def block_diag(xb):
        zero = jnp.zeros_like(xb)
        return jnp.concatenate([jnp.where(first_lanes, xb, zero), jnp.where(first_lanes, zero, xb)], axis=0)

    expand = expand_ref[...]

    def a_step(j, first, last):
        zeros_halo = jnp.zeros((halo, 3 * C_WIDTH), F32)
        if first:
            r0 = 0
            win_scr[0:halo, :] = zeros_halo
            n_in = span + (0 if last else halo)
            win_scr[halo:halo + n_in, :] = cqkv_ref[0:n_in, :].astype(F32)
            if last:
                win_scr[halo + span:2 * halo + span, :] = zeros_halo
        elif last:
            r0 = t - span
            win_scr[0:halo + span, :] = cqkv_ref[t - span - halo:t, :].astype(F32)
            win_scr[halo + span:2 * halo + span, :] = zeros_halo
        else:
            r0 = pl.multiple_of(j * span, span)
            win_scr[...] = cqkv_ref[pl.ds(r0 - halo, span + 2 * halo), :].astype(F32)
        y = win_scr[halo:halo + span, :] * convw[CONV_K // 2:CONV_K // 2 + 1, :]
        for tap in range(CONV_K):
            if tap != CONV_K // 2:
                lo = halo - CONV_K // 2 + tap
                y = y + win_scr[lo:lo + span, :] * convw[tap:tap + 1, :]
        y = _silu(y)
        q_all = y[:, 0:C_WIDTH]
        k_all = y[:, C_WIDTH:2 * C_WIDTH]
        v_all = y[:, 2 * C_WIDTH:3 * C_WIDTH]
        q_all = q_all * lax.rsqrt(_seg_sum(q_all * q_all, seg, terms=2) + EPS) * (hd ** -0.5)
        k_all = k_all * lax.rsqrt(_seg_sum(k_all * k_all, seg, terms=2) + EPS)

        ab = ab_ref[pl.ds(r0, span), :]
        g_all = neg_rate * _softplus(ab[:, 0:N_GATES] + dtb)
        beta_all = _sigmoid(ab[:, N_GATES:2 * N_GATES])

        gc_x, beta_x, e_x, begc_x, eglgc_x, egl_x, grow = [], [], [], [], [], [], []
        for ci in range(ach):
            rs = slice(ci * c, (ci + 1) * c)
            gc_f = _tri_sum(tri_l, g_all[rs])
            gc_b = _tri_sum(tri_u, g_all[rs])
            gcc = jnp.where(is_fwd, gc_f, gc_b)
            gl = jnp.where(is_fwd, gc_f[c - 1:c, :], gc_b[0:1, :])
            grow.append(gcc.T)
            stacked = jnp.concatenate([gcc, jnp.broadcast_to(gl, (8, N_GATES)), beta_all[rs]], axis=0)
            hi, mid, lo = _split3(stacked)
            ex = _dot(hi, expand) + _dot(mid, expand) + _dot(lo, expand)
            gx = ex[0:c, :]
            glx = ex[c:c + 1, :]
            bx = ex[c + 8:2 * c + 8, :]
            ee = jnp.exp(gx)
            gc_x.append(gx)
            beta_x.append(bx)
            e_x.append(ee)
            begc_x.append(bx * ee)
            eglgc_x.append(jnp.exp(glx - gx))
            egl_x.append(jnp.exp(glx))

        cps = [(ci, p) for ci in range(ach) for p in range(n_pairs)]
        chains = [(ci, d, p) for ci in range(ach) for d in range(N_DIRS) for p in range(n_pairs)]

        def pslice(arr, ci, p):
            return arr[ci * c:(ci + 1) * c, p * pw:(p + 1) * pw]

        def xs(vals, ci, d, p):
            lo = d * C_WIDTH + p * pw
            return vals[ci][:, lo:lo + pw]

        kp16 = {key: pslice(k_all, *key).astype(BF16) for key in cps}
        k_bd = {key: block_diag(kp16[key]) for key in cps}
        kk = {key: _dot_nt(kp16[key], k_bd[key]) for key in cps}
        qk = {key: _dot_nt(pslice(q_all, *key).astype(BF16), k_bd[key]) for key in cps}

        decay, a_mat = {}, {}
        for ci, d, p in chains:
            g0 = d * C_HEADS + 2 * p
            grow_pair = jnp.concatenate([grow[ci][g0:g0 + 1, :], grow[ci][g0 + 1:g0 + 2, :]], axis=1)
            dec = jnp.where(incl_pw[d], jnp.exp(xs(gc_x, ci, d, p) - grow_pair), 0.0)
            decay[ci, d, p] = dec
            a_mat[ci, d, p] = jnp.where(strict_pw[d], kk[ci, p] * xs(beta_x, ci, d, p) * dec, 0.0)

        t_mat = {ch: jnp.where(eye_pw, 1.0, 0.0) - jnp.where(level_pw[0], a_mat[ch], 0.0) for ch in chains}
        for mask in level_pw[1:]:
            tb = {ch: t_mat[ch].astype(BF16) for ch in chains}
            t_bd = {ch: block_diag(tb[ch]) for ch in chains}
            et = {ch: _dot(jnp.where(mask, a_mat[ch], 0.0).astype(BF16), t_bd[ch]).astype(BF16) for ch in chains}
            t_mat = {ch: t_mat[ch] - _dot(tb[ch], block_diag(et[ch])) for ch in chains}

        uw = {}
        for ci, d, p in chains:
            vb = (pslice(v_all, ci, p) * xs(beta_x, ci, d, p)).astype(BF16)
            kbe = (pslice(k_all, ci, p) * xs(begc_x, ci, d, p)).astype(BF16)
            rhs = jnp.concatenate([block_diag(vb), block_diag(kbe)], axis=1)
            uw[ci, d, p] = _dot(t_mat[ci, d, p].astype(BF16), rhs)

        for ci in range(ach):
            m = (r0 // c) + ci
            q_c = q_all[ci * c:(ci + 1) * c, :]
            k_c = k_all[ci * c:(ci + 1) * c, :]
            for d in range(N_DIRS):
                dl = slice(d * C_WIDTH, (d + 1) * C_WIDTH)
                wq_scr[m, d, c:2 * c, :] = (q_c * e_x[ci][:, dl]).astype(BF16)
                kd_scr[m, d] = (k_c * eglgc_x[ci][:, dl]).astype(BF16)
                egl_scr[m, d] = egl_x[ci][:, dl]
                for p in range(n_pairs):
                    pl_ = slice(p * pw, (p + 1) * pw)
                    u_scr[m, d, :, pl_] = uw[ci, d, p][:, 0:pw].astype(BF16)
                    wq_scr[m, d, 0:c, pl_] = uw[ci, d, p][:, pw:2 * pw].astype(BF16)
                    aqk_scr[m, d, :, pl_] = (qk[ci, p] * decay[ci, d, p]).astype(BF16)

    n_steps = nch // ach
    if n_steps == 1:
        a_step(0, True, True)
    else:
        a_step(0, True, False)
        if n_steps > 2:
            def a_loop(j, carry):
                a_step(j, False, False)
                return carry
            lax.fori_loop(1, n_steps - 1, a_loop, 0)
        a_step(n_steps - 1, False, True)

    zeros_h = jnp.zeros((hd, hd), F32)
    for d in range(N_DIRS):
        for p in range(n_pairs):
            if has_s0:
                top = jnp.concatenate([s0_ref[d, 2 * p], zeros_h], axis=1)
                bot = jnp.concatenate([zeros_h, s0_ref[d, 2 * p + 1]], axis=1)
                s_scr[d, p] = jnp.concatenate([top, bot], axis=0)
            else:
                s_scr[d, p] = jnp.zeros((pw, pw), F32)

    prow = lax.broadcasted_iota(jnp.int32, (pw, pw), 0)
    pcol = lax.broadcasted_iota(jnp.int32, (pw, pw), 1)
    same_head = (prow >= hd) == (pcol >= hd)
    first_head_lanes = lax.broadcasted_iota(jnp.int32, (1, pw), 1) < hd
    pairs = [(d, p) for d in range(N_DIRS) for p in range(n_pairs)]

    def b_step(n, carry):
        ms = (n, nch - 1 - n)
        ls = {(d, p): slice(p * pw, (p + 1) * pw) for d, p in pairs}
        state = {(d, p): s_scr[d, p] for d, p in pairs}
        r = {(d, p): _dot(wq_scr[ms[d], d, :, ls[d, p]], state[d, p].astype(BF16)) for d, p in pairs}
        vnb = {}
        for d, p in pairs:
            v_new = u_scr[ms[d], d, :, ls[d, p]].astype(F32) - r[d, p][0:c, :]
            vnb[d, p] = v_new.astype(BF16)
        for d, p in pairs:
            zero = jnp.zeros_like(vnb[d, p])
            vn_bd = jnp.concatenate([jnp.where(first_head_lanes, vnb[d, p], zero),
                                     jnp.where(first_head_lanes, zero, vnb[d, p])], axis=0)
            o = r[d, p][c:2 * c, :] + _dot(aqk_scr[ms[d], d, :, ls[d, p]], vn_bd)
            kv = _dot_tn(kd_scr[ms[d], d, :, ls[d, p]], vnb[d, p])
            s_scr[d, p] = state[d, p] * egl_scr[ms[d], d, :, ls[d, p]] + jnp.where(same_head, kv, 0.0)
            u_scr[ms[d], d, :, ls[d, p]] = o.astype(BF16)
        return carry

    lax.fori_loop(0, nch, b_step, 0)

    inv_n = 1.0 / hd

    c_chunks = 4 if nch % 4 == 0 else 1

    def c_step(i, carry):
        m0 = i * c_chunks
        o = jnp.concatenate([u_scr[m0 + ci, 0].astype(F32) + u_scr[m0 + ci, 1].astype(F32)
                             for ci in range(c_chunks)], axis=0)
        r0 = pl.multiple_of(m0 * c, c_chunks * c)
        ms = _seg_sum(o * o, seg, terms=2) * inv_n
        on = o * lax.rsqrt(ms + EPS) * normg_ref[...]
        oc_ref[pl.ds(r0, c_chunks * c), :] = (on * scg_ref[pl.ds(r0, c_chunks * c), :].astype(F32)).astype(BF16)
        return carry

    lax.fori_loop(0, nch // c_chunks, c_step, 0)

    if emit_state:
        for d in range(N_DIRS):
            for p in range(n_pairs):
                s_pair = s_scr[d, p]
                sfin_ref[d, 2 * p] = s_pair[0:hd, 0:hd]
                sfin_ref[d, 2 * p + 1] = s_pair[hd:pw, hd:pw]


def _gdn(cqkv, ab, scg, convw, alog, dtb, normg, seg, expand, s0, *, emit_state, ach):
    b, t, _ = cqkv.shape
    nch = t // GDN_CHUNK
    assert nch % ach == 0
    has_s0 = s0 is not None
    const2 = lambda bi: (0, 0)
    in_specs = [
        pl.BlockSpec((None, t, 3 * C_WIDTH), lambda bi: (bi, 0, 0)),
        pl.BlockSpec((None, t, 2 * N_GATES), lambda bi: (bi, 0, 0)),
        pl.BlockSpec((None, t, C_WIDTH), lambda bi: (bi, 0, 0)),
        pl.BlockSpec(convw.shape, const2),
        pl.BlockSpec((1, N_GATES), const2),
        pl.BlockSpec((1, N_GATES), const2),
        pl.BlockSpec((1, C_WIDTH), const2),
        pl.BlockSpec(seg.shape, const2),
        pl.BlockSpec(expand.shape, const2),
    ]
    args = [cqkv, ab, scg, convw, alog, dtb, normg, seg, expand]
    state_spec = pl.BlockSpec((None, N_DIRS, C_HEADS, C_HEAD_DIM, C_HEAD_DIM), lambda bi: (bi, 0, 0, 0, 0))
    if has_s0:
        in_specs.append(state_spec)
        args.append(s0)
    out_specs = [pl.BlockSpec((None, t, C_WIDTH), lambda bi: (bi, 0, 0))]
    out_shape = [jax.ShapeDtypeStruct((b, t, C_WIDTH), BF16)]
    if emit_state:
        out_specs.append(state_spec)
        out_shape.append(jax.ShapeDtypeStruct((b, N_DIRS, C_HEADS, C_HEAD_DIM, C_HEAD_DIM), F32))
    pair_w = 2 * C_HEAD_DIM
    scratch = [
        pltpu.VMEM((nch, N_DIRS, 2 * GDN_CHUNK, C_WIDTH), BF16),
        pltpu.VMEM((nch, N_DIRS, GDN_CHUNK, C_WIDTH), BF16),
        pltpu.VMEM((nch, N_DIRS, GDN_CHUNK, C_WIDTH), BF16),
        pltpu.VMEM((nch, N_DIRS, GDN_CHUNK, C_WIDTH), BF16),
        pltpu.VMEM((nch, N_DIRS, 1, C_WIDTH), F32),
        pltpu.VMEM((N_DIRS, C_HEADS // 2, pair_w, pair_w), F32),
        pltpu.VMEM((ach * GDN_CHUNK + 2 * BF16_SUBLANES, 3 * C_WIDTH), F32),
    ]
    return pl.pallas_call(
        functools.partial(_gdn_body, t=t, has_s0=has_s0, emit_state=emit_state, ach=ach),
        grid=(b,),
        in_specs=in_specs,
        out_specs=out_specs,
        out_shape=out_shape,
        scratch_shapes=scratch,
        compiler_params=pltpu.CompilerParams(dimension_semantics=("arbitrary",),
                                             vmem_limit_bytes=VMEM_LIMIT_BYTES),
        name="gdn_state" if emit_state else "gdn",
    )(*args)


def _attn_body(*refs, banded, tq, nblk_total):
    it = iter(refs)
    x_ref, q_ref, kv_ref = (next(it) for _ in range(3))
    if banded:
        kvl_ref, kvr_ref, band_ref = (next(it) for _ in range(3))
        ck_ref, cv_ref = (next(it) for _ in range(2))
    sink_ref, sag_ref, ob_ref, oc_ref, wout_ref, gpost_ref, gate_ref, y_ref = (next(it) for _ in range(8))

    nb = tq // ATTN_BLOCK
    rows_q = A_GROUP * ATTN_BLOCK
    lane = lax.broadcasted_iota(jnp.int32, (1, A_KV_WIDTH), 1)
    head_mask = [lane < HEAD_DIM, lane >= HEAD_DIM]

    def split_heads(a):
        zero = jnp.zeros_like(a)
        return [jnp.where(head_mask[h], a, zero) for h in range(A_KV_HEADS)]

    if banded:
        kv_ext = jnp.concatenate([kvl_ref[...], kv_ref[...], kvr_ref[...]], axis=0)
        k_own = split_heads(kv_ext[:, 0:A_KV_WIDTH])
        v_own = split_heads(kv_ext[:, A_KV_WIDTH:2 * A_KV_WIDTH])
        k_ctx = split_heads(ck_ref[...].astype(BF16))
        v_ctx = split_heads(cv_ref[...].astype(BF16))
        band_l = band_ref[:, 0:ATTN_BLOCK]
        band_r = band_ref[:, 2 * ATTN_BLOCK:3 * ATTN_BLOCK]
        first_blk = pl.program_id(1) * nb
    else:
        kv_all = kv_ref[...]
        k_ctx = split_heads(kv_all[:, 0:A_KV_WIDTH])
        v_ctx = split_heads(kv_all[:, A_KV_WIDTH:2 * A_KV_WIDTH])

    units = [(j, h) for j in range(nb) for h in range(A_KV_HEADS)]
    q2 = [jnp.concatenate([q_ref[j * ATTN_BLOCK:(j + 1) * ATTN_BLOCK, g * A_KV_WIDTH:(g + 1) * A_KV_WIDTH]
                           for g in range(A_GROUP)], axis=0) for j in range(nb)]

    def scores(j, h):
        s_ctx = _dot_nt(q2[j], k_ctx[h])
        if not banded:
            return s_ctx, None
        s_own = _dot_nt(q2[j], k_own[h][j * ATTN_BLOCK:(j + 3) * ATTN_BLOCK])
        blk = first_blk + j
        thr_l = jnp.where(blk > 0, 0.5, 2.0)
        thr_r = jnp.where(blk < nblk_total - 1, 0.5, 2.0)
        left = jnp.where(band_l > thr_l, s_own[:, 0:ATTN_BLOCK], NEG_INF)
        right = jnp.where(band_r > thr_r, s_own[:, 2 * ATTN_BLOCK:3 * ATTN_BLOCK], NEG_INF)
        return s_ctx, jnp.concatenate([left, s_own[:, ATTN_BLOCK:2 * ATTN_BLOCK], right], axis=1)

    o_acc = [jnp.zeros((rows_q, A_KV_WIDTH), F32) for _ in range(nb)]
    pending = scores(*units[0])
    for ui, (j, h) in enumerate(units):
        s_ctx, s_own = pending
        if ui + 1 < len(units):
            pending = scores(*units[ui + 1])
        sink = sink_ref[h] * LOG2_E
        s_all = jnp.concatenate([s_ctx, s_own], axis=1) if banded else s_ctx
        m = jnp.maximum(jnp.max(s_all, axis=-1, keepdims=True), sink)
        p = jnp.exp2(s_all - m)
        denom = jnp.sum(p, axis=-1, keepdims=True) + jnp.exp2(sink - m)
        pb = p.astype(BF16)
        n_ctx = s_ctx.shape[1]
        acc = _dot(pb[:, 0:n_ctx], v_ctx[h])
        if banded:
            acc = acc + _dot(pb[:, n_ctx:], v_own[h][j * ATTN_BLOCK:(j + 3) * ATTN_BLOCK])
        o_acc[j] = o_acc[j] + acc * (1.0 / denom)
    o_blocks = [jnp.concatenate([o_acc[j][g * ATTN_BLOCK:(g + 1) * ATTN_BLOCK] for g in range(A_GROUP)], axis=1)
                for j in range(nb)]
    o_a = jnp.concatenate(o_blocks, axis=0) if nb > 1 else o_blocks[0]

    mix_a = (o_a * sag_ref[...].astype(F32)).astype(BF16)
    mix = (_dot(mix_a, wout_ref[0:A_WIDTH, :])
           + _dot(ob_ref[...], wout_ref[A_WIDTH:A_WIDTH + B_WIDTH, :])
           + _dot(oc_ref[...], wout_ref[A_WIDTH + B_WIDTH:A_WIDTH + B_WIDTH + C_WIDTH, :]))
    ms = jnp.mean(mix * mix, axis=-1, keepdims=True)
    y_ref[...] = x_ref[...] + gate_ref[...] * (mix * lax.rsqrt(ms + EPS) * gpost_ref[...])


def _attention_out(x, q, kv, sag, ob, oc, sink_rows, wout, gpost, gate, band, ctx_k, ctx_v, layer, *, tq):
    b, t, d = x.shape
    banded = band is not None
    nblk_total = t // ATTN_BLOCK
    per_q = tq // ATTN_BLOCK
    ngate = gate.shape[0]
    gate_map = (lambda bi, i: (bi, 0, 0)) if ngate > 1 else (lambda bi, i: (0, 0, 0))
    const2 = lambda bi, i: (0, 0)

    def tok(width):
        return pl.BlockSpec((None, tq, width), lambda bi, i: (bi, i, 0))

    in_specs = [tok(d), tok(A_WIDTH)]
    args = [x, q]
    if banded:
        in_specs.append(tok(2 * A_KV_WIDTH))
        args.append(kv)
        in_specs.append(pl.BlockSpec((None, ATTN_BLOCK, 2 * A_KV_WIDTH),
                                     lambda bi, i: (bi, jnp.maximum(i * per_q - 1, 0), 0)))
        in_specs.append(pl.BlockSpec((None, ATTN_BLOCK, 2 * A_KV_WIDTH),
                                     lambda bi, i: (bi, jnp.minimum((i + 1) * per_q, nblk_total - 1), 0)))
        in_specs.append(pl.BlockSpec(band.shape, const2))
        args += [kv, kv, band]
        past = ctx_k.shape[2]
        in_specs += [pl.BlockSpec((None, None, past, A_KV_WIDTH), lambda bi, i: (bi, layer, 0, 0))] * 2
        args += [ctx_k, ctx_v]
    else:
        in_specs.append(pl.BlockSpec((None, t, 2 * A_KV_WIDTH), lambda bi, i: (bi, 0, 0)))
        args.append(kv)
    in_specs += [
        pl.BlockSpec(sink_rows.shape, lambda bi, i: (0, 0, 0)),
        tok(A_WIDTH), tok(B_WIDTH), tok(C_WIDTH),
        pl.BlockSpec(wout.shape, const2),
        pl.BlockSpec((1, d), const2),
        pl.BlockSpec((None, 1, d), gate_map),
    ]
    args += [sink_rows, sag, ob, oc, wout, gpost, gate]
    return pl.pallas_call(
        functools.partial(_attn_body, banded=banded, tq=tq, nblk_total=nblk_total),
        grid=(b, t // tq),
        in_specs=in_specs,
        out_specs=tok(d),
        out_shape=jax.ShapeDtypeStruct((b, t, d), F32),
        compiler_params=pltpu.CompilerParams(dimension_semantics=("arbitrary", "arbitrary"),
                                             vmem_limit_bytes=VMEM_LIMIT_BYTES),
        name="attention_banded" if banded else "attention_full",
    )(*args)


def _block_ones(width, block):
    idx = np.arange(width) // block
    return jnp.asarray((idx[:, None] == idx[None, :]).astype(np.float32), dtype=BF16)


def _gate_expansion():
    lane_gate = np.arange(N_GATES * C_HEAD_DIM) // C_HEAD_DIM
    return jnp.asarray((np.arange(N_GATES)[:, None] == lane_gate[None, :]).astype(np.float32), dtype=BF16)


def _rope_tables(n):
    half = HEAD_DIM // 2
    nf = half // 2
    pos = np.arange(n)
    row = (pos // GRID_W).astype(np.float32)
    colp = (pos % GRID_W).astype(np.float32)
    inv_freq = jnp.asarray(ROPE_BASE, F32) ** (-jnp.arange(nf, dtype=F32) / nf)
    lane = np.arange(LANES) % HEAD_DIM
    freq_idx = lane % nf
    quarter = lane // nf
    p = jnp.where(jnp.asarray(quarter < 2)[None, :], jnp.asarray(row)[:, None], jnp.asarray(colp)[:, None])
    ang = p * inv_freq[freq_idx][None, :]
    cos, sin = jnp.cos(ang), jnp.sin(ang)
    first = jnp.asarray(quarter % 2 == 0)[None, :]
    sa = jnp.where(first, -sin, 0.0)
    sb = jnp.where(first, 0.0, sin)
    return cos.astype(F32), sa.astype(F32), sb.astype(F32)


def _band_table():
    q_rel = np.arange(ATTN_BLOCK)
    k_rel = np.arange(3 * ATTN_BLOCK) - ATTN_BLOCK
    band = (np.abs(q_rel[:, None] - k_rel[None, :]) <= ATTN_BLOCK).astype(np.float32)
    return jnp.asarray(np.tile(band, (A_GROUP, 1)))


def _head_perm():
    idx = np.arange(A_WIDTH).reshape(A_KV_HEADS, A_GROUP, HEAD_DIM)
    return np.transpose(idx, (1, 0, 2)).reshape(-1)


def _layout_w_in(w_in_l):
    perm = _head_perm()
    o = 0
    sec = {}
    for name, width in (("aq", A_WIDTH), ("ak", A_KV_WIDTH), ("av", A_KV_WIDTH), ("ag", A_WIDTH),
                        ("bu", B_WIDTH), ("bv", B_WIDTH), ("bg", B_WIDTH),
                        ("cq", C_WIDTH), ("ck", C_WIDTH), ("cv", C_WIDTH), ("ca", N_GATES), ("cb", N_GATES),
                        ("cg", C_WIDTH)):
        sec[name] = w_in_l[:, o:o + width]
        o += width
    d = w_in_l.shape[0]
    pad = jnp.zeros((d, LANES - 2 * N_GATES), w_in_l.dtype)
    cols = [sec["aq"][:, perm], sec["ak"], sec["av"], sec["ag"][:, perm], sec["bu"], sec["bv"], sec["bg"],
            sec["cq"], sec["ck"], sec["cv"], sec["cg"], sec["ca"], sec["cb"], pad]
    return jnp.concatenate(cols, axis=1).astype(BF16)


def _row_tile(t, target):
    tile = min(t, target)
    while t % tile:
        tile -= ATTN_BLOCK
    return tile


def kernel(x_prompt, x_sample, cache_k, cache_v, state_delta, c, c_ctx, w_mod, b_mod, g_pre, w_in, attn_sink,
           sgu_ln_g, sgu_ln_b, sgu_w, sgu_b, gdn_conv_w, gdn_a_log, gdn_dt_bias, gdn_norm_g, g_post, w_out):
    depth, d, _ = w_mod.shape
    b_ctx, t_ctx, _ = x_prompt.shape
    b_lat, t_lat, _ = x_sample.shape
    past = cache_k.shape[2]
    assert t_ctx % SGU_CHUNK == 0 and t_lat % SGU_CHUNK == 0 and t_lat % GRID_W == 0
    assert t_ctx // GDN_CHUNK >= 2 and t_lat // GDN_CHUNK >= 2
    assert w_in.shape[2] == 2 * A_WIDTH + 2 * A_KV_WIDTH + 3 * B_WIDTH + 4 * C_WIDTH + 2 * N_GATES

    n_cond = b_lat + 1
    rows = -(-n_cond // 8) * 8
    conds = jnp.concatenate([c, c_ctx[None, :], jnp.zeros((rows - n_cond, d), c.dtype)], axis=0)
    mod = _modulation(conds, w_mod, b_mod).reshape(depth, rows, 3, d)

    seg_b = _block_ones(B_WIDTH, B_GROUP_DIM)
    seg_c = _block_ones(C_WIDTH, C_HEAD_DIM)
    expand_c = _gate_expansion()
    rope_tabs = _rope_tables(t_lat)
    band = _band_table()
    perm = _head_perm()
    ck = cache_k.reshape(b_lat, depth, past, A_KV_WIDTH)
    cv = cache_v.reshape(b_lat, depth, past, A_KV_WIDTH)
    tm_ctx, tm_lat = _row_tile(t_ctx, 512), _row_tile(t_lat, 512)

    xp, xs = x_prompt, x_sample
    new_k, new_v, new_s = [], [], []
    for l in range(depth):
        w1 = _layout_w_in(w_in[l])
        wout = jnp.concatenate([w_out[l][0:A_WIDTH][perm], w_out[l][A_WIDTH:]], axis=0).astype(BF16)
        gpre = g_pre[l].reshape(1, d)
        gpost = g_post[l].reshape(1, d)
        lng = sgu_ln_g[l].reshape(1, B_WIDTH)
        lnb = sgu_ln_b[l].reshape(1, B_WIDTH)
        sw = sgu_w[l].astype(BF16)
        sbias = jnp.repeat(jnp.swapaxes(sgu_b[l], 0, 1), B_GROUP_DIM, axis=1)
        sink = attn_sink[l].reshape(A_KV_HEADS, A_GROUP)
        sink_rows = jnp.repeat(sink, ATTN_BLOCK, axis=1)[:, :, None]
        alog = gdn_a_log[l].reshape(1, N_GATES)
        dtb = gdn_dt_bias[l].reshape(1, N_GATES)
        normg = jnp.tile(gdn_norm_g[l], C_HEADS).reshape(1, C_WIDTH)
        convw = gdn_conv_w[l]
        mod_lat = mod[l, 0:b_lat]
        mod_ctx = mod[l, b_lat:b_lat + 1]

        q, kv, sag, ob, cqkv, scg, ab, kf, vf = _projection(
            xp, mod_ctx, gpre, w1, seg_b, lng, lnb, sw, sbias, None, keep_f32_kv=True, tm=tm_ctx)
        oc, s_fin = _gdn(cqkv, ab, scg, convw, alog, dtb, normg, seg_c, expand_c, None, emit_state=True,
                         ach=min(GDN_A_CHUNKS, t_ctx // GDN_CHUNK))
        xp = _attention_out(xp, q, kv, sag, ob, oc, sink_rows, wout, gpost, mod_ctx[:, 2:3], None, None, None, l,
                            tq=t_ctx)
        new_k.append(kf.reshape(b_ctx, t_ctx, A_KV_HEADS, HEAD_DIM))
        new_v.append(vf.reshape(b_ctx, t_ctx, A_KV_HEADS, HEAD_DIM))
        new_s.append(s_fin)

        q, kv, sag, ob, cqkv, scg, ab = _projection(
            xs, mod_lat, gpre, w1, seg_b, lng, lnb, sw, sbias, rope_tabs, keep_f32_kv=False, tm=tm_lat)
        (oc,) = _gdn(cqkv, ab, scg, convw, alog, dtb, normg, seg_c, expand_c, state_delta[:, l], emit_state=False,
                     ach=min(GDN_A_CHUNKS, t_lat // GDN_CHUNK))
        xs = _attention_out(xs, q, kv, sag, ob, oc, sink_rows, wout, gpost, mod_lat[:, 2:3], band, ck, cv, l,
                            tq=_row_tile(t_lat, 512))

    return (xp, xs, jnp.stack(new_k, axis=1), jnp.stack(new_v, axis=1), jnp.stack(new_s, axis=1))
```

```python
import functools

import numpy as np
import jax
import jax.numpy as jnp
from jax import lax
from jax.experimental import pallas as pl
from jax.experimental.pallas import tpu as pltpu

F32 = jnp.float32
BF16 = jnp.bfloat16

GRID_W = 64
HEAD_DIM = 64
A_HEADS = 8
A_KV_HEADS = 2
A_GROUP = A_HEADS // A_KV_HEADS
A_WIDTH = A_HEADS * HEAD_DIM
A_KV_WIDTH = A_KV_HEADS * HEAD_DIM
ATTN_BLOCK = 128
ROPE_BASE = 10000.0
B_GROUPS = 4
B_GROUP_DIM = 64
B_WIDTH = B_GROUPS * B_GROUP_DIM
SGU_CHUNK = 128
C_HEADS = 4
C_HEAD_DIM = 64
C_WIDTH = C_HEADS * C_HEAD_DIM
CONV_K = 5
GDN_CHUNK = 64
N_DIRS = 2
N_GATES = N_DIRS * C_HEADS
EPS = 1e-6
NEG_INF = -1e30
LOG2_E = 1.4426950408889634

LANES = 128
BF16_SUBLANES = 16
VMEM_LIMIT_BYTES = 56 * 1024 * 1024

GDN_A_CHUNKS = 4

PROJ_ROWS_CTX = 512
PROJ_ROWS_LAT = 1024
ATTN_ROWS = 512

_OFF_Q = 0
_OFF_K = _OFF_Q + A_WIDTH
_OFF_V = _OFF_K + A_KV_WIDTH
_OFF_AG = _OFF_V + A_KV_WIDTH
_OFF_BU = _OFF_AG + A_WIDTH
_OFF_BV = _OFF_BU + B_WIDTH
_OFF_BG = _OFF_BV + B_WIDTH
_OFF_CQKV = _OFF_BG + B_WIDTH
_OFF_CG = _OFF_CQKV + 3 * C_WIDTH
_OFF_AB = _OFF_CG + C_WIDTH
_W1_COLS = _OFF_AB + LANES

def _dot(a, b):
    return jnp.dot(a, b, preferred_element_type=F32)


def _layer_spec(arr, layer):
    zeros = (0,) * (arr.ndim - 1)
    return pl.BlockSpec((None,) + tuple(arr.shape[1:]), lambda *_: (layer,) + zeros)


def _mod_spec(mod, layer, row0, per_batch):
    return pl.BlockSpec((None, None) + tuple(mod.shape[2:]),
                        lambda bi, i: (layer, row0 + (bi if per_batch else 0), 0, 0))


def _dot_nt(a, b):
    return lax.dot_general(a, b, (((1,), (1,)), ((), ())), preferred_element_type=F32)


def _dot_tn(a, b):
    return lax.dot_general(a, b, (((0,), (0,)), ((), ())), preferred_element_type=F32)


def _split3(x):
    hi = x.astype(BF16)
    r1 = x - hi.astype(F32)
    mid = r1.astype(BF16)
    lo = (r1 - mid.astype(F32)).astype(BF16)
    return hi, mid, lo


def _seg_sum(x, seg, terms=3):
    hi, mid, lo = _split3(x)
    out = _dot(hi, seg) + _dot(mid, seg)
    return out + _dot(lo, seg) if terms == 3 else out


def _tri_sum(tri, x):
    hi, mid, lo = _split3(x)
    return _dot(tri, hi) + _dot(tri, mid) + _dot(tri, lo)


def _sigmoid(x):
    return 1.0 / (1.0 + jnp.exp(-x))


def _silu(x):
    return x * _sigmoid(x)


def _softplus(x):
    return jnp.maximum(x, 0.0) + jnp.log1p(jnp.exp(-jnp.abs(x)))


def _mod_body(cond_ref, w_ref, b_ref, o_ref):
    a = _silu(cond_ref[...]).astype(BF16)
    o_ref[...] = _dot(a, w_ref[...].astype(BF16)) + b_ref[...]


def _modulation(conds, w_mod, b_mod):
    depth, d, d3 = w_mod.shape
    rows = conds.shape[0]
    tn = d
    return pl.pallas_call(
        _mod_body,
        grid=(depth, d3 // tn),
        in_specs=[
            pl.BlockSpec((rows, d), lambda l, j: (0, 0)),
            pl.BlockSpec((None, d, tn), lambda l, j: (l, 0, j)),
            pl.BlockSpec((None, 1, tn), lambda l, j: (l, 0, j)),
        ],
        out_specs=pl.BlockSpec((None, rows, tn), lambda l, j: (l, 0, j)),
        out_shape=jax.ShapeDtypeStruct((depth, rows, d3), F32),
        compiler_params=pltpu.CompilerParams(dimension_semantics=("arbitrary", "arbitrary"),
                                             vmem_limit_bytes=VMEM_LIMIT_BYTES),
        name="modulation",
    )(conds, w_mod, b_mod.reshape(depth, 1, d3))


def _rope(z, c, sa, sb):
    w = z.shape[1]
    reps = w // LANES
    if reps > 1:
        c = jnp.concatenate([c] * reps, axis=1)
        sa = jnp.concatenate([sa] * reps, axis=1)
        sb = jnp.concatenate([sb] * reps, axis=1)
    half_pair = HEAD_DIM // 4
    return z * c + pltpu.roll(z, w - half_pair, 1) * sa + pltpu.roll(z, half_pair, 1) * sb


def _proj_body(*refs, rope, keep_f32_kv, tm):
    it = iter(refs)
    x_ref, mod_ref, gpre_ref, w_ref, seg_ref, lng_ref, lnb_ref, sw_ref, sbias_ref = (next(it) for _ in range(9))
    if rope:
        rc_ref, rsa_ref, rsb_ref = (next(it) for _ in range(3))
    q_ref, kv_ref, sag_ref, ob_ref, cqkv_ref, scg_ref, ab_ref = (next(it) for _ in range(7))
    if keep_f32_kv:
        kf_ref, vf_ref = (next(it) for _ in range(2))

    x = x_ref[...]
    ms = jnp.mean(x * x, axis=-1, keepdims=True)
    h = x * lax.rsqrt(ms + EPS) * gpre_ref[...]
    h = h * (1.0 + mod_ref[1:2, :]) + mod_ref[0:1, :]
    hb = h.astype(BF16)

    def proj(lo, width):
        return _dot(hb, w_ref[:, lo:lo + width])

    zq = proj(_OFF_Q, A_WIDTH)
    zk = proj(_OFF_K, A_KV_WIDTH)
    zv = proj(_OFF_V, A_KV_WIDTH)
    if keep_f32_kv:
        kf_ref[...] = zk
        vf_ref[...] = zv
    if rope:
        c, sa, sb = rc_ref[...], rsa_ref[...], rsb_ref[...]
        zq = _rope(zq, c, sa, sb)
        zk = _rope(zk, c, sa, sb)
    q_ref[...] = (zq * (HEAD_DIM ** -0.5 * LOG2_E)).astype(BF16)
    kv_ref[:, 0:A_KV_WIDTH] = zk.astype(BF16)
    kv_ref[:, A_KV_WIDTH:2 * A_KV_WIDTH] = zv.astype(BF16)
    sag_ref[...] = _silu(proj(_OFF_AG, A_WIDTH)).astype(BF16)

    zu = proj(_OFF_BU, B_WIDTH)
    zbv = proj(_OFF_BV, B_WIDTH)
    zbg = proj(_OFF_BG, B_WIDTH)
    seg = seg_ref[...]
    inv_n = 1.0 / B_GROUP_DIM
    mean = _seg_sum(zbv, seg) * inv_n
    xc = zbv - mean
    var = _seg_sum(xc * xc, seg) * inv_n
    vg = (xc * lax.rsqrt(var + EPS) * lng_ref[...] + lnb_ref[...]).astype(BF16)
    sbias = sbias_ref[...]
    mixed_chunks = []
    for ci in range(tm // SGU_CHUNK):
        rows = slice(ci * SGU_CHUNK, (ci + 1) * SGU_CHUNK)
        per_group = [_dot(sw_ref[g], vg[rows, g * B_GROUP_DIM:(g + 1) * B_GROUP_DIM]) for g in range(B_GROUPS)]
        mixed_chunks.append(jnp.concatenate(per_group, axis=1) + sbias)
    mixed = jnp.concatenate(mixed_chunks, axis=0) if len(mixed_chunks) > 1 else mixed_chunks[0]
    ob_ref[...] = (zu * mixed * _silu(zbg)).astype(BF16)

    cqkv_ref[...] = proj(_OFF_CQKV, 3 * C_WIDTH).astype(BF16)
    scg_ref[...] = _silu(proj(_OFF_CG, C_WIDTH)).astype(BF16)
    ab_ref[...] = proj(_OFF_AB, LANES)[:, 0:2 * N_GATES]


def _projection(x, mod, gpre, w1, seg, lng, lnb, sw, sbias, rope_tabs, layer, *, mod_row0, per_batch, keep_f32_kv,
                tm):
    b, t, d = x.shape
    rope = rope_tabs is not None
    const2 = lambda bi, i: (0, 0)
    in_specs = [
        pl.BlockSpec((None, tm, d), lambda bi, i: (bi, i, 0)),
        _mod_spec(mod, layer, mod_row0, per_batch),
        _layer_spec(gpre, layer),
        _layer_spec(w1, layer),
        pl.BlockSpec(seg.shape, const2),
        _layer_spec(lng, layer),
        _layer_spec(lnb, layer),
        _layer_spec(sw, layer),
        _layer_spec(sbias, layer),
    ]
    args = [x, mod, gpre, w1, seg, lng, lnb, sw, sbias]
    if rope:
        in_specs += [pl.BlockSpec((tm, LANES), lambda bi, i: (i, 0))] * 3
        args += list(rope_tabs)

    def tok(width, dtype):
        return (pl.BlockSpec((None, tm, width), lambda bi, i: (bi, i, 0)),
                jax.ShapeDtypeStruct((b, t, width), dtype))

    outs = [tok(A_WIDTH, BF16), tok(2 * A_KV_WIDTH, BF16), tok(A_WIDTH, BF16), tok(B_WIDTH, BF16),
            tok(3 * C_WIDTH, BF16), tok(C_WIDTH, BF16), tok(2 * N_GATES, F32)]
    if keep_f32_kv:
        outs += [tok(A_KV_WIDTH, F32), tok(A_KV_WIDTH, F32)]
    return pl.pallas_call(
        functools.partial(_proj_body, rope=rope, keep_f32_kv=keep_f32_kv, tm=tm),
        grid=(b, t // tm),
        in_specs=in_specs,
        out_specs=[o[0] for o in outs],
        out_shape=[o[1] for o in outs],
        compiler_params=pltpu.CompilerParams(dimension_semantics=("arbitrary", "arbitrary"),
                                             vmem_limit_bytes=VMEM_LIMIT_BYTES),
        name="projection_rope" if rope else "projection",
    )(*args)


def _tri_masks(n, reps):
    rows = lax.broadcasted_iota(jnp.int32, (n, reps * n), 0)
    cols = lax.broadcasted_iota(jnp.int32, (n, reps * n), 1) & (n - 1)
    masks = []
    level = 0
    while (1 << level) < n:
        same_pair = (rows >> (level + 1)) == (cols >> (level + 1))
        other_half = (rows >> level) != (cols >> level)
        masks.append(same_pair & other_half)
        level += 1
    return rows, cols, rows == cols, masks


def _gdn_body(*refs, t, has_s0, emit_state, ach):
    it = iter(refs)
    cqkv_ref, ab_ref, scg_ref, convw_ref, alog_ref, dtb_ref, normg_ref, seg_ref, expand_ref = (
        next(it) for _ in range(9))
    s0_ref = next(it) if has_s0 else None
    oc_ref = next(it)
    sfin_ref = next(it) if emit_state else None
    wq_scr, u_scr, aqk_scr, kd_scr, egl_scr, s_scr, win_scr = (next(it) for _ in range(7))

    c = GDN_CHUNK
    hd = C_HEAD_DIM
    pw = 2 * hd
    n_pairs = C_HEADS // 2
    nch = t // c
    span = ach * c
    halo = BF16_SUBLANES
    rows, cols, _, _ = _tri_masks(c, 1)
    tri_l = jnp.where(rows >= cols, 1.0, 0.0).astype(BF16)
    tri_u = jnp.where(rows <= cols, 1.0, 0.0).astype(BF16)
    prow_, pcol_, eye_pw, level_pw = _tri_masks(c, 2)
    incl_pw = (prow_ >= pcol_, prow_ <= pcol_)
    strict_pw = (prow_ > pcol_, prow_ < pcol_)
    gate_lane = lax.broadcasted_iota(jnp.int32, (1, N_GATES), 1)
    is_fwd = gate_lane < C_HEADS
    seg = seg_ref[...]
    convw = convw_ref[...]
    neg_rate = -jnp.exp(alog_ref[...])
    dtb = dtb_ref[...]

    lane_pw = lax.broadcasted_iota(jnp.int32, (1, pw), 1)
    first_lanes = lane_pw < hd

    def block_diag(xb):
        zero = jnp.zeros_like(xb)
        return jnp.concatenate([jnp.where(first_lanes, xb, zero), jnp.where(first_lanes, zero, xb)], axis=0)

    expand = expand_ref[...]

    def a_step(j, first, last):
        zeros_halo = jnp.zeros((halo, 3 * C_WIDTH), F32)
        if first:
            r0 = 0
            win_scr[0:halo, :] = zeros_halo
            n_in = span + (0 if last else halo)
            win_scr[halo:halo + n_in, :] = cqkv_ref[0:n_in, :].astype(F32)
            if last:
                win_scr[halo + span:2 * halo + span, :] = zeros_halo
        elif last:
            r0 = t - span
            win_scr[0:halo + span, :] = cqkv_ref[t - span - halo:t, :].astype(F32)
            win_scr[halo + span:2 * halo + span, :] = zeros_halo
        else:
            r0 = pl.multiple_of(j * span, span)
            win_scr[...] = cqkv_ref[pl.ds(r0 - halo, span + 2 * halo), :].astype(F32)
        y = win_scr[halo:halo + span, :] * convw[CONV_K // 2:CONV_K // 2 + 1, :]
        for tap in range(CONV_K):
            if tap != CONV_K // 2:
                lo = halo - CONV_K // 2 + tap
                y = y + win_scr[lo:lo + span, :] * convw[tap:tap + 1, :]
        y = _silu(y)
        q_all = y[:, 0:C_WIDTH]
        k_all = y[:, C_WIDTH:2 * C_WIDTH]
        v_all = y[:, 2 * C_WIDTH:3 * C_WIDTH]
        q_all = q_all * lax.rsqrt(_seg_sum(q_all * q_all, seg, terms=2) + EPS) * (hd ** -0.5)
        k_all = k_all * lax.rsqrt(_seg_sum(k_all * k_all, seg, terms=2) + EPS)

        ab = ab_ref[pl.ds(r0, span), :]
        g_all = neg_rate * _softplus(ab[:, 0:N_GATES] + dtb)
        beta_all = _sigmoid(ab[:, N_GATES:2 * N_GATES])

        gc_x, beta_x, e_x, begc_x, eglgc_x, egl_x, grow = [], [], [], [], [], [], []
        for ci in range(ach):
            rs = slice(ci * c, (ci + 1) * c)
            gc_f = _tri_sum(tri_l, g_all[rs])
            gc_b = _tri_sum(tri_u, g_all[rs])
            gcc = jnp.where(is_fwd, gc_f, gc_b)
            gl = jnp.where(is_fwd, gc_f[c - 1:c, :], gc_b[0:1, :])
            grow.append(gcc.T)
            stacked = jnp.concatenate([gcc, jnp.broadcast_to(gl, (8, N_GATES)), beta_all[rs]], axis=0)
            hi, mid, lo = _split3(stacked)
            ex = _dot(hi, expand) + _dot(mid, expand) + _dot(lo, expand)
            gx = ex[0:c, :]
            glx = ex[c:c + 1, :]
            bx = ex[c + 8:2 * c + 8, :]
            ee = jnp.exp(gx)
            gc_x.append(gx)
            beta_x.append(bx)
            e_x.append(ee)
            begc_x.append(bx * ee)
            eglgc_x.append(jnp.exp(glx - gx))
            egl_x.append(jnp.exp(glx))

        cps = [(ci, p) for ci in range(ach) for p in range(n_pairs)]
        chains = [(ci, d, p) for ci in range(ach) for d in range(N_DIRS) for p in range(n_pairs)]

        def pslice(arr, ci, p):
            return arr[ci * c:(ci + 1) * c, p * pw:(p + 1) * pw]

        def xs(vals, ci, d, p):
            lo = d * C_WIDTH + p * pw
            return vals[ci][:, lo:lo + pw]

        kp16 = {key: pslice(k_all, *key).astype(BF16) for key in cps}
        k_bd = {key: block_diag(kp16[key]) for key in cps}
        kk = {key: _dot_nt(kp16[key], k_bd[key]) for key in cps}
        qk = {key: _dot_nt(pslice(q_all, *key).astype(BF16), k_bd[key]) for key in cps}

        decay, a_mat = {}, {}
        for ci, d, p in chains:
            g0 = d * C_HEADS + 2 * p
            grow_pair = jnp.concatenate([grow[ci][g0:g0 + 1, :], grow[ci][g0 + 1:g0 + 2, :]], axis=1)
            dec = jnp.where(incl_pw[d], jnp.exp(xs(gc_x, ci, d, p) - grow_pair), 0.0)
            decay[ci, d, p] = dec
            a_mat[ci, d, p] = jnp.where(strict_pw[d], kk[ci, p] * xs(beta_x, ci, d, p) * dec, 0.0)

        t_mat = {ch: jnp.where(eye_pw, 1.0, 0.0) - jnp.where(level_pw[0], a_mat[ch], 0.0) for ch in chains}
        for mask in level_pw[1:]:
            tb = {ch: t_mat[ch].astype(BF16) for ch in chains}
            t_bd = {ch: block_diag(tb[ch]) for ch in chains}
            et = {ch: _dot(jnp.where(mask, a_mat[ch], 0.0).astype(BF16), t_bd[ch]).astype(BF16) for ch in chains}
            t_mat = {ch: t_mat[ch] - _dot(tb[ch], block_diag(et[ch])) for ch in chains}

        uw = {}
        for ci, d, p in chains:
            vb = (pslice(v_all, ci, p) * xs(beta_x, ci, d, p)).astype(BF16)
            kbe = (pslice(k_all, ci, p) * xs(begc_x, ci, d, p)).astype(BF16)
            rhs = jnp.concatenate([block_diag(vb), block_diag(kbe)], axis=1)
            uw[ci, d, p] = _dot(t_mat[ci, d, p].astype(BF16), rhs)

        for ci in range(ach):
            m = (r0 // c) + ci
            q_c = q_all[ci * c:(ci + 1) * c, :]
            k_c = k_all[ci * c:(ci + 1) * c, :]
            for d in range(N_DIRS):
                dl = slice(d * C_WIDTH, (d + 1) * C_WIDTH)
                wq_scr[m, d, c:2 * c, :] = (q_c * e_x[ci][:, dl]).astype(BF16)
                kd_scr[m, d] = (k_c * eglgc_x[ci][:, dl]).astype(BF16)
                egl_scr[m, d] = egl_x[ci][:, dl]
                for p in range(n_pairs):
                    pl_ = slice(p * pw, (p + 1) * pw)
                    u_scr[m, d, :, pl_] = uw[ci, d, p][:, 0:pw].astype(BF16)
                    wq_scr[m, d, 0:c, pl_] = uw[ci, d, p][:, pw:2 * pw].astype(BF16)
                    aqk_scr[m, d, :, pl_] = (qk[ci, p] * decay[ci, d, p]).astype(BF16)

    n_steps = nch // ach
    if n_steps == 1:
        a_step(0, True, True)
    else:
        a_step(0, True, False)
        if n_steps > 2:
            def a_loop(j, carry):
                a_step(j, False, False)
                return carry
            lax.fori_loop(1, n_steps - 1, a_loop, 0)
        a_step(n_steps - 1, False, True)

    zeros_h = jnp.zeros((hd, hd), F32)
    for d in range(N_DIRS):
        for p in range(n_pairs):
            if has_s0:
                top = jnp.concatenate([s0_ref[d, 2 * p], zeros_h], axis=1)
                bot = jnp.concatenate([zeros_h, s0_ref[d, 2 * p + 1]], axis=1)
                s_scr[d, p] = jnp.concatenate([top, bot], axis=0)
            else:
                s_scr[d, p] = jnp.zeros((pw, pw), F32)

    prow = lax.broadcasted_iota(jnp.int32, (pw, pw), 0)
    pcol = lax.broadcasted_iota(jnp.int32, (pw, pw), 1)
    same_head = (prow >= hd) == (pcol >= hd)
    first_head_lanes = lax.broadcasted_iota(jnp.int32, (1, pw), 1) < hd
    pairs = [(d, p) for d in range(N_DIRS) for p in range(n_pairs)]

    def b_step(n, carry):
        ms = (n, nch - 1 - n)
        ls = {(d, p): slice(p * pw, (p + 1) * pw) for d, p in pairs}
        state = {(d, p): s_scr[d, p] for d, p in pairs}
        r = {(d, p): _dot(wq_scr[ms[d], d, :, ls[d, p]], state[d, p].astype(BF16)) for d, p in pairs}
        vnb = {}
        for d, p in pairs:
            v_new = u_scr[ms[d], d, :, ls[d, p]].astype(F32) - r[d, p][0:c, :]
            vnb[d, p] = v_new.astype(BF16)
        for d, p in pairs:
            zero = jnp.zeros_like(vnb[d, p])
            vn_bd = jnp.concatenate([jnp.where(first_head_lanes, vnb[d, p], zero),
                                     jnp.where(first_head_lanes, zero, vnb[d, p])], axis=0)
            o = r[d, p][c:2 * c, :] + _dot(aqk_scr[ms[d], d, :, ls[d, p]], vn_bd)
            kv = _dot_tn(kd_scr[ms[d], d, :, ls[d, p]], vnb[d, p])
            s_scr[d, p] = state[d, p] * egl_scr[ms[d], d, :, ls[d, p]] + jnp.where(same_head, kv, 0.0)
            u_scr[ms[d], d, :, ls[d, p]] = o.astype(BF16)
        return carry

    lax.fori_loop(0, nch, b_step, 0)

    inv_n = 1.0 / hd

    c_chunks = 4 if nch % 4 == 0 else 1

    def c_step(i, carry):
        m0 = i * c_chunks
        o = jnp.concatenate([u_scr[m0 + ci, 0].astype(F32) + u_scr[m0 + ci, 1].astype(F32)
                             for ci in range(c_chunks)], axis=0)
        r0 = pl.multiple_of(m0 * c, c_chunks * c)
        ms = _seg_sum(o * o, seg, terms=2) * inv_n
        on = o * lax.rsqrt(ms + EPS) * normg_ref[...]
        oc_ref[pl.ds(r0, c_chunks * c), :] = (on * scg_ref[pl.ds(r0, c_chunks * c), :].astype(F32)).astype(BF16)
        return carry

    lax.fori_loop(0, nch // c_chunks, c_step, 0)

    if emit_state:
        for d in range(N_DIRS):
            for p in range(n_pairs):
                s_pair = s_scr[d, p]
                sfin_ref[d, 2 * p] = s_pair[0:hd, 0:hd]
                sfin_ref[d, 2 * p + 1] = s_pair[hd:pw, hd:pw]


def _gdn(cqkv, ab, scg, convw, alog, dtb, normg, seg, expand, s0, layer, *, emit_state, ach):
    b, t, _ = cqkv.shape
    nch = t // GDN_CHUNK
    assert nch % ach == 0
    has_s0 = s0 is not None
    const2 = lambda bi: (0, 0)
    in_specs = [
        pl.BlockSpec((None, t, 3 * C_WIDTH), lambda bi: (bi, 0, 0)),
        pl.BlockSpec((None, t, 2 * N_GATES), lambda bi: (bi, 0, 0)),
        pl.BlockSpec((None, t, C_WIDTH), lambda bi: (bi, 0, 0)),
        _layer_spec(convw, layer),
        _layer_spec(alog, layer),
        _layer_spec(dtb, layer),
        _layer_spec(normg, layer),
        pl.BlockSpec(seg.shape, const2),
        pl.BlockSpec(expand.shape, const2),
    ]
    args = [cqkv, ab, scg, convw, alog, dtb, normg, seg, expand]
    state_spec = pl.BlockSpec((None, N_DIRS, C_HEADS, C_HEAD_DIM, C_HEAD_DIM), lambda bi: (bi, 0, 0, 0, 0))
    if has_s0:
        in_specs.append(pl.BlockSpec((None, None, N_DIRS, C_HEADS, C_HEAD_DIM, C_HEAD_DIM),
                                     lambda bi: (bi, layer, 0, 0, 0, 0)))
        args.append(s0)
    out_specs = [pl.BlockSpec((None, t, C_WIDTH), lambda bi: (bi, 0, 0))]
    out_shape = [jax.ShapeDtypeStruct((b, t, C_WIDTH), BF16)]
    if emit_state:
        out_specs.append(state_spec)
        out_shape.append(jax.ShapeDtypeStruct((b, N_DIRS, C_HEADS, C_HEAD_DIM, C_HEAD_DIM), F32))
    pair_w = 2 * C_HEAD_DIM
    scratch = [
        pltpu.VMEM((nch, N_DIRS, 2 * GDN_CHUNK, C_WIDTH), BF16),
        pltpu.VMEM((nch, N_DIRS, GDN_CHUNK, C_WIDTH), BF16),
        pltpu.VMEM((nch, N_DIRS, GDN_CHUNK, C_WIDTH), BF16),
        pltpu.VMEM((nch, N_DIRS, GDN_CHUNK, C_WIDTH), BF16),
        pltpu.VMEM((nch, N_DIRS, 1, C_WIDTH), F32),
        pltpu.VMEM((N_DIRS, C_HEADS // 2, pair_w, pair_w), F32),
        pltpu.VMEM((ach * GDN_CHUNK + 2 * BF16_SUBLANES, 3 * C_WIDTH), F32),
    ]
    return pl.pallas_call(
        functools.partial(_gdn_body, t=t, has_s0=has_s0, emit_state=emit_state, ach=ach),
        grid=(b,),
        in_specs=in_specs,
        out_specs=out_specs,
        out_shape=out_shape,
        scratch_shapes=scratch,
        compiler_params=pltpu.CompilerParams(dimension_semantics=("arbitrary",),
                                             vmem_limit_bytes=VMEM_LIMIT_BYTES),
        name="gdn_state" if emit_state else "gdn",
    )(*args)


def _attn_body(*refs, banded, nseq, tq, nblk_total):
    it = iter(refs)
    x_ref, q_ref, kv_ref = (next(it) for _ in range(3))
    if banded:
        kvl_ref, kvr_ref, band_ref = (next(it) for _ in range(3))
        ck_ref, cv_ref = (next(it) for _ in range(2))
    sink_ref, sag_ref, ob_ref, oc_ref, wout_ref, gpost_ref, mod_ref, y_ref = (next(it) for _ in range(8))

    nb = tq // ATTN_BLOCK
    rows_q = A_GROUP * ATTN_BLOCK
    lane = lax.broadcasted_iota(jnp.int32, (1, A_KV_WIDTH), 1)
    head_mask = [lane < HEAD_DIM, lane >= HEAD_DIM]

    def split_heads(a):
        zero = jnp.zeros_like(a)
        return [jnp.where(head_mask[h], a, zero) for h in range(A_KV_HEADS)]

    if banded:
        kv_ext = jnp.concatenate([kvl_ref[0], kv_ref[0], kvr_ref[0]], axis=0)
        k_own = split_heads(kv_ext[:, 0:A_KV_WIDTH])
        v_own = split_heads(kv_ext[:, A_KV_WIDTH:2 * A_KV_WIDTH])
        k_ctx = [split_heads(ck_ref[...].astype(BF16))]
        v_ctx = [split_heads(cv_ref[...].astype(BF16))]
        first_blk = pl.program_id(1) * nb
        vis_l, vis_r = [], []
        for j in range(nb):
            thr_l = jnp.where(first_blk + j > 0, 0.5, 2.0)
            thr_r = jnp.where(first_blk + j < nblk_total - 1, 0.5, 2.0)
            vis_l.append(band_ref[:, 0:ATTN_BLOCK] > thr_l)
            vis_r.append(band_ref[:, 2 * ATTN_BLOCK:3 * ATTN_BLOCK] > thr_r)
    else:
        k_ctx = [split_heads(kv_ref[s, :, 0:A_KV_WIDTH]) for s in range(nseq)]
        v_ctx = [split_heads(kv_ref[s, :, A_KV_WIDTH:2 * A_KV_WIDTH]) for s in range(nseq)]

    units = [(s, j, h) for s in range(nseq) for j in range(nb) for h in range(A_KV_HEADS)]
    q2 = {(s, j): jnp.concatenate([q_ref[s, j * ATTN_BLOCK:(j + 1) * ATTN_BLOCK, g * A_KV_WIDTH:(g + 1) * A_KV_WIDTH]
                                   for g in range(A_GROUP)], axis=0) for s in range(nseq) for j in range(nb)}

    def scores(s, j, h):
        s_ctx = _dot_nt(q2[s, j], k_ctx[s][h])
        if not banded:
            return s_ctx, None
        s_own = _dot_nt(q2[s, j], k_own[h][j * ATTN_BLOCK:(j + 3) * ATTN_BLOCK])
        left = jnp.where(vis_l[j], s_own[:, 0:ATTN_BLOCK], NEG_INF)
        right = jnp.where(vis_r[j], s_own[:, 2 * ATTN_BLOCK:3 * ATTN_BLOCK], NEG_INF)
        return s_ctx, jnp.concatenate([left, s_own[:, ATTN_BLOCK:2 * ATTN_BLOCK], right], axis=1)

    o_acc = {(s, j): jnp.zeros((rows_q, A_KV_WIDTH), F32) for s in range(nseq) for j in range(nb)}
    pending = scores(*units[0])
    for ui, (s, j, h) in enumerate(units):
        s_ctx, s_own = pending
        if ui + 1 < len(units):
            pending = scores(*units[ui + 1])
        sink = sink_ref[h] * LOG2_E
        s_all = jnp.concatenate([s_ctx, s_own], axis=1) if banded else s_ctx
        m = jnp.maximum(jnp.max(s_all, axis=-1, keepdims=True), sink)
        p = jnp.exp2(s_all - m)
        denom = jnp.sum(p, axis=-1, keepdims=True) + jnp.exp2(sink - m)
        pb = p.astype(BF16)
        n_ctx = s_ctx.shape[1]
        acc = _dot(pb[:, 0:n_ctx], v_ctx[s][h])
        if banded:
            acc = acc + _dot(pb[:, n_ctx:], v_own[h][j * ATTN_BLOCK:(j + 3) * ATTN_BLOCK])
        o_acc[s, j] = o_acc[s, j] + acc * (1.0 / denom)
    o_blocks = [jnp.concatenate([o_acc[s, j][g * ATTN_BLOCK:(g + 1) * ATTN_BLOCK] for g in range(A_GROUP)], axis=1)
                for s in range(nseq) for j in range(nb)]
    o_a = jnp.concatenate(o_blocks, axis=0) if len(o_blocks) > 1 else o_blocks[0]

    def rows_of(ref):
        return jnp.concatenate([ref[s] for s in range(nseq)], axis=0) if nseq > 1 else ref[0]

    mix_a = (o_a * rows_of(sag_ref).astype(F32)).astype(BF16)
    mix = (_dot(mix_a, wout_ref[0:A_WIDTH, :])
           + _dot(rows_of(ob_ref), wout_ref[A_WIDTH:A_WIDTH + B_WIDTH, :])
           + _dot(rows_of(oc_ref), wout_ref[A_WIDTH + B_WIDTH:A_WIDTH + B_WIDTH + C_WIDTH, :]))
    ms = jnp.mean(mix * mix, axis=-1, keepdims=True)
    y = rows_of(x_ref) + mod_ref[2:3, :] * (mix * lax.rsqrt(ms + EPS) * gpost_ref[...])
    for s in range(nseq):
        y_ref[s] = y[s * tq:(s + 1) * tq]


def _attention_out(x, q, kv, sag, ob, oc, sink_rows, wout, gpost, mod, band, ctx_k, ctx_v, layer, *, mod_row0,
                   per_batch, tq, nseq):
    b, t, d = x.shape
    banded = band is not None
    assert b % nseq == 0 and (nseq == 1 or not per_batch)
    nblk_total = t // ATTN_BLOCK
    per_q = tq // ATTN_BLOCK
    const2 = lambda bi, i: (0, 0)

    def tok(width):
        return pl.BlockSpec((nseq, tq, width), lambda bi, i: (bi, i, 0))

    in_specs = [tok(d), tok(A_WIDTH)]
    args = [x, q]
    if banded:
        in_specs.append(tok(2 * A_KV_WIDTH))
        args.append(kv)
        in_specs.append(pl.BlockSpec((1, ATTN_BLOCK, 2 * A_KV_WIDTH),
                                     lambda bi, i: (bi, jnp.maximum(i * per_q - 1, 0), 0)))
        in_specs.append(pl.BlockSpec((1, ATTN_BLOCK, 2 * A_KV_WIDTH),
                                     lambda bi, i: (bi, jnp.minimum((i + 1) * per_q, nblk_total - 1), 0)))
        in_specs.append(pl.BlockSpec(band.shape, const2))
        args += [kv, kv, band]
        past = ctx_k.shape[2]
        in_specs += [pl.BlockSpec((None, None, past, A_KV_WIDTH), lambda bi, i: (bi, layer, 0, 0))] * 2
        args += [ctx_k, ctx_v]
    else:
        in_specs.append(pl.BlockSpec((nseq, t, 2 * A_KV_WIDTH), lambda bi, i: (bi, 0, 0)))
        args.append(kv)
    in_specs += [
        _layer_spec(sink_rows, layer),
        tok(A_WIDTH), tok(B_WIDTH), tok(C_WIDTH),
        _layer_spec(wout, layer),
        _layer_spec(gpost, layer),
        _mod_spec(mod, layer, mod_row0, per_batch),
    ]
    args += [sink_rows, sag, ob, oc, wout, gpost, mod]
    return pl.pallas_call(
        functools.partial(_attn_body, banded=banded, nseq=nseq, tq=tq, nblk_total=nblk_total),
        grid=(b // nseq, t // tq),
        in_specs=in_specs,
        out_specs=tok(d),
        out_shape=jax.ShapeDtypeStruct((b, t, d), F32),
        compiler_params=pltpu.CompilerParams(dimension_semantics=("arbitrary", "arbitrary"),
                                             vmem_limit_bytes=VMEM_LIMIT_BYTES),
        name="attention_banded" if banded else "attention_full",
    )(*args)


def _block_ones(width, block):
    idx = np.arange(width) // block
    return jnp.asarray((idx[:, None] == idx[None, :]).astype(np.float32), dtype=BF16)


def _gate_expansion():
    lane_gate = np.arange(N_GATES * C_HEAD_DIM) // C_HEAD_DIM
    return jnp.asarray((np.arange(N_GATES)[:, None] == lane_gate[None, :]).astype(np.float32), dtype=BF16)


def _rope_tables(n):
    half = HEAD_DIM // 2
    nf = half // 2
    pos = np.arange(n)
    row = (pos // GRID_W).astype(np.float32)
    colp = (pos % GRID_W).astype(np.float32)
    inv_freq = np.float32(ROPE_BASE) ** (-np.arange(nf, dtype=np.float32) / np.float32(nf))
    lane = np.arange(LANES) % HEAD_DIM
    freq_idx = lane % nf
    quarter = lane // nf
    p = np.where((quarter < 2)[None, :], row[:, None], colp[:, None])
    ang = (p * inv_freq[freq_idx][None, :]).astype(np.float32)
    cos, sin = np.cos(ang), np.sin(ang)
    first = (quarter % 2 == 0)[None, :]
    sa = np.where(first, -sin, 0.0)
    sb = np.where(first, 0.0, sin)
    return tuple(jnp.asarray(a, dtype=F32) for a in (cos, sa, sb))


def _band_table():
    q_rel = np.arange(ATTN_BLOCK)
    k_rel = np.arange(3 * ATTN_BLOCK) - ATTN_BLOCK
    band = (np.abs(q_rel[:, None] - k_rel[None, :]) <= ATTN_BLOCK).astype(np.float32)
    return jnp.asarray(np.tile(band, (A_GROUP, 1)))


def _head_perm():
    idx = np.arange(A_WIDTH).reshape(A_KV_HEADS, A_GROUP, HEAD_DIM)
    return np.transpose(idx, (1, 0, 2)).reshape(-1)


def _layout_w_in(w_in):
    perm = _head_perm()
    o = 0
    sec = {}
    for name, width in (("aq", A_WIDTH), ("ak", A_KV_WIDTH), ("av", A_KV_WIDTH), ("ag", A_WIDTH),
                        ("bu", B_WIDTH), ("bv", B_WIDTH), ("bg", B_WIDTH),
                        ("cq", C_WIDTH), ("ck", C_WIDTH), ("cv", C_WIDTH), ("ca", N_GATES), ("cb", N_GATES),
                        ("cg", C_WIDTH)):
        sec[name] = w_in[:, :, o:o + width]
        o += width
    pad = jnp.zeros(w_in.shape[:2] + (LANES - 2 * N_GATES,), w_in.dtype)
    cols = [sec["aq"][:, :, perm], sec["ak"], sec["av"], sec["ag"][:, :, perm], sec["bu"], sec["bv"], sec["bg"],
            sec["cq"], sec["ck"], sec["cv"], sec["cg"], sec["ca"], sec["cb"], pad]
    return jnp.concatenate(cols, axis=2).astype(BF16)


def _row_tile(t, target):
    tile = min(t, target)
    while t % tile:
        tile -= ATTN_BLOCK
    return tile


def kernel(x_prompt, x_sample, cache_k, cache_v, state_delta, c, c_ctx, w_mod, b_mod, g_pre, w_in, attn_sink,
           sgu_ln_g, sgu_ln_b, sgu_w, sgu_b, gdn_conv_w, gdn_a_log, gdn_dt_bias, gdn_norm_g, g_post, w_out):
    depth, d, _ = w_mod.shape
    b_ctx, t_ctx, _ = x_prompt.shape
    b_lat, t_lat, _ = x_sample.shape
    past = cache_k.shape[2]
    assert t_ctx % SGU_CHUNK == 0 and t_lat % SGU_CHUNK == 0 and t_lat % GRID_W == 0
    assert t_ctx // GDN_CHUNK >= 2 and t_lat // GDN_CHUNK >= 2
    assert w_in.shape[2] == 2 * A_WIDTH + 2 * A_KV_WIDTH + 3 * B_WIDTH + 4 * C_WIDTH + 2 * N_GATES

    n_cond = b_lat + 1
    rows = -(-n_cond // 8) * 8
    conds = jnp.concatenate([c, c_ctx[None, :], jnp.zeros((rows - n_cond, d), c.dtype)], axis=0)
    mod = _modulation(conds, w_mod, b_mod).reshape(depth, rows, 3, d)

    seg_b = _block_ones(B_WIDTH, B_GROUP_DIM)
    seg_c = _block_ones(C_WIDTH, C_HEAD_DIM)
    expand_c = _gate_expansion()
    rope_tabs = _rope_tables(t_lat)
    band = _band_table()
    perm = _head_perm()
    ck = cache_k.reshape(b_lat, depth, past, A_KV_WIDTH)
    cv = cache_v.reshape(b_lat, depth, past, A_KV_WIDTH)
    tm_ctx, tm_lat = _row_tile(b_ctx * t_ctx, PROJ_ROWS_CTX), _row_tile(t_lat, PROJ_ROWS_LAT)

    w1 = _layout_w_in(w_in)
    wout = jnp.concatenate([w_out[:, 0:A_WIDTH][:, perm], w_out[:, A_WIDTH:]], axis=1).astype(BF16)
    gpre = g_pre.reshape(depth, 1, d)
    gpost = g_post.reshape(depth, 1, d)
    lng = sgu_ln_g.reshape(depth, 1, B_WIDTH)
    lnb = sgu_ln_b.reshape(depth, 1, B_WIDTH)
    sw = sgu_w.astype(BF16)
    sbias = jnp.repeat(jnp.swapaxes(sgu_b, 1, 2), B_GROUP_DIM, axis=2)
    sink_rows = jnp.repeat(attn_sink.reshape(depth, A_KV_HEADS, A_GROUP), ATTN_BLOCK, axis=2)[..., None]
    alog = gdn_a_log.reshape(depth, 1, N_GATES)
    dtb = gdn_dt_bias.reshape(depth, 1, N_GATES)
    normg = jnp.tile(gdn_norm_g, (1, C_HEADS)).reshape(depth, 1, C_WIDTH)
    ctx_nseq = max(1, min(b_ctx, ATTN_ROWS // t_ctx))

    xp, xs = x_prompt, x_sample
    new_k, new_v, new_s = [], [], []
    for l in range(depth):
        flat = _projection(xp.reshape(1, b_ctx * t_ctx, d), mod, gpre, w1, seg_b, lng, lnb, sw, sbias, None, l,
                           mod_row0=b_lat, per_batch=False, keep_f32_kv=True, tm=tm_ctx)
        q, kv, sag, ob, cqkv, scg, ab, kf, vf = (a.reshape(b_ctx, t_ctx, a.shape[-1]) for a in flat)
        oc, s_fin = _gdn(cqkv, ab, scg, gdn_conv_w, alog, dtb, normg, seg_c, expand_c, None, l, emit_state=True,
                         ach=min(GDN_A_CHUNKS, t_ctx // GDN_CHUNK))
        xp = _attention_out(xp, q, kv, sag, ob, oc, sink_rows, wout, gpost, mod, None, None, None, l,
                            mod_row0=b_lat, per_batch=False, tq=t_ctx, nseq=ctx_nseq)
        new_k.append(kf.reshape(b_ctx, t_ctx, A_KV_HEADS, HEAD_DIM))
        new_v.append(vf.reshape(b_ctx, t_ctx, A_KV_HEADS, HEAD_DIM))
        new_s.append(s_fin)

        q, kv, sag, ob, cqkv, scg, ab = _projection(
            xs, mod, gpre, w1, seg_b, lng, lnb, sw, sbias, rope_tabs, l, mod_row0=0, per_batch=True,
            keep_f32_kv=False, tm=tm_lat)
        (oc,) = _gdn(cqkv, ab, scg, gdn_conv_w, alog, dtb, normg, seg_c, expand_c, state_delta, l, emit_state=False,
                     ach=min(GDN_A_CHUNKS, t_lat // GDN_CHUNK))
        xs = _attention_out(xs, q, kv, sag, ob, oc, sink_rows, wout, gpost, mod, band, ck, cv, l,
                            mod_row0=0, per_batch=True, tq=_row_tile(t_lat, ATTN_ROWS), nseq=1)

    return (xp, xs, jnp.stack(new_k, axis=1), jnp.stack(new_v, axis=1), jnp.stack(new_s, axis=1))
```

```python
import functools

import numpy as np
import jax
import jax.numpy as jnp
from jax import lax
from jax.experimental import pallas as pl
from jax.experimental.pallas import tpu as pltpu

F32 = jnp.float32
BF16 = jnp.bfloat16

GRID_W = 64
HEAD_DIM = 64
A_HEADS = 8
A_KV_HEADS = 2
A_GROUP = A_HEADS // A_KV_HEADS
A_WIDTH = A_HEADS * HEAD_DIM
A_KV_WIDTH = A_KV_HEADS * HEAD_DIM
ATTN_BLOCK = 128
ROPE_BASE = 10000.0
B_GROUPS = 4
B_GROUP_DIM = 64
B_WIDTH = B_GROUPS * B_GROUP_DIM
SGU_CHUNK = 128
C_HEADS = 4
C_HEAD_DIM = 64
C_WIDTH = C_HEADS * C_HEAD_DIM
CONV_K = 5
GDN_CHUNK = 64
N_DIRS = 2
N_GATES = N_DIRS * C_HEADS
EPS = 1e-6
NEG_INF = -1e30
LOG2_E = 1.4426950408889634

LANES = 128
BF16_SUBLANES = 16
VMEM_LIMIT_BYTES = 56 * 1024 * 1024

GDN_A_CHUNKS = 4

PROJ_ROWS_CTX = 512
PROJ_ROWS_LAT = 1024
ATTN_ROWS_CTX = 1024
ATTN_ROWS_LAT = 512

_OFF_Q = 0
_OFF_K = _OFF_Q + A_WIDTH
_OFF_V = _OFF_K + A_KV_WIDTH
_OFF_AG = _OFF_V + A_KV_WIDTH
_OFF_BU = _OFF_AG + A_WIDTH
_OFF_BV = _OFF_BU + B_WIDTH
_OFF_BG = _OFF_BV + B_WIDTH
_OFF_CQKV = _OFF_BG + B_WIDTH
_OFF_CG = _OFF_CQKV + 3 * C_WIDTH
_OFF_AB = _OFF_CG + C_WIDTH
_W1_COLS = _OFF_AB + LANES

def _dot(a, b):
    return jnp.dot(a, b, preferred_element_type=F32)


def _layer_spec(arr, layer):
    zeros = (0,) * (arr.ndim - 1)
    return pl.BlockSpec((None,) + tuple(arr.shape[1:]), lambda *_: (layer,) + zeros)


def _mod_spec(mod, layer, row0, per_batch):
    return pl.BlockSpec((None, None) + tuple(mod.shape[2:]),
                        lambda bi, i: (layer, row0 + (bi if per_batch else 0), 0, 0))


def _dot_nt(a, b):
    return lax.dot_general(a, b, (((1,), (1,)), ((), ())), preferred_element_type=F32)


def _dot_tn(a, b):
    return lax.dot_general(a, b, (((0,), (0,)), ((), ())), preferred_element_type=F32)


def _split3(x):
    hi = x.astype(BF16)
    r1 = x - hi.astype(F32)
    mid = r1.astype(BF16)
    lo = (r1 - mid.astype(F32)).astype(BF16)
    return hi, mid, lo


def _seg_sum(x, seg, terms=3):
    hi, mid, lo = _split3(x)
    out = _dot(hi, seg) + _dot(mid, seg)
    return out + _dot(lo, seg) if terms == 3 else out


def _tri_sum(tri, x):
    hi, mid, lo = _split3(x)
    return _dot(tri, hi) + _dot(tri, mid) + _dot(tri, lo)


def _sigmoid(x):
    return 1.0 / (1.0 + jnp.exp(-x))


def _silu(x):
    return x * _sigmoid(x)


def _softplus(x):
    return jnp.maximum(x, 0.0) + jnp.log1p(jnp.exp(-jnp.abs(x)))


def _mod_body(cond_ref, w_ref, b_ref, o_ref):
    a = _silu(cond_ref[...]).astype(BF16)
    o_ref[...] = _dot(a, w_ref[...].astype(BF16)) + b_ref[...]


def _modulation(conds, w_mod, b_mod):
    depth, d, d3 = w_mod.shape
    rows = conds.shape[0]
    tn = d
    return pl.pallas_call(
        _mod_body,
        grid=(depth, d3 // tn),
        in_specs=[
            pl.BlockSpec((rows, d), lambda l, j: (0, 0)),
            pl.BlockSpec((None, d, tn), lambda l, j: (l, 0, j)),
            pl.BlockSpec((None, 1, tn), lambda l, j: (l, 0, j)),
        ],
        out_specs=pl.BlockSpec((None, rows, tn), lambda l, j: (l, 0, j)),
        out_shape=jax.ShapeDtypeStruct((depth, rows, d3), F32),
        compiler_params=pltpu.CompilerParams(dimension_semantics=("arbitrary", "arbitrary"),
                                             vmem_limit_bytes=VMEM_LIMIT_BYTES),
        name="modulation",
    )(conds, w_mod, b_mod.reshape(depth, 1, d3))


def _rope(z, c, sa, sb):
    w = z.shape[1]
    reps = w // LANES
    if reps > 1:
        c = jnp.concatenate([c] * reps, axis=1)
        sa = jnp.concatenate([sa] * reps, axis=1)
        sb = jnp.concatenate([sb] * reps, axis=1)
    half_pair = HEAD_DIM // 4
    return z * c + pltpu.roll(z, w - half_pair, 1) * sa + pltpu.roll(z, half_pair, 1) * sb


def _proj_body(*refs, rope, keep_f32_kv, tm):
    it = iter(refs)
    x_ref, mod_ref, gpre_ref, w_ref, seg_ref, lng_ref, lnb_ref, sw_ref, sbias_ref = (next(it) for _ in range(9))
    if rope:
        rc_ref, rsa_ref, rsb_ref = (next(it) for _ in range(3))
    q_ref, kv_ref, sag_ref, ob_ref, cqkv_ref, scg_ref, ab_ref = (next(it) for _ in range(7))
    if keep_f32_kv:
        kf_ref, vf_ref = (next(it) for _ in range(2))

    x = x_ref[...]
    ms = jnp.mean(x * x, axis=-1, keepdims=True)
    h = x * lax.rsqrt(ms + EPS) * gpre_ref[...]
    h = h * (1.0 + mod_ref[1:2, :]) + mod_ref[0:1, :]
    hb = h.astype(BF16)

    def proj(lo, width):
        return _dot(hb, w_ref[:, lo:lo + width])

    zq = proj(_OFF_Q, A_WIDTH)
    zk = proj(_OFF_K, A_KV_WIDTH)
    zv = proj(_OFF_V, A_KV_WIDTH)
    if keep_f32_kv:
        kf_ref[...] = zk
        vf_ref[...] = zv
    if rope:
        c, sa, sb = rc_ref[...], rsa_ref[...], rsb_ref[...]
        zq = _rope(zq, c, sa, sb)
        zk = _rope(zk, c, sa, sb)
    q_ref[...] = (zq * (HEAD_DIM ** -0.5 * LOG2_E)).astype(BF16)
    kv_ref[:, 0:A_KV_WIDTH] = zk.astype(BF16)
    kv_ref[:, A_KV_WIDTH:2 * A_KV_WIDTH] = zv.astype(BF16)
    sag_ref[...] = _silu(proj(_OFF_AG, A_WIDTH)).astype(BF16)

    zu = proj(_OFF_BU, B_WIDTH)
    zbv = proj(_OFF_BV, B_WIDTH)
    zbg = proj(_OFF_BG, B_WIDTH)
    seg = seg_ref[...]
    inv_n = 1.0 / B_GROUP_DIM
    mean = _seg_sum(zbv, seg) * inv_n
    xc = zbv - mean
    var = _seg_sum(xc * xc, seg) * inv_n
    vg = (xc * lax.rsqrt(var + EPS) * lng_ref[...] + lnb_ref[...]).astype(BF16)
    sbias = sbias_ref[...]
    mixed_chunks = []
    for ci in range(tm // SGU_CHUNK):
        rows = slice(ci * SGU_CHUNK, (ci + 1) * SGU_CHUNK)
        per_group = [_dot(sw_ref[g], vg[rows, g * B_GROUP_DIM:(g + 1) * B_GROUP_DIM]) for g in range(B_GROUPS)]
        mixed_chunks.append(jnp.concatenate(per_group, axis=1) + sbias)
    mixed = jnp.concatenate(mixed_chunks, axis=0) if len(mixed_chunks) > 1 else mixed_chunks[0]
    ob_ref[...] = (zu * mixed * _silu(zbg)).astype(BF16)

    cqkv_ref[...] = proj(_OFF_CQKV, 3 * C_WIDTH).astype(BF16)
    scg_ref[...] = _silu(proj(_OFF_CG, C_WIDTH)).astype(BF16)
    ab_ref[...] = proj(_OFF_AB, LANES)[:, 0:2 * N_GATES]


def _projection(x, mod, gpre, w1, seg, lng, lnb, sw, sbias, rope_tabs, layer, *, mod_row0, per_batch, keep_f32_kv,
                tm):
    b, t, d = x.shape
    rope = rope_tabs is not None
    const2 = lambda bi, i: (0, 0)
    in_specs = [
        pl.BlockSpec((None, tm, d), lambda bi, i: (bi, i, 0)),
        _mod_spec(mod, layer, mod_row0, per_batch),
        _layer_spec(gpre, layer),
        _layer_spec(w1, layer),
        pl.BlockSpec(seg.shape, const2),
        _layer_spec(lng, layer),
        _layer_spec(lnb, layer),
        _layer_spec(sw, layer),
        _layer_spec(sbias, layer),
    ]
    args = [x, mod, gpre, w1, seg, lng, lnb, sw, sbias]
    if rope:
        in_specs += [pl.BlockSpec((tm, LANES), lambda bi, i: (i, 0))] * 3
        args += list(rope_tabs)

    def tok(width, dtype):
        return (pl.BlockSpec((None, tm, width), lambda bi, i: (bi, i, 0)),
                jax.ShapeDtypeStruct((b, t, width), dtype))

    outs = [tok(A_WIDTH, BF16), tok(2 * A_KV_WIDTH, BF16), tok(A_WIDTH, BF16), tok(B_WIDTH, BF16),
            tok(3 * C_WIDTH, BF16), tok(C_WIDTH, BF16), tok(2 * N_GATES, F32)]
    if keep_f32_kv:
        outs += [tok(A_KV_WIDTH, F32), tok(A_KV_WIDTH, F32)]
    return pl.pallas_call(
        functools.partial(_proj_body, rope=rope, keep_f32_kv=keep_f32_kv, tm=tm),
        grid=(b, t // tm),
        in_specs=in_specs,
        out_specs=[o[0] for o in outs],
        out_shape=[o[1] for o in outs],
        compiler_params=pltpu.CompilerParams(dimension_semantics=("arbitrary", "arbitrary"),
                                             vmem_limit_bytes=VMEM_LIMIT_BYTES),
        name="projection_rope" if rope else "projection",
    )(*args)


def _tri_masks(n, reps):
    rows = lax.broadcasted_iota(jnp.int32, (n, reps * n), 0)
    cols = lax.broadcasted_iota(jnp.int32, (n, reps * n), 1) & (n - 1)
    masks = []
    level = 0
    while (1 << level) < n:
        same_pair = (rows >> (level + 1)) == (cols >> (level + 1))
        other_half = (rows >> level) != (cols >> level)
        masks.append(same_pair & other_half)
        level += 1
    return rows, cols, rows == cols, masks


def _gdn_body(*refs, t, has_s0, emit_state, ach):
    it = iter(refs)
    cqkv_ref, ab_ref, scg_ref, convw_ref, alog_ref, dtb_ref, normg_ref, seg_ref, expand_ref = (
        next(it) for _ in range(9))
    s0_ref = next(it) if has_s0 else None
    oc_ref = next(it)
    sfin_ref = next(it) if emit_state else None
    wq_scr, u_scr, aqk_scr, kd_scr, egl_scr, s_scr, win_scr = (next(it) for _ in range(7))

    c = GDN_CHUNK
    hd = C_HEAD_DIM
    pw = 2 * hd
    n_pairs = C_HEADS // 2
    nch = t // c
    span = ach * c
    halo = BF16_SUBLANES
    rows, cols, _, _ = _tri_masks(c, 1)
    tri_l = jnp.where(rows >= cols, 1.0, 0.0).astype(BF16)
    tri_u = jnp.where(rows <= cols, 1.0, 0.0).astype(BF16)
    prow_, pcol_, eye_pw, level_pw = _tri_masks(c, 2)
    incl_pw = (prow_ >= pcol_, prow_ <= pcol_)
    strict_pw = (prow_ > pcol_, prow_ < pcol_)
    gate_lane = lax.broadcasted_iota(jnp.int32, (1, N_GATES), 1)
    is_fwd = gate_lane < C_HEADS
    seg = seg_ref[...]
    convw = convw_ref[...]
    neg_rate = -jnp.exp(alog_ref[...])
    dtb = dtb_ref[...]

    lane_pw = lax.broadcasted_iota(jnp.int32, (1, pw), 1)
    first_lanes = lane_pw < hd

    def block_diag(xb):
        zero = jnp.zeros_like(xb)
        return jnp.concatenate([jnp.where(first_lanes, xb, zero), jnp.where(first_lanes, zero, xb)], axis=0)

    expand = expand_ref[...]

    def a_step(j, first, last):
        zeros_halo = jnp.zeros((halo, 3 * C_WIDTH), F32)
        if first:
            r0 = 0
            win_scr[0:halo, :] = zeros_halo
            n_in = span + (0 if last else halo)
            win_scr[halo:halo + n_in, :] = cqkv_ref[0:n_in, :].astype(F32)
            if last:
                win_scr[halo + span:2 * halo + span, :] = zeros_halo
        elif last:
            r0 = t - span
            win_scr[0:halo + span, :] = cqkv_ref[t - span - halo:t, :].astype(F32)
            win_scr[halo + span:2 * halo + span, :] = zeros_halo
        else:
            r0 = pl.multiple_of(j * span, span)
            win_scr[...] = cqkv_ref[pl.ds(r0 - halo, span + 2 * halo), :].astype(F32)
        y = win_scr[halo:halo + span, :] * convw[CONV_K // 2:CONV_K // 2 + 1, :]
        for tap in range(CONV_K):
            if tap != CONV_K // 2:
                lo = halo - CONV_K // 2 + tap
                y = y + win_scr[lo:lo + span, :] * convw[tap:tap + 1, :]
        y = _silu(y)
        q_all = y[:, 0:C_WIDTH]
        k_all = y[:, C_WIDTH:2 * C_WIDTH]
        v_all = y[:, 2 * C_WIDTH:3 * C_WIDTH]
        q_all = q_all * lax.rsqrt(_seg_sum(q_all * q_all, seg, terms=2) + EPS) * (hd ** -0.5)
        k_all = k_all * lax.rsqrt(_seg_sum(k_all * k_all, seg, terms=2) + EPS)

        ab = ab_ref[pl.ds(r0, span), :]
        g_all = neg_rate * _softplus(ab[:, 0:N_GATES] + dtb)
        beta_all = _sigmoid(ab[:, N_GATES:2 * N_GATES])

        gc_x, beta_x, e_x, begc_x, eglgc_x, egl_x, grow = [], [], [], [], [], [], []
        for ci in range(ach):
            rs = slice(ci * c, (ci + 1) * c)
            gc_f = _tri_sum(tri_l, g_all[rs])
            gc_b = _tri_sum(tri_u, g_all[rs])
            gcc = jnp.where(is_fwd, gc_f, gc_b)
            gl = jnp.where(is_fwd, gc_f[c - 1:c, :], gc_b[0:1, :])
            grow.append(gcc.T)
            stacked = jnp.concatenate([gcc, jnp.broadcast_to(gl, (8, N_GATES)), beta_all[rs]], axis=0)
            hi, mid, lo = _split3(stacked)
            ex = _dot(hi, expand) + _dot(mid, expand) + _dot(lo, expand)
            gx = ex[0:c, :]
            glx = ex[c:c + 1, :]
            bx = ex[c + 8:2 * c + 8, :]
            ee = jnp.exp(gx)
            gc_x.append(gx)
            beta_x.append(bx)
            e_x.append(ee)
            begc_x.append(bx * ee)
            eglgc_x.append(jnp.exp(glx - gx))
            egl_x.append(jnp.exp(glx))

        cps = [(ci, p) for ci in range(ach) for p in range(n_pairs)]
        chains = [(ci, d, p) for ci in range(ach) for d in range(N_DIRS) for p in range(n_pairs)]

        def pslice(arr, ci, p):
            return arr[ci * c:(ci + 1) * c, p * pw:(p + 1) * pw]

        def xs(vals, ci, d, p):
            lo = d * C_WIDTH + p * pw
            return vals[ci][:, lo:lo + pw]

        kp16 = {key: pslice(k_all, *key).astype(BF16) for key in cps}
        k_bd = {key: block_diag(kp16[key]) for key in cps}
        kk = {key: _dot_nt(kp16[key], k_bd[key]) for key in cps}
        qk = {key: _dot_nt(pslice(q_all, *key).astype(BF16), k_bd[key]) for key in cps}

        decay, a_mat = {}, {}
        for ci, d, p in chains:
            g0 = d * C_HEADS + 2 * p
            grow_pair = jnp.concatenate([grow[ci][g0:g0 + 1, :], grow[ci][g0 + 1:g0 + 2, :]], axis=1)
            dec = jnp.where(incl_pw[d], jnp.exp(xs(gc_x, ci, d, p) - grow_pair), 0.0)
            decay[ci, d, p] = dec
            a_mat[ci, d, p] = jnp.where(strict_pw[d], kk[ci, p] * xs(beta_x, ci, d, p) * dec, 0.0)

        t_mat = {ch: jnp.where(eye_pw, 1.0, 0.0) - jnp.where(level_pw[0], a_mat[ch], 0.0) for ch in chains}
        for mask in level_pw[1:]:
            tb = {ch: t_mat[ch].astype(BF16) for ch in chains}
            t_bd = {ch: block_diag(tb[ch]) for ch in chains}
            et = {ch: _dot(jnp.where(mask, a_mat[ch], 0.0).astype(BF16), t_bd[ch]).astype(BF16) for ch in chains}
            t_mat = {ch: t_mat[ch] - _dot(tb[ch], block_diag(et[ch])) for ch in chains}

        uw = {}
        for ci, d, p in chains:
            vb = (pslice(v_all, ci, p) * xs(beta_x, ci, d, p)).astype(BF16)
            kbe = (pslice(k_all, ci, p) * xs(begc_x, ci, d, p)).astype(BF16)
            rhs = jnp.concatenate([block_diag(vb), block_diag(kbe)], axis=1)
            uw[ci, d, p] = _dot(t_mat[ci, d, p].astype(BF16), rhs)

        for ci in range(ach):
            m = (r0 // c) + ci
            q_c = q_all[ci * c:(ci + 1) * c, :]
            k_c = k_all[ci * c:(ci + 1) * c, :]
            for d in range(N_DIRS):
                dl = slice(d * C_WIDTH, (d + 1) * C_WIDTH)
                wq_scr[m, d, c:2 * c, :] = (q_c * e_x[ci][:, dl]).astype(BF16)
                kd_scr[m, d] = (k_c * eglgc_x[ci][:, dl]).astype(BF16)
                egl_scr[m, d] = egl_x[ci][:, dl]
                for p in range(n_pairs):
                    pl_ = slice(p * pw, (p + 1) * pw)
                    u_scr[m, d, :, pl_] = uw[ci, d, p][:, 0:pw].astype(BF16)
                    wq_scr[m, d, 0:c, pl_] = uw[ci, d, p][:, pw:2 * pw].astype(BF16)
                    aqk_scr[m, d, :, pl_] = (qk[ci, p] * decay[ci, d, p]).astype(BF16)

    n_steps = nch // ach
    if n_steps == 1:
        a_step(0, True, True)
    else:
        a_step(0, True, False)
        if n_steps > 2:
            def a_loop(j, carry):
                a_step(j, False, False)
                return carry
            lax.fori_loop(1, n_steps - 1, a_loop, 0)
        a_step(n_steps - 1, False, True)

    zeros_h = jnp.zeros((hd, hd), F32)
    for d in range(N_DIRS):
        for p in range(n_pairs):
            if has_s0:
                top = jnp.concatenate([s0_ref[d, 2 * p], zeros_h], axis=1)
                bot = jnp.concatenate([zeros_h, s0_ref[d, 2 * p + 1]], axis=1)
                s_scr[d, p] = jnp.concatenate([top, bot], axis=0)
            else:
                s_scr[d, p] = jnp.zeros((pw, pw), F32)

    prow = lax.broadcasted_iota(jnp.int32, (pw, pw), 0)
    pcol = lax.broadcasted_iota(jnp.int32, (pw, pw), 1)
    same_head = (prow >= hd) == (pcol >= hd)
    first_head_lanes = lax.broadcasted_iota(jnp.int32, (1, pw), 1) < hd
    pairs = [(d, p) for d in range(N_DIRS) for p in range(n_pairs)]

    def b_step(n, carry):
        ms = (n, nch - 1 - n)
        ls = {(d, p): slice(p * pw, (p + 1) * pw) for d, p in pairs}
        state = {(d, p): s_scr[d, p] for d, p in pairs}
        r = {(d, p): _dot(wq_scr[ms[d], d, :, ls[d, p]], state[d, p].astype(BF16)) for d, p in pairs}
        vnb = {}
        for d, p in pairs:
            v_new = u_scr[ms[d], d, :, ls[d, p]].astype(F32) - r[d, p][0:c, :]
            vnb[d, p] = v_new.astype(BF16)
        for d, p in pairs:
            zero = jnp.zeros_like(vnb[d, p])
            vn_bd = jnp.concatenate([jnp.where(first_head_lanes, vnb[d, p], zero),
                                     jnp.where(first_head_lanes, zero, vnb[d, p])], axis=0)
            o = r[d, p][c:2 * c, :] + _dot(aqk_scr[ms[d], d, :, ls[d, p]], vn_bd)
            kv = _dot_tn(kd_scr[ms[d], d, :, ls[d, p]], vnb[d, p])
            s_scr[d, p] = state[d, p] * egl_scr[ms[d], d, :, ls[d, p]] + jnp.where(same_head, kv, 0.0)
            u_scr[ms[d], d, :, ls[d, p]] = o.astype(BF16)
        return carry

    lax.fori_loop(0, nch, b_step, 0)

    inv_n = 1.0 / hd

    c_chunks = 4 if nch % 4 == 0 else 1

    def c_step(i, carry):
        m0 = i * c_chunks
        o = jnp.concatenate([u_scr[m0 + ci, 0].astype(F32) + u_scr[m0 + ci, 1].astype(F32)
                             for ci in range(c_chunks)], axis=0)
        r0 = pl.multiple_of(m0 * c, c_chunks * c)
        ms = _seg_sum(o * o, seg, terms=2) * inv_n
        on = o * lax.rsqrt(ms + EPS) * normg_ref[...]
        oc_ref[pl.ds(r0, c_chunks * c), :] = (on * scg_ref[pl.ds(r0, c_chunks * c), :].astype(F32)).astype(BF16)
        return carry

    lax.fori_loop(0, nch // c_chunks, c_step, 0)

    if emit_state:
        for d in range(N_DIRS):
            for p in range(n_pairs):
                s_pair = s_scr[d, p]
                sfin_ref[d, 2 * p] = s_pair[0:hd, 0:hd]
                sfin_ref[d, 2 * p + 1] = s_pair[hd:pw, hd:pw]


def _gdn(cqkv, ab, scg, convw, alog, dtb, normg, seg, expand, s0, layer, *, emit_state, ach):
    b, t, _ = cqkv.shape
    nch = t // GDN_CHUNK
    assert nch % ach == 0
    has_s0 = s0 is not None
    const2 = lambda bi: (0, 0)
    in_specs = [
        pl.BlockSpec((None, t, 3 * C_WIDTH), lambda bi: (bi, 0, 0)),
        pl.BlockSpec((None, t, 2 * N_GATES), lambda bi: (bi, 0, 0)),
        pl.BlockSpec((None, t, C_WIDTH), lambda bi: (bi, 0, 0)),
        _layer_spec(convw, layer),
        _layer_spec(alog, layer),
        _layer_spec(dtb, layer),
        _layer_spec(normg, layer),
        pl.BlockSpec(seg.shape, const2),
        pl.BlockSpec(expand.shape, const2),
    ]
    args = [cqkv, ab, scg, convw, alog, dtb, normg, seg, expand]
    state_spec = pl.BlockSpec((None, N_DIRS, C_HEADS, C_HEAD_DIM, C_HEAD_DIM), lambda bi: (bi, 0, 0, 0, 0))
    if has_s0:
        in_specs.append(pl.BlockSpec((None, None, N_DIRS, C_HEADS, C_HEAD_DIM, C_HEAD_DIM),
                                     lambda bi: (bi, layer, 0, 0, 0, 0)))
        args.append(s0)
    out_specs = [pl.BlockSpec((None, t, C_WIDTH), lambda bi: (bi, 0, 0))]
    out_shape = [jax.ShapeDtypeStruct((b, t, C_WIDTH), BF16)]
    if emit_state:
        out_specs.append(state_spec)
        out_shape.append(jax.ShapeDtypeStruct((b, N_DIRS, C_HEADS, C_HEAD_DIM, C_HEAD_DIM), F32))
    pair_w = 2 * C_HEAD_DIM
    scratch = [
        pltpu.VMEM((nch, N_DIRS, 2 * GDN_CHUNK, C_WIDTH), BF16),
        pltpu.VMEM((nch, N_DIRS, GDN_CHUNK, C_WIDTH), BF16),
        pltpu.VMEM((nch, N_DIRS, GDN_CHUNK, C_WIDTH), BF16),
        pltpu.VMEM((nch, N_DIRS, GDN_CHUNK, C_WIDTH), BF16),
        pltpu.VMEM((nch, N_DIRS, 1, C_WIDTH), F32),
        pltpu.VMEM((N_DIRS, C_HEADS // 2, pair_w, pair_w), F32),
        pltpu.VMEM((ach * GDN_CHUNK + 2 * BF16_SUBLANES, 3 * C_WIDTH), F32),
    ]
    return pl.pallas_call(
        functools.partial(_gdn_body, t=t, has_s0=has_s0, emit_state=emit_state, ach=ach),
        grid=(b,),
        in_specs=in_specs,
        out_specs=out_specs,
        out_shape=out_shape,
        scratch_shapes=scratch,
        compiler_params=pltpu.CompilerParams(dimension_semantics=("arbitrary",),
                                             vmem_limit_bytes=VMEM_LIMIT_BYTES),
        name="gdn_state" if emit_state else "gdn",
    )(*args)


def _attn_body(*refs, banded, nseq, tq, nblk_total):
    it = iter(refs)
    x_ref, q_ref, kv_ref = (next(it) for _ in range(3))
    if banded:
        kvl_ref, kvr_ref, band_ref = (next(it) for _ in range(3))
        ck_ref, cv_ref = (next(it) for _ in range(2))
    sink_ref, sag_ref, ob_ref, oc_ref, wout_ref, gpost_ref, mod_ref, y_ref = (next(it) for _ in range(8))

    nb = tq // ATTN_BLOCK
    rows_q = A_GROUP * ATTN_BLOCK
    lane = lax.broadcasted_iota(jnp.int32, (1, A_KV_WIDTH), 1)
    head_mask = [lane < HEAD_DIM, lane >= HEAD_DIM]
    sum_lane = [HEAD_DIM, 0]

    def split_heads(a, spare=0.0):
        fill = [jnp.where(lane == sum_lane[h], spare, 0.0).astype(a.dtype) for h in range(A_KV_HEADS)]
        return [jnp.where(head_mask[h], a, fill[h]) for h in range(A_KV_HEADS)]

    if banded:
        kv_ext = jnp.concatenate([kvl_ref[0], kv_ref[0], kvr_ref[0]], axis=0)
        k_own = split_heads(kv_ext[:, 0:A_KV_WIDTH])
        v_own = split_heads(kv_ext[:, A_KV_WIDTH:2 * A_KV_WIDTH], spare=1.0)
        k_ctx = [split_heads(ck_ref[...].astype(BF16))]
        v_ctx = [split_heads(cv_ref[...].astype(BF16), spare=1.0)]
        first_blk = pl.program_id(1) * nb
        vis_l, vis_r = [], []
        for j in range(nb):
            thr_l = jnp.where(first_blk + j > 0, 0.5, 2.0)
            thr_r = jnp.where(first_blk + j < nblk_total - 1, 0.5, 2.0)
            vis_l.append(band_ref[:, 0:ATTN_BLOCK] > thr_l)
            vis_r.append(band_ref[:, 2 * ATTN_BLOCK:3 * ATTN_BLOCK] > thr_r)
    else:
        k_ctx = [split_heads(kv_ref[s, :, 0:A_KV_WIDTH]) for s in range(nseq)]
        v_ctx = [split_heads(kv_ref[s, :, A_KV_WIDTH:2 * A_KV_WIDTH], spare=1.0) for s in range(nseq)]

    units = [(s, j, h) for s in range(nseq) for j in range(nb) for h in range(A_KV_HEADS)]
    q2 = {(s, j): jnp.concatenate([q_ref[s, j * ATTN_BLOCK:(j + 1) * ATTN_BLOCK, g * A_KV_WIDTH:(g + 1) * A_KV_WIDTH]
                                   for g in range(A_GROUP)], axis=0) for s in range(nseq) for j in range(nb)}

    def scores(s, j, h):
        s_ctx = _dot_nt(q2[s, j], k_ctx[s][h])
        if not banded:
            return s_ctx, None
        s_own = _dot_nt(q2[s, j], k_own[h][j * ATTN_BLOCK:(j + 3) * ATTN_BLOCK])
        left = jnp.where(vis_l[j], s_own[:, 0:ATTN_BLOCK], NEG_INF)
        right = jnp.where(vis_r[j], s_own[:, 2 * ATTN_BLOCK:3 * ATTN_BLOCK], NEG_INF)
        return s_ctx, jnp.concatenate([left, s_own[:, ATTN_BLOCK:2 * ATTN_BLOCK], right], axis=1)

    o_acc = {(s, j): jnp.zeros((rows_q, A_KV_WIDTH), F32) for s in range(nseq) for j in range(nb)}
    pending = scores(*units[0])
    for ui, (s, j, h) in enumerate(units):
        s_ctx, s_own = pending
        if ui + 1 < len(units):
            pending = scores(*units[ui + 1])
        sink = sink_ref[h] * LOG2_E
        s_all = jnp.concatenate([s_ctx, s_own], axis=1) if banded else s_ctx
        m = jnp.maximum(jnp.max(s_all, axis=-1, keepdims=True), sink)
        pb = jnp.exp2(s_all - m).astype(BF16)
        n_ctx = s_ctx.shape[1]
        acc = _dot(pb[:, 0:n_ctx], v_ctx[s][h])
        if banded:
            acc = acc + _dot(pb[:, n_ctx:], v_own[h][j * ATTN_BLOCK:(j + 3) * ATTN_BLOCK])
        denom = acc[:, sum_lane[h]:sum_lane[h] + 1] + jnp.exp2(sink - m)
        o_acc[s, j] = o_acc[s, j] + jnp.where(head_mask[h], acc, 0.0) * (1.0 / denom)
    o_blocks = [jnp.concatenate([o_acc[s, j][g * ATTN_BLOCK:(g + 1) * ATTN_BLOCK] for g in range(A_GROUP)], axis=1)
                for s in range(nseq) for j in range(nb)]
    o_a = jnp.concatenate(o_blocks, axis=0) if len(o_blocks) > 1 else o_blocks[0]

    def rows_of(ref):
        return jnp.concatenate([ref[s] for s in range(nseq)], axis=0) if nseq > 1 else ref[0]

    mix_a = (o_a * rows_of(sag_ref).astype(F32)).astype(BF16)
    mix = (_dot(mix_a, wout_ref[0:A_WIDTH, :])
           + _dot(rows_of(ob_ref), wout_ref[A_WIDTH:A_WIDTH + B_WIDTH, :])
           + _dot(rows_of(oc_ref), wout_ref[A_WIDTH + B_WIDTH:A_WIDTH + B_WIDTH + C_WIDTH, :]))
    ms = jnp.mean(mix * mix, axis=-1, keepdims=True)
    y = rows_of(x_ref) + mod_ref[2:3, :] * (mix * lax.rsqrt(ms + EPS) * gpost_ref[...])
    for s in range(nseq):
        y_ref[s] = y[s * tq:(s + 1) * tq]


def _attention_out(x, q, kv, sag, ob, oc, sink_rows, wout, gpost, mod, band, ctx_k, ctx_v, layer, *, mod_row0,
                   per_batch, tq, nseq):
    b, t, d = x.shape
    banded = band is not None
    assert b % nseq == 0 and (nseq == 1 or not per_batch)
    nblk_total = t // ATTN_BLOCK
    per_q = tq // ATTN_BLOCK
    const2 = lambda bi, i: (0, 0)

    def tok(width):
        return pl.BlockSpec((nseq, tq, width), lambda bi, i: (bi, i, 0))

    in_specs = [tok(d), tok(A_WIDTH)]
    args = [x, q]
    if banded:
        in_specs.append(tok(2 * A_KV_WIDTH))
        args.append(kv)
        in_specs.append(pl.BlockSpec((1, ATTN_BLOCK, 2 * A_KV_WIDTH),
                                     lambda bi, i: (bi, jnp.maximum(i * per_q - 1, 0), 0)))
        in_specs.append(pl.BlockSpec((1, ATTN_BLOCK, 2 * A_KV_WIDTH),
                                     lambda bi, i: (bi, jnp.minimum((i + 1) * per_q, nblk_total - 1), 0)))
        in_specs.append(pl.BlockSpec(band.shape, const2))
        args += [kv, kv, band]
        past = ctx_k.shape[2]
        in_specs += [pl.BlockSpec((None, None, past, A_KV_WIDTH), lambda bi, i: (bi, layer, 0, 0))] * 2
        args += [ctx_k, ctx_v]
    else:
        in_specs.append(pl.BlockSpec((nseq, t, 2 * A_KV_WIDTH), lambda bi, i: (bi, 0, 0)))
        args.append(kv)
    in_specs += [
        _layer_spec(sink_rows, layer),
        tok(A_WIDTH), tok(B_WIDTH), tok(C_WIDTH),
        _layer_spec(wout, layer),
        _layer_spec(gpost, layer),
        _mod_spec(mod, layer, mod_row0, per_batch),
    ]
    args += [sink_rows, sag, ob, oc, wout, gpost, mod]
    return pl.pallas_call(
        functools.partial(_attn_body, banded=banded, nseq=nseq, tq=tq, nblk_total=nblk_total),
        grid=(b // nseq, t // tq),
        in_specs=in_specs,
        out_specs=tok(d),
        out_shape=jax.ShapeDtypeStruct((b, t, d), F32),
        compiler_params=pltpu.CompilerParams(dimension_semantics=("arbitrary", "arbitrary"),
                                             vmem_limit_bytes=VMEM_LIMIT_BYTES),
        name="attention_banded" if banded else "attention_full",
    )(*args)


def _block_ones(width, block):
    idx = np.arange(width) // block
    return jnp.asarray((idx[:, None] == idx[None, :]).astype(np.float32), dtype=BF16)


def _gate_expansion():
    lane_gate = np.arange(N_GATES * C_HEAD_DIM) // C_HEAD_DIM
    return jnp.asarray((np.arange(N_GATES)[:, None] == lane_gate[None, :]).astype(np.float32), dtype=BF16)


def _rope_tables(n):
    half = HEAD_DIM // 2
    nf = half // 2
    pos = np.arange(n)
    row = (pos // GRID_W).astype(np.float32)
    colp = (pos % GRID_W).astype(np.float32)
    inv_freq = np.float32(ROPE_BASE) ** (-np.arange(nf, dtype=np.float32) / np.float32(nf))
    lane = np.arange(LANES) % HEAD_DIM
    freq_idx = lane % nf
    quarter = lane // nf
    p = np.where((quarter < 2)[None, :], row[:, None], colp[:, None])
    ang = (p * inv_freq[freq_idx][None, :]).astype(np.float32)
    cos, sin = np.cos(ang), np.sin(ang)
    first = (quarter % 2 == 0)[None, :]
    sa = np.where(first, -sin, 0.0)
    sb = np.where(first, 0.0, sin)
    return tuple(jnp.asarray(a, dtype=F32) for a in (cos, sa, sb))


def _band_table():
    q_rel = np.arange(ATTN_BLOCK)
    k_rel = np.arange(3 * ATTN_BLOCK) - ATTN_BLOCK
    band = (np.abs(q_rel[:, None] - k_rel[None, :]) <= ATTN_BLOCK).astype(np.float32)
    return jnp.asarray(np.tile(band, (A_GROUP, 1)))


def _group_major(a, axis):
    shape = a.shape
    split = shape[:axis] + (A_KV_HEADS, A_GROUP, HEAD_DIM) + shape[axis + 1:]
    return jnp.swapaxes(a.reshape(split), axis, axis + 1).reshape(shape)


def _layout_w_in(w_in):
    o = 0
    sec = {}
    for name, width in (("aq", A_WIDTH), ("ak", A_KV_WIDTH), ("av", A_KV_WIDTH), ("ag", A_WIDTH),
                        ("bu", B_WIDTH), ("bv", B_WIDTH), ("bg", B_WIDTH),
                        ("cq", C_WIDTH), ("ck", C_WIDTH), ("cv", C_WIDTH), ("ca", N_GATES), ("cb", N_GATES),
                        ("cg", C_WIDTH)):
        sec[name] = w_in[:, :, o:o + width].astype(BF16)
        o += width
    pad = jnp.zeros(w_in.shape[:2] + (LANES - 2 * N_GATES,), BF16)
    cols = [_group_major(sec["aq"], 2), sec["ak"], sec["av"], _group_major(sec["ag"], 2), sec["bu"], sec["bv"],
            sec["bg"], sec["cq"], sec["ck"], sec["cv"], sec["cg"], sec["ca"], sec["cb"], pad]
    return jnp.concatenate(cols, axis=2)


def _row_tile(t, target):
    tile = min(t, target)
    while t % tile:
        tile -= ATTN_BLOCK
    return tile


def kernel(x_prompt, x_sample, cache_k, cache_v, state_delta, c, c_ctx, w_mod, b_mod, g_pre, w_in, attn_sink,
           sgu_ln_g, sgu_ln_b, sgu_w, sgu_b, gdn_conv_w, gdn_a_log, gdn_dt_bias, gdn_norm_g, g_post, w_out):
    depth, d, _ = w_mod.shape
    b_ctx, t_ctx, _ = x_prompt.shape
    b_lat, t_lat, _ = x_sample.shape
    past = cache_k.shape[2]
    assert t_ctx % SGU_CHUNK == 0 and t_lat % SGU_CHUNK == 0 and t_lat % GRID_W == 0
    assert t_ctx // GDN_CHUNK >= 2 and t_lat // GDN_CHUNK >= 2
    assert w_in.shape[2] == 2 * A_WIDTH + 2 * A_KV_WIDTH + 3 * B_WIDTH + 4 * C_WIDTH + 2 * N_GATES

    n_cond = b_lat + 1
    rows = -(-n_cond // 8) * 8
    conds = jnp.concatenate([c, c_ctx[None, :], jnp.zeros((rows - n_cond, d), c.dtype)], axis=0)
    mod = _modulation(conds, w_mod, b_mod).reshape(depth, rows, 3, d)

    seg_b = _block_ones(B_WIDTH, B_GROUP_DIM)
    seg_c = _block_ones(C_WIDTH, C_HEAD_DIM)
    expand_c = _gate_expansion()
    rope_tabs = _rope_tables(t_lat)
    band = _band_table()
    ck = cache_k.reshape(b_lat, depth, past, A_KV_WIDTH)
    cv = cache_v.reshape(b_lat, depth, past, A_KV_WIDTH)
    tm_ctx, tm_lat = _row_tile(b_ctx * t_ctx, PROJ_ROWS_CTX), _row_tile(t_lat, PROJ_ROWS_LAT)

    w1 = _layout_w_in(w_in)
    wout_b = w_out.astype(BF16)
    wout = jnp.concatenate([_group_major(wout_b[:, 0:A_WIDTH], 1), wout_b[:, A_WIDTH:]], axis=1)
    gpre = g_pre.reshape(depth, 1, d)
    gpost = g_post.reshape(depth, 1, d)
    lng = sgu_ln_g.reshape(depth, 1, B_WIDTH)
    lnb = sgu_ln_b.reshape(depth, 1, B_WIDTH)
    sw = sgu_w.astype(BF16)
    sbias = jnp.repeat(jnp.swapaxes(sgu_b, 1, 2), B_GROUP_DIM, axis=2)
    sink_rows = jnp.repeat(attn_sink.reshape(depth, A_KV_HEADS, A_GROUP), ATTN_BLOCK, axis=2)[..., None]
    alog = gdn_a_log.reshape(depth, 1, N_GATES)
    dtb = gdn_dt_bias.reshape(depth, 1, N_GATES)
    normg = jnp.tile(gdn_norm_g, (1, C_HEADS)).reshape(depth, 1, C_WIDTH)
    ctx_nseq = max(1, min(b_ctx, ATTN_ROWS_CTX // t_ctx))
    while b_ctx % ctx_nseq:
        ctx_nseq -= 1

    xp, xs = x_prompt, x_sample
    new_k, new_v, new_s = [], [], []
    for l in range(depth):
        flat = _projection(xp.reshape(1, b_ctx * t_ctx, d), mod, gpre, w1, seg_b, lng, lnb, sw, sbias, None, l,
                           mod_row0=b_lat, per_batch=False, keep_f32_kv=True, tm=tm_ctx)
        q, kv, sag, ob, cqkv, scg, ab, kf, vf = (a.reshape(b_ctx, t_ctx, a.shape[-1]) for a in flat)
        oc, s_fin = _gdn(cqkv, ab, scg, gdn_conv_w, alog, dtb, normg, seg_c, expand_c, None, l, emit_state=True,
                         ach=min(GDN_A_CHUNKS, t_ctx // GDN_CHUNK))
        xp = _attention_out(xp, q, kv, sag, ob, oc, sink_rows, wout, gpost, mod, None, None, None, l,
                            mod_row0=b_lat, per_batch=False, tq=t_ctx, nseq=ctx_nseq)
        new_k.append(kf.reshape(b_ctx, t_ctx, A_KV_HEADS, HEAD_DIM))
        new_v.append(vf.reshape(b_ctx, t_ctx, A_KV_HEADS, HEAD_DIM))
        new_s.append(s_fin)

        q, kv, sag, ob, cqkv, scg, ab = _projection(
            xs, mod, gpre, w1, seg_b, lng, lnb, sw, sbias, rope_tabs, l, mod_row0=0, per_batch=True,
            keep_f32_kv=False, tm=tm_lat)
        (oc,) = _gdn(cqkv, ab, scg, gdn_conv_w, alog, dtb, normg, seg_c, expand_c, state_delta, l, emit_state=False,
                     ach=min(GDN_A_CHUNKS, t_lat // GDN_CHUNK))
        xs = _attention_out(xs, q, kv, sag, ob, oc, sink_rows, wout, gpost, mod, band, ck, cv, l,
                            mod_row0=0, per_batch=True, tq=_row_tile(t_lat, ATTN_ROWS_LAT), nseq=1)

    return (xp, xs, jnp.stack(new_k, axis=1), jnp.stack(new_v, axis=1), jnp.stack(new_s, axis=1))
```

```python
import functools

import numpy as np
import jax
import jax.numpy as jnp
from jax import lax
from jax.experimental import pallas as pl
from jax.experimental.pallas import tpu as pltpu

F32 = jnp.float32
BF16 = jnp.bfloat16

GRID_W = 64
HEAD_DIM = 64
A_HEADS = 8
A_KV_HEADS = 2
A_GROUP = A_HEADS // A_KV_HEADS
A_WIDTH = A_HEADS * HEAD_DIM
A_KV_WIDTH = A_KV_HEADS * HEAD_DIM
ATTN_BLOCK = 128
ROPE_BASE = 10000.0
B_GROUPS = 4
B_GROUP_DIM = 64
B_WIDTH = B_GROUPS * B_GROUP_DIM
SGU_CHUNK = 128
C_HEADS = 4
C_HEAD_DIM = 64
C_WIDTH = C_HEADS * C_HEAD_DIM
CONV_K = 5
GDN_CHUNK = 64
N_DIRS = 2
N_GATES = N_DIRS * C_HEADS
EPS = 1e-6
NEG_INF = -1e30
LOG2_E = 1.4426950408889634

LANES = 128
BF16_SUBLANES = 16
VMEM_LIMIT_BYTES = 56 * 1024 * 1024

GDN_A_CHUNKS = 4

PROJ_ROWS_CTX = 512
PROJ_ROWS_LAT = 1024
ATTN_ROWS_CTX = 1024
ATTN_ROWS_LAT = 512

_OFF_Q = 0
_OFF_K = _OFF_Q + A_WIDTH
_OFF_V = _OFF_K + A_KV_WIDTH
_OFF_AG = _OFF_V + A_KV_WIDTH
_OFF_BU = _OFF_AG + A_WIDTH
_OFF_BV = _OFF_BU + B_WIDTH
_OFF_BG = _OFF_BV + B_WIDTH
_OFF_CQKV = _OFF_BG + B_WIDTH
_OFF_CG = _OFF_CQKV + 3 * C_WIDTH
_OFF_AB = _OFF_CG + C_WIDTH
_W1_COLS = _OFF_AB + LANES

def _dot(a, b):
    return jnp.dot(a, b, preferred_element_type=F32)


def _layer_spec(arr, layer):
    zeros = (0,) * (arr.ndim - 1)
    return pl.BlockSpec((None,) + tuple(arr.shape[1:]), lambda *_: (layer,) + zeros)


def _mod_spec(mod, layer, row0, per_batch):
    return pl.BlockSpec((None, None) + tuple(mod.shape[2:]),
                        lambda bi, i: (layer, row0 + (bi if per_batch else 0), 0, 0))


def _dot_nt(a, b):
    return lax.dot_general(a, b, (((1,), (1,)), ((), ())), preferred_element_type=F32)


def _dot_tn(a, b):
    return lax.dot_general(a, b, (((0,), (0,)), ((), ())), preferred_element_type=F32)


def _split3(x):
    hi = x.astype(BF16)
    r1 = x - hi.astype(F32)
    mid = r1.astype(BF16)
    lo = (r1 - mid.astype(F32)).astype(BF16)
    return hi, mid, lo


def _seg_sum(x, seg, terms=3):
    hi, mid, lo = _split3(x)
    out = _dot(hi, seg) + _dot(mid, seg)
    return out + _dot(lo, seg) if terms == 3 else out


def _tri_sum(tri, x):
    hi, mid, lo = _split3(x)
    return _dot(tri, hi) + _dot(tri, mid) + _dot(tri, lo)


def _sigmoid(x):
    return 1.0 / (1.0 + jnp.exp(-x))


def _silu(x):
    return x * _sigmoid(x)


def _softplus(x):
    return jnp.maximum(x, 0.0) + jnp.log1p(jnp.exp(-jnp.abs(x)))


def _mod_body(cond_ref, w_ref, b_ref, o_ref):
    a = _silu(cond_ref[...]).astype(BF16)
    o_ref[...] = _dot(a, w_ref[...].astype(BF16)) + b_ref[...]


def _modulation(conds, w_mod, b_mod):
    depth, d, d3 = w_mod.shape
    rows = conds.shape[0]
    tn = d
    return pl.pallas_call(
        _mod_body,
        grid=(depth, d3 // tn),
        in_specs=[
            pl.BlockSpec((rows, d), lambda l, j: (0, 0)),
            pl.BlockSpec((None, d, tn), lambda l, j: (l, 0, j)),
            pl.BlockSpec((None, 1, tn), lambda l, j: (l, 0, j)),
        ],
        out_specs=pl.BlockSpec((None, rows, tn), lambda l, j: (l, 0, j)),
        out_shape=jax.ShapeDtypeStruct((depth, rows, d3), F32),
        compiler_params=pltpu.CompilerParams(dimension_semantics=("arbitrary", "arbitrary"),
                                             vmem_limit_bytes=VMEM_LIMIT_BYTES),
        name="modulation",
    )(conds, w_mod, b_mod.reshape(depth, 1, d3))


def _rope(z, c, sa, sb):
    w = z.shape[1]
    reps = w // LANES
    if reps > 1:
        c = jnp.concatenate([c] * reps, axis=1)
        sa = jnp.concatenate([sa] * reps, axis=1)
        sb = jnp.concatenate([sb] * reps, axis=1)
    half_pair = HEAD_DIM // 4
    return z * c + pltpu.roll(z, w - half_pair, 1) * sa + pltpu.roll(z, half_pair, 1) * sb


def _proj_body(*refs, rope, keep_f32_kv, tm):
    it = iter(refs)
    x_ref, mod_ref, gpre_ref, w_ref, seg_ref, lng_ref, lnb_ref, sw_ref, sbias_ref = (next(it) for _ in range(9))
    if rope:
        rc_ref, rsa_ref, rsb_ref = (next(it) for _ in range(3))
    q_ref, kv_ref, sag_ref, ob_ref, cqkv_ref, scg_ref, ab_ref = (next(it) for _ in range(7))
    if keep_f32_kv:
        kf_ref, vf_ref = (next(it) for _ in range(2))

    x = x_ref[...]
    ms = jnp.mean(x * x, axis=-1, keepdims=True)
    h = x * lax.rsqrt(ms + EPS) * gpre_ref[...]
    h = h * (1.0 + mod_ref[1:2, :]) + mod_ref[0:1, :]
    hb = h.astype(BF16)

    def proj(lo, width):
        return _dot(hb, w_ref[:, lo:lo + width])

    zq = proj(_OFF_Q, A_WIDTH)
    zk = proj(_OFF_K, A_KV_WIDTH)
    zv = proj(_OFF_V, A_KV_WIDTH)
    if keep_f32_kv:
        kf_ref[...] = zk
        vf_ref[...] = zv
    if rope:
        c, sa, sb = rc_ref[...], rsa_ref[...], rsb_ref[...]
        zq = _rope(zq, c, sa, sb)
        zk = _rope(zk, c, sa, sb)
    q_ref[...] = (zq * (HEAD_DIM ** -0.5 * LOG2_E)).astype(BF16)
    kv_ref[:, 0:A_KV_WIDTH] = zk.astype(BF16)
    kv_ref[:, A_KV_WIDTH:2 * A_KV_WIDTH] = zv.astype(BF16)
    sag_ref[...] = _silu(proj(_OFF_AG, A_WIDTH)).astype(BF16)

    zu = proj(_OFF_BU, B_WIDTH)
    zbv = proj(_OFF_BV, B_WIDTH)
    zbg = proj(_OFF_BG, B_WIDTH)
    seg = seg_ref[...]
    inv_n = 1.0 / B_GROUP_DIM
    mean = _seg_sum(zbv, seg) * inv_n
    xc = zbv - mean
    var = _seg_sum(xc * xc, seg) * inv_n
    vg = (xc * lax.rsqrt(var + EPS) * lng_ref[...] + lnb_ref[...]).astype(BF16)
    sbias = sbias_ref[...]
    mixed_chunks = []
    for ci in range(tm // SGU_CHUNK):
        rows = slice(ci * SGU_CHUNK, (ci + 1) * SGU_CHUNK)
        per_group = [_dot(sw_ref[g], vg[rows, g * B_GROUP_DIM:(g + 1) * B_GROUP_DIM]) for g in range(B_GROUPS)]
        mixed_chunks.append(jnp.concatenate(per_group, axis=1) + sbias)
    mixed = jnp.concatenate(mixed_chunks, axis=0) if len(mixed_chunks) > 1 else mixed_chunks[0]
    ob_ref[...] = (zu * mixed * _silu(zbg)).astype(BF16)

    cqkv_ref[...] = proj(_OFF_CQKV, 3 * C_WIDTH).astype(BF16)
    scg_ref[...] = _silu(proj(_OFF_CG, C_WIDTH)).astype(BF16)
    ab_ref[...] = proj(_OFF_AB, LANES)[:, 0:2 * N_GATES]


def _projection(x, mod, gpre, w1, seg, lng, lnb, sw, sbias, rope_tabs, layer, *, mod_row0, per_batch, keep_f32_kv,
                tm):
    b, t, d = x.shape
    rope = rope_tabs is not None
    const2 = lambda bi, i: (0, 0)
    in_specs = [
        pl.BlockSpec((None, tm, d), lambda bi, i: (bi, i, 0)),
        _mod_spec(mod, layer, mod_row0, per_batch),
        _layer_spec(gpre, layer),
        _layer_spec(w1, layer),
        pl.BlockSpec(seg.shape, const2),
        _layer_spec(lng, layer),
        _layer_spec(lnb, layer),
        _layer_spec(sw, layer),
        _layer_spec(sbias, layer),
    ]
    args = [x, mod, gpre, w1, seg, lng, lnb, sw, sbias]
    if rope:
        in_specs += [pl.BlockSpec((tm, LANES), lambda bi, i: (i, 0))] * 3
        args += list(rope_tabs)

    def tok(width, dtype):
        return (pl.BlockSpec((None, tm, width), lambda bi, i: (bi, i, 0)),
                jax.ShapeDtypeStruct((b, t, width), dtype))

    outs = [tok(A_WIDTH, BF16), tok(2 * A_KV_WIDTH, BF16), tok(A_WIDTH, BF16), tok(B_WIDTH, BF16),
            tok(3 * C_WIDTH, BF16), tok(C_WIDTH, BF16), tok(2 * N_GATES, F32)]
    if keep_f32_kv:
        outs += [tok(A_KV_WIDTH, F32), tok(A_KV_WIDTH, F32)]
    return pl.pallas_call(
        functools.partial(_proj_body, rope=rope, keep_f32_kv=keep_f32_kv, tm=tm),
        grid=(b, t // tm),
        in_specs=in_specs,
        out_specs=[o[0] for o in outs],
        out_shape=[o[1] for o in outs],
        compiler_params=pltpu.CompilerParams(dimension_semantics=("arbitrary", "arbitrary"),
                                             vmem_limit_bytes=VMEM_LIMIT_BYTES),
        name="projection_rope" if rope else "projection",
    )(*args)


def _tri_masks(n, reps):
    rows = lax.broadcasted_iota(jnp.int32, (n, reps * n), 0)
    cols = lax.broadcasted_iota(jnp.int32, (n, reps * n), 1) & (n - 1)
    masks = []
    level = 0
    while (1 << level) < n:
        same_pair = (rows >> (level + 1)) == (cols >> (level + 1))
        other_half = (rows >> level) != (cols >> level)
        masks.append(same_pair & other_half)
        level += 1
    return rows, cols, rows == cols, masks


def _gdn_body(*refs, t, has_s0, emit_state, ach):
    it = iter(refs)
    cqkv_ref, ab_ref, scg_ref, convw_ref, alog_ref, dtb_ref, normg_ref, seg_ref, expand_ref = (
        next(it) for _ in range(9))
    s0_ref = next(it) if has_s0 else None
    oc_ref = next(it)
    sfin_ref = next(it) if emit_state else None
    wq_scr, u_scr, aqk_scr, kd_scr, egl_scr, s_scr, win_scr = (next(it) for _ in range(7))

    c = GDN_CHUNK
    hd = C_HEAD_DIM
    pw = 2 * hd
    n_pairs = C_HEADS // 2
    nch = t // c
    span = ach * c
    halo = BF16_SUBLANES
    rows, cols, _, _ = _tri_masks(c, 1)
    tri_lu = jnp.concatenate([jnp.where(rows >= cols, 1.0, 0.0), jnp.where(rows <= cols, 1.0, 0.0)],
                             axis=0).astype(BF16)
    prow_, pcol_, eye_pw, level_pw = _tri_masks(c, 2)
    incl_pw = (prow_ >= pcol_, prow_ <= pcol_)
    strict_pw = (prow_ > pcol_, prow_ < pcol_)
    gate_lane = lax.broadcasted_iota(jnp.int32, (1, N_GATES), 1)
    is_fwd = gate_lane < C_HEADS
    seg = seg_ref[...]
    convw = convw_ref[...]
    neg_rate = -jnp.exp(alog_ref[...])
    dtb = dtb_ref[...]

    lane_pw = lax.broadcasted_iota(jnp.int32, (1, pw), 1)
    first_lanes = lane_pw < hd

    def block_diag(xb):
        zero = jnp.zeros_like(xb)
        return jnp.concatenate([jnp.where(first_lanes, xb, zero), jnp.where(first_lanes, zero, xb)], axis=0)

    expand = expand_ref[...]

    def a_step(j, first, last):
        zeros_halo = jnp.zeros((halo, 3 * C_WIDTH), F32)
        if first:
            r0 = 0
            win_scr[0:halo, :] = zeros_halo
            n_in = span + (0 if last else halo)
            win_scr[halo:halo + n_in, :] = cqkv_ref[0:n_in, :].astype(F32)
            if last:
                win_scr[halo + span:2 * halo + span, :] = zeros_halo
        elif last:
            r0 = t - span
            win_scr[0:halo + span, :] = cqkv_ref[t - span - halo:t, :].astype(F32)
            win_scr[halo + span:2 * halo + span, :] = zeros_halo
        else:
            r0 = pl.multiple_of(j * span, span)
            win_scr[...] = cqkv_ref[pl.ds(r0 - halo, span + 2 * halo), :].astype(F32)
        y = win_scr[halo:halo + span, :] * convw[CONV_K // 2:CONV_K // 2 + 1, :]
        for tap in range(CONV_K):
            if tap != CONV_K // 2:
                lo = halo - CONV_K // 2 + tap
                y = y + win_scr[lo:lo + span, :] * convw[tap:tap + 1, :]
        y = _silu(y)
        q_all = y[:, 0:C_WIDTH]
        k_all = y[:, C_WIDTH:2 * C_WIDTH]
        v_all = y[:, 2 * C_WIDTH:3 * C_WIDTH]
        sumsq = _seg_sum(jnp.concatenate([q_all * q_all, k_all * k_all], axis=0), seg, terms=2)
        q_all = q_all * lax.rsqrt(sumsq[0:span] + EPS) * (hd ** -0.5)
        k_all = k_all * lax.rsqrt(sumsq[span:2 * span] + EPS)

        ab = ab_ref[pl.ds(r0, span), :]
        g_all = neg_rate * _softplus(ab[:, 0:N_GATES] + dtb)
        beta_all = _sigmoid(ab[:, N_GATES:2 * N_GATES])

        grow, stacked = [], []
        for ci in range(ach):
            rs = slice(ci * c, (ci + 1) * c)
            gc_fb = _tri_sum(tri_lu, g_all[rs])
            gcc = jnp.where(is_fwd, gc_fb[0:c], gc_fb[c:2 * c])
            gl = jnp.where(is_fwd, gc_fb[c - 1:c, :], gc_fb[c:c + 1, :])
            grow.append(gcc.T)
            stacked += [gcc, jnp.broadcast_to(gl, (8, N_GATES)), beta_all[rs]]
        hi, mid, lo = _split3(jnp.concatenate(stacked, axis=0))
        ex = _dot(hi, expand) + _dot(mid, expand) + _dot(lo, expand)
        per_chunk = 2 * c + 8
        gc_x, beta_x, e_x, begc_x, eglgc_x, egl_x = [], [], [], [], [], []
        for ci in range(ach):
            base = ci * per_chunk
            gx = ex[base:base + c, :]
            glx = ex[base + c:base + c + 1, :]
            bx = ex[base + c + 8:base + per_chunk, :]
            ee = jnp.exp(gx)
            gc_x.append(gx)
            beta_x.append(bx)
            e_x.append(ee)
            begc_x.append(bx * ee)
            eglgc_x.append(jnp.exp(glx - gx))
            egl_x.append(jnp.exp(glx))

        cps = [(ci, p) for ci in range(ach) for p in range(n_pairs)]
        chains = [(ci, d, p) for ci in range(ach) for d in range(N_DIRS) for p in range(n_pairs)]

        def pslice(arr, ci, p):
            return arr[ci * c:(ci + 1) * c, p * pw:(p + 1) * pw]

        def xs(vals, ci, d, p):
            lo = d * C_WIDTH + p * pw
            return vals[ci][:, lo:lo + pw]

        kp16 = {key: pslice(k_all, *key).astype(BF16) for key in cps}
        k_bd = {key: block_diag(kp16[key]) for key in cps}
        kq = {key: _dot_nt(jnp.concatenate([kp16[key], pslice(q_all, *key).astype(BF16)], axis=0), k_bd[key])
              for key in cps}
        kk = {key: kq[key][0:c] for key in cps}
        qk = {key: kq[key][c:2 * c] for key in cps}

        decay, a_mat = {}, {}
        for ci, d, p in chains:
            g0 = d * C_HEADS + 2 * p
            grow_pair = jnp.concatenate([grow[ci][g0:g0 + 1, :], grow[ci][g0 + 1:g0 + 2, :]], axis=1)
            dec = jnp.where(incl_pw[d], jnp.exp(xs(gc_x, ci, d, p) - grow_pair), 0.0)
            decay[ci, d, p] = dec
            a_mat[ci, d, p] = jnp.where(strict_pw[d], kk[ci, p] * xs(beta_x, ci, d, p) * dec, 0.0)

        t_mat = {ch: jnp.where(eye_pw, 1.0, 0.0) - jnp.where(level_pw[0], a_mat[ch], 0.0) for ch in chains}
        for mask in level_pw[1:]:
            tb = {ch: t_mat[ch].astype(BF16) for ch in chains}
            t_bd = {ch: block_diag(tb[ch]) for ch in chains}
            et = {ch: _dot(jnp.where(mask, a_mat[ch], 0.0).astype(BF16), t_bd[ch]).astype(BF16) for ch in chains}
            t_mat = {ch: t_mat[ch] - _dot(tb[ch], block_diag(et[ch])) for ch in chains}

        uw = {}
        for ci, d, p in chains:
            vb = (pslice(v_all, ci, p) * xs(beta_x, ci, d, p)).astype(BF16)
            kbe = (pslice(k_all, ci, p) * xs(begc_x, ci, d, p)).astype(BF16)
            rhs = jnp.concatenate([block_diag(vb), block_diag(kbe)], axis=1)
            uw[ci, d, p] = _dot(t_mat[ci, d, p].astype(BF16), rhs)

        for ci in range(ach):
            m = (r0 // c) + ci
            q_c = q_all[ci * c:(ci + 1) * c, :]
            k_c = k_all[ci * c:(ci + 1) * c, :]
            for d in range(N_DIRS):
                dl = slice(d * C_WIDTH, (d + 1) * C_WIDTH)
                wq_scr[m, d, c:2 * c, :] = (q_c * e_x[ci][:, dl]).astype(BF16)
                kd_scr[m, d] = (k_c * eglgc_x[ci][:, dl]).astype(BF16)
                egl_scr[m, d] = egl_x[ci][:, dl]
                for p in range(n_pairs):
                    pl_ = slice(p * pw, (p + 1) * pw)
                    u_scr[m, d, :, pl_] = uw[ci, d, p][:, 0:pw].astype(BF16)
                    wq_scr[m, d, 0:c, pl_] = uw[ci, d, p][:, pw:2 * pw].astype(BF16)
                    aqk_scr[m, d, :, pl_] = (qk[ci, p] * decay[ci, d, p]).astype(BF16)

    n_steps = nch // ach
    if n_steps == 1:
        a_step(0, True, True)
    else:
        a_step(0, True, False)
        if n_steps > 2:
            def a_loop(j, carry):
                a_step(j, False, False)
                return carry
            lax.fori_loop(1, n_steps - 1, a_loop, 0)
        a_step(n_steps - 1, False, True)

    zeros_h = jnp.zeros((hd, hd), F32)
    for d in range(N_DIRS):
        for p in range(n_pairs):
            if has_s0:
                top = jnp.concatenate([s0_ref[d, 2 * p], zeros_h], axis=1)
                bot = jnp.concatenate([zeros_h, s0_ref[d, 2 * p + 1]], axis=1)
                s_scr[d, p] = jnp.concatenate([top, bot], axis=0)
            else:
                s_scr[d, p] = jnp.zeros((pw, pw), F32)

    prow = lax.broadcasted_iota(jnp.int32, (pw, pw), 0)
    pcol = lax.broadcasted_iota(jnp.int32, (pw, pw), 1)
    same_head = (prow >= hd) == (pcol >= hd)
    first_head_lanes = lax.broadcasted_iota(jnp.int32, (1, pw), 1) < hd
    pairs = [(d, p) for d in range(N_DIRS) for p in range(n_pairs)]

    def b_step(n, carry):
        ms = (n, nch - 1 - n)
        ls = {(d, p): slice(p * pw, (p + 1) * pw) for d, p in pairs}
        state = {(d, p): s_scr[d, p] for d, p in pairs}
        r = {(d, p): _dot(wq_scr[ms[d], d, :, ls[d, p]], state[d, p].astype(BF16)) for d, p in pairs}
        vnb = {}
        for d, p in pairs:
            v_new = u_scr[ms[d], d, :, ls[d, p]].astype(F32) - r[d, p][0:c, :]
            vnb[d, p] = v_new.astype(BF16)
        for d, p in pairs:
            zero = jnp.zeros_like(vnb[d, p])
            vn_bd = jnp.concatenate([jnp.where(first_head_lanes, vnb[d, p], zero),
                                     jnp.where(first_head_lanes, zero, vnb[d, p])], axis=0)
            o = r[d, p][c:2 * c, :] + _dot(aqk_scr[ms[d], d, :, ls[d, p]], vn_bd)
            kv = _dot_tn(kd_scr[ms[d], d, :, ls[d, p]], vnb[d, p])
            s_scr[d, p] = state[d, p] * egl_scr[ms[d], d, :, ls[d, p]] + jnp.where(same_head, kv, 0.0)
            u_scr[ms[d], d, :, ls[d, p]] = o.astype(BF16)
        return carry

    lax.fori_loop(0, nch, b_step, 0)

    inv_n = 1.0 / hd

    c_chunks = 4 if nch % 4 == 0 else 1

    def c_step(i, carry):
        m0 = i * c_chunks
        o = jnp.concatenate([u_scr[m0 + ci, 0].astype(F32) + u_scr[m0 + ci, 1].astype(F32)
                             for ci in range(c_chunks)], axis=0)
        r0 = pl.multiple_of(m0 * c, c_chunks * c)
        ms = _seg_sum(o * o, seg, terms=2) * inv_n
        on = o * lax.rsqrt(ms + EPS) * normg_ref[...]
        oc_ref[pl.ds(r0, c_chunks * c), :] = (on * scg_ref[pl.ds(r0, c_chunks * c), :].astype(F32)).astype(BF16)
        return carry

    lax.fori_loop(0, nch // c_chunks, c_step, 0)

    if emit_state:
        for d in range(N_DIRS):
            for p in range(n_pairs):
                s_pair = s_scr[d, p]
                sfin_ref[d, 2 * p] = s_pair[0:hd, 0:hd]
                sfin_ref[d, 2 * p + 1] = s_pair[hd:pw, hd:pw]


def _gdn(cqkv, ab, scg, convw, alog, dtb, normg, seg, expand, s0, layer, *, emit_state, ach):
    b, t, _ = cqkv.shape
    nch = t // GDN_CHUNK
    assert nch % ach == 0
    has_s0 = s0 is not None
    const2 = lambda bi: (0, 0)
    in_specs = [
        pl.BlockSpec((None, t, 3 * C_WIDTH), lambda bi: (bi, 0, 0)),
        pl.BlockSpec((None, t, 2 * N_GATES), lambda bi: (bi, 0, 0)),
        pl.BlockSpec((None, t, C_WIDTH), lambda bi: (bi, 0, 0)),
        _layer_spec(convw, layer),
        _layer_spec(alog, layer),
        _layer_spec(dtb, layer),
        _layer_spec(normg, layer),
        pl.BlockSpec(seg.shape, const2),
        pl.BlockSpec(expand.shape, const2),
    ]
    args = [cqkv, ab, scg, convw, alog, dtb, normg, seg, expand]
    state_spec = pl.BlockSpec((None, N_DIRS, C_HEADS, C_HEAD_DIM, C_HEAD_DIM), lambda bi: (bi, 0, 0, 0, 0))
    if has_s0:
        in_specs.append(pl.BlockSpec((None, None, N_DIRS, C_HEADS, C_HEAD_DIM, C_HEAD_DIM),
                                     lambda bi: (bi, layer, 0, 0, 0, 0)))
        args.append(s0)
    out_specs = [pl.BlockSpec((None, t, C_WIDTH), lambda bi: (bi, 0, 0))]
    out_shape = [jax.ShapeDtypeStruct((b, t, C_WIDTH), BF16)]
    if emit_state:
        out_specs.append(state_spec)
        out_shape.append(jax.ShapeDtypeStruct((b, N_DIRS, C_HEADS, C_HEAD_DIM, C_HEAD_DIM), F32))
    pair_w = 2 * C_HEAD_DIM
    scratch = [
        pltpu.VMEM((nch, N_DIRS, 2 * GDN_CHUNK, C_WIDTH), BF16),
        pltpu.VMEM((nch, N_DIRS, GDN_CHUNK, C_WIDTH), BF16),
        pltpu.VMEM((nch, N_DIRS, GDN_CHUNK, C_WIDTH), BF16),
        pltpu.VMEM((nch, N_DIRS, GDN_CHUNK, C_WIDTH), BF16),
        pltpu.VMEM((nch, N_DIRS, 1, C_WIDTH), F32),
        pltpu.VMEM((N_DIRS, C_HEADS // 2, pair_w, pair_w), F32),
        pltpu.VMEM((ach * GDN_CHUNK + 2 * BF16_SUBLANES, 3 * C_WIDTH), F32),
    ]
    return pl.pallas_call(
        functools.partial(_gdn_body, t=t, has_s0=has_s0, emit_state=emit_state, ach=ach),
        grid=(b,),
        in_specs=in_specs,
        out_specs=out_specs,
        out_shape=out_shape,
        scratch_shapes=scratch,
        compiler_params=pltpu.CompilerParams(dimension_semantics=("arbitrary",),
                                             vmem_limit_bytes=VMEM_LIMIT_BYTES),
        name="gdn_state" if emit_state else "gdn",
    )(*args)


def _attn_body(*refs, banded, nseq, tq, nblk_total):
    it = iter(refs)
    x_ref, q_ref, kv_ref = (next(it) for _ in range(3))
    if banded:
        kvl_ref, kvr_ref, band_ref = (next(it) for _ in range(3))
        ck_ref, cv_ref = (next(it) for _ in range(2))
    sink_ref, sag_ref, ob_ref, oc_ref, wout_ref, gpost_ref, mod_ref, y_ref = (next(it) for _ in range(8))

    nb = tq // ATTN_BLOCK
    rows_q = A_GROUP * ATTN_BLOCK
    lane = lax.broadcasted_iota(jnp.int32, (1, A_KV_WIDTH), 1)
    head_mask = [lane < HEAD_DIM, lane >= HEAD_DIM]
    sum_lane = [HEAD_DIM, 0]

    def split_heads(a, spare=0.0):
        fill = [jnp.where(lane == sum_lane[h], spare, 0.0).astype(a.dtype) for h in range(A_KV_HEADS)]
        return [jnp.where(head_mask[h], a, fill[h]) for h in range(A_KV_HEADS)]

    if banded:
        kv_ext = jnp.concatenate([kvl_ref[0], kv_ref[0], kvr_ref[0]], axis=0)
        k_own = split_heads(kv_ext[:, 0:A_KV_WIDTH])
        v_own = split_heads(kv_ext[:, A_KV_WIDTH:2 * A_KV_WIDTH], spare=1.0)
        k_ctx = [split_heads(ck_ref[...].astype(BF16))]
        v_ctx = [split_heads(cv_ref[...].astype(BF16), spare=1.0)]
        first_blk = pl.program_id(1) * nb
        vis_l, vis_r = [], []
        for j in range(nb):
            thr_l = jnp.where(first_blk + j > 0, 0.5, 2.0)
            thr_r = jnp.where(first_blk + j < nblk_total - 1, 0.5, 2.0)
            vis_l.append(band_ref[:, 0:ATTN_BLOCK] > thr_l)
            vis_r.append(band_ref[:, 2 * ATTN_BLOCK:3 * ATTN_BLOCK] > thr_r)
    else:
        k_ctx = [split_heads(kv_ref[s, :, 0:A_KV_WIDTH]) for s in range(nseq)]
        v_ctx = [split_heads(kv_ref[s, :, A_KV_WIDTH:2 * A_KV_WIDTH], spare=1.0) for s in range(nseq)]

    units = [(s, j, h) for s in range(nseq) for j in range(nb) for h in range(A_KV_HEADS)]
    q2 = {(s, j): jnp.concatenate([q_ref[s, j * ATTN_BLOCK:(j + 1) * ATTN_BLOCK, g * A_KV_WIDTH:(g + 1) * A_KV_WIDTH]
                                   for g in range(A_GROUP)], axis=0) for s in range(nseq) for j in range(nb)}

    def scores(s, j, h):
        s_ctx = _dot_nt(q2[s, j], k_ctx[s][h])
        if not banded:
            return s_ctx, None
        s_own = _dot_nt(q2[s, j], k_own[h][j * ATTN_BLOCK:(j + 3) * ATTN_BLOCK])
        left = jnp.where(vis_l[j], s_own[:, 0:ATTN_BLOCK], NEG_INF)
        right = jnp.where(vis_r[j], s_own[:, 2 * ATTN_BLOCK:3 * ATTN_BLOCK], NEG_INF)
        return s_ctx, jnp.concatenate([left, s_own[:, ATTN_BLOCK:2 * ATTN_BLOCK], right], axis=1)

    o_acc = {(s, j): jnp.zeros((rows_q, A_KV_WIDTH), F32) for s in range(nseq) for j in range(nb)}
    pending = scores(*units[0])
    for ui, (s, j, h) in enumerate(units):
        s_ctx, s_own = pending
        if ui + 1 < len(units):
            pending = scores(*units[ui + 1])
        sink = sink_ref[h] * LOG2_E
        s_all = jnp.concatenate([s_ctx, s_own], axis=1) if banded else s_ctx
        m = jnp.maximum(jnp.max(s_all, axis=-1, keepdims=True), sink)
        pb = jnp.exp2(s_all - m).astype(BF16)
        n_ctx = s_ctx.shape[1]
        acc = _dot(pb[:, 0:n_ctx], v_ctx[s][h])
        if banded:
            acc = acc + _dot(pb[:, n_ctx:], v_own[h][j * ATTN_BLOCK:(j + 3) * ATTN_BLOCK])
        denom = acc[:, sum_lane[h]:sum_lane[h] + 1] + jnp.exp2(sink - m)
        o_acc[s, j] = o_acc[s, j] + jnp.where(head_mask[h], acc, 0.0) * (1.0 / denom)
    o_blocks = [jnp.concatenate([o_acc[s, j][g * ATTN_BLOCK:(g + 1) * ATTN_BLOCK] for g in range(A_GROUP)], axis=1)
                for s in range(nseq) for j in range(nb)]
    o_a = jnp.concatenate(o_blocks, axis=0) if len(o_blocks) > 1 else o_blocks[0]

    def rows_of(ref):
        return jnp.concatenate([ref[s] for s in range(nseq)], axis=0) if nseq > 1 else ref[0]

    mix_a = (o_a * rows_of(sag_ref).astype(F32)).astype(BF16)
    mix = (_dot(mix_a, wout_ref[0:A_WIDTH, :])
           + _dot(rows_of(ob_ref), wout_ref[A_WIDTH:A_WIDTH + B_WIDTH, :])
           + _dot(rows_of(oc_ref), wout_ref[A_WIDTH + B_WIDTH:A_WIDTH + B_WIDTH + C_WIDTH, :]))
    ms = jnp.mean(mix * mix, axis=-1, keepdims=True)
    y = rows_of(x_ref) + mod_ref[2:3, :] * (mix * lax.rsqrt(ms + EPS) * gpost_ref[...])
    for s in range(nseq):
        y_ref[s] = y[s * tq:(s + 1) * tq]


def _attention_out(x, q, kv, sag, ob, oc, sink_rows, wout, gpost, mod, band, ctx_k, ctx_v, layer, *, mod_row0,
                   per_batch, tq, nseq):
    b, t, d = x.shape
    banded = band is not None
    assert b % nseq == 0 and (nseq == 1 or not per_batch)
    nblk_total = t // ATTN_BLOCK
    per_q = tq // ATTN_BLOCK
    const2 = lambda bi, i: (0, 0)

    def tok(width):
        return pl.BlockSpec((nseq, tq, width), lambda bi, i: (bi, i, 0))

    in_specs = [tok(d), tok(A_WIDTH)]
    args = [x, q]
    if banded:
        in_specs.append(tok(2 * A_KV_WIDTH))
        args.append(kv)
        in_specs.append(pl.BlockSpec((1, ATTN_BLOCK, 2 * A_KV_WIDTH),
                                     lambda bi, i: (bi, jnp.maximum(i * per_q - 1, 0), 0)))
        in_specs.append(pl.BlockSpec((1, ATTN_BLOCK, 2 * A_KV_WIDTH),
                                     lambda bi, i: (bi, jnp.minimum((i + 1) * per_q, nblk_total - 1), 0)))
        in_specs.append(pl.BlockSpec(band.shape, const2))
        args += [kv, kv, band]
        past = ctx_k.shape[2]
        in_specs += [pl.BlockSpec((None, None, past, A_KV_WIDTH), lambda bi, i: (bi, layer, 0, 0))] * 2
        args += [ctx_k, ctx_v]
    else:
        in_specs.append(pl.BlockSpec((nseq, t, 2 * A_KV_WIDTH), lambda bi, i: (bi, 0, 0)))
        args.append(kv)
    in_specs += [
        _layer_spec(sink_rows, layer),
        tok(A_WIDTH), tok(B_WIDTH), tok(C_WIDTH),
        _layer_spec(wout, layer),
        _layer_spec(gpost, layer),
        _mod_spec(mod, layer, mod_row0, per_batch),
    ]
    args += [sink_rows, sag, ob, oc, wout, gpost, mod]
    return pl.pallas_call(
        functools.partial(_attn_body, banded=banded, nseq=nseq, tq=tq, nblk_total=nblk_total),
        grid=(b // nseq, t // tq),
        in_specs=in_specs,
        out_specs=tok(d),
        out_shape=jax.ShapeDtypeStruct((b, t, d), F32),
        compiler_params=pltpu.CompilerParams(dimension_semantics=("arbitrary", "arbitrary"),
                                             vmem_limit_bytes=VMEM_LIMIT_BYTES),
        name="attention_banded" if banded else "attention_full",
    )(*args)


def _block_ones(width, block):
    idx = np.arange(width) // block
    return jnp.asarray((idx[:, None] == idx[None, :]).astype(np.float32), dtype=BF16)


def _gate_expansion():
    lane_gate = np.arange(N_GATES * C_HEAD_DIM) // C_HEAD_DIM
    return jnp.asarray((np.arange(N_GATES)[:, None] == lane_gate[None, :]).astype(np.float32), dtype=BF16)


def _rope_tables(n):
    half = HEAD_DIM // 2
    nf = half // 2
    pos = np.arange(n)
    row = (pos // GRID_W).astype(np.float32)
    colp = (pos % GRID_W).astype(np.float32)
    inv_freq = np.float32(ROPE_BASE) ** (-np.arange(nf, dtype=np.float32) / np.float32(nf))
    lane = np.arange(LANES) % HEAD_DIM
    freq_idx = lane % nf
    quarter = lane // nf
    p = np.where((quarter < 2)[None, :], row[:, None], colp[:, None])
    ang = (p * inv_freq[freq_idx][None, :]).astype(np.float32)
    cos, sin = np.cos(ang), np.sin(ang)
    first = (quarter % 2 == 0)[None, :]
    sa = np.where(first, -sin, 0.0)
    sb = np.where(first, 0.0, sin)
    return tuple(jnp.asarray(a, dtype=F32) for a in (cos, sa, sb))


def _band_table():
    q_rel = np.arange(ATTN_BLOCK)
    k_rel = np.arange(3 * ATTN_BLOCK) - ATTN_BLOCK
    band = (np.abs(q_rel[:, None] - k_rel[None, :]) <= ATTN_BLOCK).astype(np.float32)
    return jnp.asarray(np.tile(band, (A_GROUP, 1)))


def _group_major(a, axis):
    shape = a.shape
    split = shape[:axis] + (A_KV_HEADS, A_GROUP, HEAD_DIM) + shape[axis + 1:]
    return jnp.swapaxes(a.reshape(split), axis, axis + 1).reshape(shape)


def _layout_w_in(w_in):
    o = 0
    sec = {}
    for name, width in (("aq", A_WIDTH), ("ak", A_KV_WIDTH), ("av", A_KV_WIDTH), ("ag", A_WIDTH),
                        ("bu", B_WIDTH), ("bv", B_WIDTH), ("bg", B_WIDTH),
                        ("cq", C_WIDTH), ("ck", C_WIDTH), ("cv", C_WIDTH), ("ca", N_GATES), ("cb", N_GATES),
                        ("cg", C_WIDTH)):
        sec[name] = w_in[:, :, o:o + width].astype(BF16)
        o += width
    pad = jnp.zeros(w_in.shape[:2] + (LANES - 2 * N_GATES,), BF16)
    cols = [_group_major(sec["aq"], 2), sec["ak"], sec["av"], _group_major(sec["ag"], 2), sec["bu"], sec["bv"],
            sec["bg"], sec["cq"], sec["ck"], sec["cv"], sec["cg"], sec["ca"], sec["cb"], pad]
    return jnp.concatenate(cols, axis=2)


def _row_tile(t, target):
    tile = min(t, target)
    while t % tile:
        tile -= ATTN_BLOCK
    return tile


def kernel(x_prompt, x_sample, cache_k, cache_v, state_delta, c, c_ctx, w_mod, b_mod, g_pre, w_in, attn_sink,
           sgu_ln_g, sgu_ln_b, sgu_w, sgu_b, gdn_conv_w, gdn_a_log, gdn_dt_bias, gdn_norm_g, g_post, w_out):
    depth, d, _ = w_mod.shape
    b_ctx, t_ctx, _ = x_prompt.shape
    b_lat, t_lat, _ = x_sample.shape
    past = cache_k.shape[2]
    assert t_ctx % SGU_CHUNK == 0 and t_lat % SGU_CHUNK == 0 and t_lat % GRID_W == 0
    assert t_ctx // GDN_CHUNK >= 2 and t_lat // GDN_CHUNK >= 2
    assert w_in.shape[2] == 2 * A_WIDTH + 2 * A_KV_WIDTH + 3 * B_WIDTH + 4 * C_WIDTH + 2 * N_GATES

    n_cond = b_lat + 1
    rows = -(-n_cond // 8) * 8
    conds = jnp.concatenate([c, c_ctx[None, :], jnp.zeros((rows - n_cond, d), c.dtype)], axis=0)
    mod = _modulation(conds, w_mod, b_mod).reshape(depth, rows, 3, d)

    seg_b = _block_ones(B_WIDTH, B_GROUP_DIM)
    seg_c = _block_ones(C_WIDTH, C_HEAD_DIM)
    expand_c = _gate_expansion()
    rope_tabs = _rope_tables(t_lat)
    band = _band_table()
    ck = cache_k.reshape(b_lat, depth, past, A_KV_WIDTH)
    cv = cache_v.reshape(b_lat, depth, past, A_KV_WIDTH)
    tm_ctx, tm_lat = _row_tile(b_ctx * t_ctx, PROJ_ROWS_CTX), _row_tile(t_lat, PROJ_ROWS_LAT)

    w1 = _layout_w_in(w_in)
    wout_b = w_out.astype(BF16)
    wout = jnp.concatenate([_group_major(wout_b[:, 0:A_WIDTH], 1), wout_b[:, A_WIDTH:]], axis=1)
    gpre = g_pre.reshape(depth, 1, d)
    gpost = g_post.reshape(depth, 1, d)
    lng = sgu_ln_g.reshape(depth, 1, B_WIDTH)
    lnb = sgu_ln_b.reshape(depth, 1, B_WIDTH)
    sw = sgu_w.astype(BF16)
    sbias = jnp.repeat(jnp.swapaxes(sgu_b, 1, 2), B_GROUP_DIM, axis=2)
    sink_rows = jnp.repeat(attn_sink.reshape(depth, A_KV_HEADS, A_GROUP), ATTN_BLOCK, axis=2)[..., None]
    alog = gdn_a_log.reshape(depth, 1, N_GATES)
    dtb = gdn_dt_bias.reshape(depth, 1, N_GATES)
    normg = jnp.tile(gdn_norm_g, (1, C_HEADS)).reshape(depth, 1, C_WIDTH)
    ctx_nseq = max(1, min(b_ctx, ATTN_ROWS_CTX // t_ctx))
    while b_ctx % ctx_nseq:
        ctx_nseq -= 1

    xp, xs = x_prompt, x_sample
    new_k, new_v, new_s = [], [], []
    for l in range(depth):
        flat = _projection(xp.reshape(1, b_ctx * t_ctx, d), mod, gpre, w1, seg_b, lng, lnb, sw, sbias, None, l,
                           mod_row0=b_lat, per_batch=False, keep_f32_kv=True, tm=tm_ctx)
        q, kv, sag, ob, cqkv, scg, ab, kf, vf = (a.reshape(b_ctx, t_ctx, a.shape[-1]) for a in flat)
        oc, s_fin = _gdn(cqkv, ab, scg, gdn_conv_w, alog, dtb, normg, seg_c, expand_c, None, l, emit_state=True,
                         ach=min(GDN_A_CHUNKS, t_ctx // GDN_CHUNK))
        xp = _attention_out(xp, q, kv, sag, ob, oc, sink_rows, wout, gpost, mod, None, None, None, l,
                            mod_row0=b_lat, per_batch=False, tq=t_ctx, nseq=ctx_nseq)
        new_k.append(kf.reshape(b_ctx, t_ctx, A_KV_HEADS, HEAD_DIM))
        new_v.append(vf.reshape(b_ctx, t_ctx, A_KV_HEADS, HEAD_DIM))
        new_s.append(s_fin)

        q, kv, sag, ob, cqkv, scg, ab = _projection(
            xs, mod, gpre, w1, seg_b, lng, lnb, sw, sbias, rope_tabs, l, mod_row0=0, per_batch=True,
            keep_f32_kv=False, tm=tm_lat)
        (oc,) = _gdn(cqkv, ab, scg, gdn_conv_w, alog, dtb, normg, seg_c, expand_c, state_delta, l, emit_state=False,
                     ach=min(GDN_A_CHUNKS, t_lat // GDN_CHUNK))
        xs = _attention_out(xs, q, kv, sag, ob, oc, sink_rows, wout, gpost, mod, band, ck, cv, l,
                            mod_row0=0, per_batch=True, tq=_row_tile(t_lat, ATTN_ROWS_LAT), nseq=1)

    return (xp, xs, jnp.stack(new_k, axis=1), jnp.stack(new_v, axis=1), jnp.stack(new_s, axis=1))
```

```python
import functools

import numpy as np
import jax
import jax.numpy as jnp
from jax import lax
from jax.experimental import pallas as pl
from jax.experimental.pallas import tpu as pltpu

F32 = jnp.float32
BF16 = jnp.bfloat16

GRID_W = 64
HEAD_DIM = 64
A_HEADS = 8
A_KV_HEADS = 2
A_GROUP = A_HEADS // A_KV_HEADS
A_WIDTH = A_HEADS * HEAD_DIM
A_KV_WIDTH = A_KV_HEADS * HEAD_DIM
ATTN_BLOCK = 128
ROPE_BASE = 10000.0
B_GROUPS = 4
B_GROUP_DIM = 64
B_WIDTH = B_GROUPS * B_GROUP_DIM
SGU_CHUNK = 128
C_HEADS = 4
C_HEAD_DIM = 64
C_WIDTH = C_HEADS * C_HEAD_DIM
CONV_K = 5
GDN_CHUNK = 64
N_DIRS = 2
N_GATES = N_DIRS * C_HEADS
EPS = 1e-6
NEG_INF = -1e30
LOG2_E = 1.4426950408889634

LANES = 128
BF16_SUBLANES = 16
VMEM_LIMIT_BYTES = 56 * 1024 * 1024

GDN_A_CHUNKS = 4

PROJ_ROWS_CTX = 512
PROJ_ROWS_LAT = 1024
ATTN_ROWS_CTX = 1024
ATTN_ROWS_LAT = 1024

_OFF_Q = 0
_OFF_K = _OFF_Q + A_WIDTH
_OFF_V = _OFF_K + A_KV_WIDTH
_OFF_AG = _OFF_V + A_KV_WIDTH
_OFF_BU = _OFF_AG + A_WIDTH
_OFF_BV = _OFF_BU + B_WIDTH
_OFF_BG = _OFF_BV + B_WIDTH
_OFF_CQKV = _OFF_BG + B_WIDTH
_OFF_CG = _OFF_CQKV + 3 * C_WIDTH
_OFF_AB = _OFF_CG + C_WIDTH
_W1_COLS = _OFF_AB + LANES

def _dot(a, b):
    return jnp.dot(a, b, preferred_element_type=F32)


def _layer_spec(arr, layer):
    zeros = (0,) * (arr.ndim - 1)
    return pl.BlockSpec((None,) + tuple(arr.shape[1:]), lambda *_: (layer,) + zeros)


def _mod_spec(mod, layer, row0, per_batch):
    return pl.BlockSpec((None, None) + tuple(mod.shape[2:]),
                        lambda bi, i: (layer, row0 + (bi if per_batch else 0), 0, 0))


def _dot_nt(a, b):
    return lax.dot_general(a, b, (((1,), (1,)), ((), ())), preferred_element_type=F32)


def _dot_tn(a, b):
    return lax.dot_general(a, b, (((0,), (0,)), ((), ())), preferred_element_type=F32)


def _split3(x):
    hi = x.astype(BF16)
    r1 = x - hi.astype(F32)
    mid = r1.astype(BF16)
    lo = (r1 - mid.astype(F32)).astype(BF16)
    return hi, mid, lo


def _seg_sum(x, seg, terms=3):
    hi, mid, lo = _split3(x)
    out = _dot(hi, seg) + _dot(mid, seg)
    return out + _dot(lo, seg) if terms == 3 else out


def _tri_sum(tri, x):
    hi, mid, lo = _split3(x)
    return _dot(tri, hi) + _dot(tri, mid) + _dot(tri, lo)


def _sigmoid(x):
    return 1.0 / (1.0 + jnp.exp(-x))


def _silu(x):
    return x * _sigmoid(x)


def _softplus(x):
    return jnp.maximum(x, 0.0) + jnp.log1p(jnp.exp(-jnp.abs(x)))


def _mod_body(cond_ref, w_ref, b_ref, o_ref):
    a = _silu(cond_ref[...]).astype(BF16)
    o_ref[...] = _dot(a, w_ref[...].astype(BF16)) + b_ref[...]


def _modulation(conds, w_mod, b_mod):
    depth, d, d3 = w_mod.shape
    rows = conds.shape[0]
    tn = d
    return pl.pallas_call(
        _mod_body,
        grid=(depth, d3 // tn),
        in_specs=[
            pl.BlockSpec((rows, d), lambda l, j: (0, 0)),
            pl.BlockSpec((None, d, tn), lambda l, j: (l, 0, j)),
            pl.BlockSpec((None, 1, tn), lambda l, j: (l, 0, j)),
        ],
        out_specs=pl.BlockSpec((None, rows, tn), lambda l, j: (l, 0, j)),
        out_shape=jax.ShapeDtypeStruct((depth, rows, d3), F32),
        compiler_params=pltpu.CompilerParams(dimension_semantics=("arbitrary", "arbitrary"),
                                             vmem_limit_bytes=VMEM_LIMIT_BYTES),
        name="modulation",
    )(conds, w_mod, b_mod.reshape(depth, 1, d3))


def _rope(z, c, sa, sb):
    w = z.shape[1]
    reps = w // LANES
    if reps > 1:
        c = jnp.concatenate([c] * reps, axis=1)
        sa = jnp.concatenate([sa] * reps, axis=1)
        sb = jnp.concatenate([sb] * reps, axis=1)
    half_pair = HEAD_DIM // 4
    return z * c + pltpu.roll(z, w - half_pair, 1) * sa + pltpu.roll(z, half_pair, 1) * sb


def _proj_body(*refs, rope, keep_f32_kv, tm):
    it = iter(refs)
    x_ref, mod_ref, gpre_ref, w_ref, seg_ref, lng_ref, lnb_ref, sw_ref, sbias_ref = (next(it) for _ in range(9))
    if rope:
        rc_ref, rsa_ref, rsb_ref = (next(it) for _ in range(3))
    q_ref, kv_ref, sag_ref, ob_ref, cqkv_ref, scg_ref, ab_ref = (next(it) for _ in range(7))
    if keep_f32_kv:
        kf_ref, vf_ref = (next(it) for _ in range(2))

    x = x_ref[...]
    ms = jnp.mean(x * x, axis=-1, keepdims=True)
    h = x * lax.rsqrt(ms + EPS) * gpre_ref[...]
    h = h * (1.0 + mod_ref[1:2, :]) + mod_ref[0:1, :]
    hb = h.astype(BF16)

    def proj(lo, width):
        return _dot(hb, w_ref[:, lo:lo + width])

    zq = proj(_OFF_Q, A_WIDTH)
    zk = proj(_OFF_K, A_KV_WIDTH)
    zv = proj(_OFF_V, A_KV_WIDTH)
    if keep_f32_kv:
        kf_ref[...] = zk
        vf_ref[...] = zv
    if rope:
        c, sa, sb = rc_ref[...], rsa_ref[...], rsb_ref[...]
        zq = _rope(zq, c, sa, sb)
        zk = _rope(zk, c, sa, sb)
    q_ref[...] = (zq * (HEAD_DIM ** -0.5 * LOG2_E)).astype(BF16)
    kv_ref[:, 0:A_KV_WIDTH] = zk.astype(BF16)
    kv_ref[:, A_KV_WIDTH:2 * A_KV_WIDTH] = zv.astype(BF16)
    sag_ref[...] = _silu(proj(_OFF_AG, A_WIDTH)).astype(BF16)

    zu = proj(_OFF_BU, B_WIDTH)
    zbv = proj(_OFF_BV, B_WIDTH)
    zbg = proj(_OFF_BG, B_WIDTH)
    seg = seg_ref[...]
    inv_n = 1.0 / B_GROUP_DIM
    mean = _seg_sum(zbv, seg) * inv_n
    xc = zbv - mean
    var = _seg_sum(xc * xc, seg) * inv_n
    vg = (xc * lax.rsqrt(var + EPS) * lng_ref[...] + lnb_ref[...]).astype(BF16)
    sbias = sbias_ref[...]
    mixed_chunks = []
    for ci in range(tm // SGU_CHUNK):
        rows = slice(ci * SGU_CHUNK, (ci + 1) * SGU_CHUNK)
        per_group = [_dot(sw_ref[g], vg[rows, g * B_GROUP_DIM:(g + 1) * B_GROUP_DIM]) for g in range(B_GROUPS)]
        mixed_chunks.append(jnp.concatenate(per_group, axis=1) + sbias)
    mixed = jnp.concatenate(mixed_chunks, axis=0) if len(mixed_chunks) > 1 else mixed_chunks[0]
    ob_ref[...] = (zu * mixed * _silu(zbg)).astype(BF16)

    cqkv_ref[...] = proj(_OFF_CQKV, 3 * C_WIDTH).astype(BF16)
    scg_ref[...] = _silu(proj(_OFF_CG, C_WIDTH)).astype(BF16)
    ab_ref[...] = proj(_OFF_AB, LANES)[:, 0:2 * N_GATES]


def _projection(x, mod, gpre, w1, seg, lng, lnb, sw, sbias, rope_tabs, layer, *, mod_row0, per_batch, keep_f32_kv,
                tm):
    b, t, d = x.shape
    rope = rope_tabs is not None
    const2 = lambda bi, i: (0, 0)
    in_specs = [
        pl.BlockSpec((None, tm, d), lambda bi, i: (bi, i, 0)),
        _mod_spec(mod, layer, mod_row0, per_batch),
        _layer_spec(gpre, layer),
        _layer_spec(w1, layer),
        pl.BlockSpec(seg.shape, const2),
        _layer_spec(lng, layer),
        _layer_spec(lnb, layer),
        _layer_spec(sw, layer),
        _layer_spec(sbias, layer),
    ]
    args = [x, mod, gpre, w1, seg, lng, lnb, sw, sbias]
    if rope:
        in_specs += [pl.BlockSpec((tm, LANES), lambda bi, i: (i, 0))] * 3
        args += list(rope_tabs)

    def tok(width, dtype):
        return (pl.BlockSpec((None, tm, width), lambda bi, i: (bi, i, 0)),
                jax.ShapeDtypeStruct((b, t, width), dtype))

    outs = [tok(A_WIDTH, BF16), tok(2 * A_KV_WIDTH, BF16), tok(A_WIDTH, BF16), tok(B_WIDTH, BF16),
            tok(3 * C_WIDTH, BF16), tok(C_WIDTH, BF16), tok(2 * N_GATES, F32)]
    if keep_f32_kv:
        outs += [tok(A_KV_WIDTH, F32), tok(A_KV_WIDTH, F32)]
    return pl.pallas_call(
        functools.partial(_proj_body, rope=rope, keep_f32_kv=keep_f32_kv, tm=tm),
        grid=(b, t // tm),
        in_specs=in_specs,
        out_specs=[o[0] for o in outs],
        out_shape=[o[1] for o in outs],
        compiler_params=pltpu.CompilerParams(dimension_semantics=("arbitrary", "arbitrary"),
                                             vmem_limit_bytes=VMEM_LIMIT_BYTES),
        name="projection_rope" if rope else "projection",
    )(*args)


def _tri_masks(n, reps):
    rows = lax.broadcasted_iota(jnp.int32, (n, reps * n), 0)
    cols = lax.broadcasted_iota(jnp.int32, (n, reps * n), 1) & (n - 1)
    masks = []
    level = 0
    while (1 << level) < n:
        same_pair = (rows >> (level + 1)) == (cols >> (level + 1))
        other_half = (rows >> level) != (cols >> level)
        masks.append(same_pair & other_half)
        level += 1
    return rows, cols, rows == cols, masks


def _gdn_body(*refs, t, has_s0, emit_state, ach):
    it = iter(refs)
    cqkv_ref, ab_ref, scg_ref, convw_ref, alog_ref, dtb_ref, normg_ref, seg_ref, expand_ref = (
        next(it) for _ in range(9))
    s0_ref = next(it) if has_s0 else None
    oc_ref = next(it)
    sfin_ref = next(it) if emit_state else None
    wq_scr, u_scr, aqk_scr, kd_scr, egl_scr, s_scr, win_scr = (next(it) for _ in range(7))

    c = GDN_CHUNK
    hd = C_HEAD_DIM
    pw = 2 * hd
    n_pairs = C_HEADS // 2
    nch = t // c
    span = ach * c
    halo = BF16_SUBLANES
    rows, cols, _, _ = _tri_masks(c, 1)
    tri_lu = jnp.concatenate([jnp.where(rows >= cols, 1.0, 0.0), jnp.where(rows <= cols, 1.0, 0.0)],
                             axis=0).astype(BF16)
    prow_, pcol_, eye_pw, level_pw = _tri_masks(c, 2)
    incl_pw = (prow_ >= pcol_, prow_ <= pcol_)
    strict_pw = (prow_ > pcol_, prow_ < pcol_)
    gate_lane = lax.broadcasted_iota(jnp.int32, (1, N_GATES), 1)
    is_fwd = gate_lane < C_HEADS
    seg = seg_ref[...]
    convw = convw_ref[...]
    neg_rate = -jnp.exp(alog_ref[...])
    dtb = dtb_ref[...]

    lane_pw = lax.broadcasted_iota(jnp.int32, (1, pw), 1)
    first_lanes = lane_pw < hd

    def block_diag(xb):
        zero = jnp.zeros_like(xb)
        return jnp.concatenate([jnp.where(first_lanes, xb, zero), jnp.where(first_lanes, zero, xb)], axis=0)

    expand = expand_ref[...]

    def a_step(j, first, last):
        zeros_halo = jnp.zeros((halo, 3 * C_WIDTH), F32)
        if first:
            r0 = 0
            win_scr[0:halo, :] = zeros_halo
            n_in = span + (0 if last else halo)
            win_scr[halo:halo + n_in, :] = cqkv_ref[0:n_in, :].astype(F32)
            if last:
                win_scr[halo + span:2 * halo + span, :] = zeros_halo
        elif last:
            r0 = t - span
            win_scr[0:halo + span, :] = cqkv_ref[t - span - halo:t, :].astype(F32)
            win_scr[halo + span:2 * halo + span, :] = zeros_halo
        else:
            r0 = pl.multiple_of(j * span, span)
            win_scr[...] = cqkv_ref[pl.ds(r0 - halo, span + 2 * halo), :].astype(F32)
        y = win_scr[halo:halo + span, :] * convw[CONV_K // 2:CONV_K // 2 + 1, :]
        for tap in range(CONV_K):
            if tap != CONV_K // 2:
                lo = halo - CONV_K // 2 + tap
                y = y + win_scr[lo:lo + span, :] * convw[tap:tap + 1, :]
        y = _silu(y)
        q_all = y[:, 0:C_WIDTH]
        k_all = y[:, C_WIDTH:2 * C_WIDTH]
        v_all = y[:, 2 * C_WIDTH:3 * C_WIDTH]
        sumsq = _seg_sum(jnp.concatenate([q_all * q_all, k_all * k_all], axis=0), seg, terms=2)
        q_all = q_all * lax.rsqrt(sumsq[0:span] + EPS) * (hd ** -0.5)
        k_all = k_all * lax.rsqrt(sumsq[span:2 * span] + EPS)

        ab = ab_ref[pl.ds(r0, span), :]
        g_all = neg_rate * _softplus(ab[:, 0:N_GATES] + dtb)
        beta_all = _sigmoid(ab[:, N_GATES:2 * N_GATES])

        grow, stacked = [], []
        for ci in range(ach):
            rs = slice(ci * c, (ci + 1) * c)
            gc_fb = _tri_sum(tri_lu, g_all[rs])
            gcc = jnp.where(is_fwd, gc_fb[0:c], gc_fb[c:2 * c])
            gl = jnp.where(is_fwd, gc_fb[c - 1:c, :], gc_fb[c:c + 1, :])
            grow.append(gcc.T)
            stacked += [gcc, jnp.broadcast_to(gl, (8, N_GATES)), beta_all[rs]]
        hi, mid, lo = _split3(jnp.concatenate(stacked, axis=0))
        ex = _dot(hi, expand) + _dot(mid, expand) + _dot(lo, expand)
        per_chunk = 2 * c + 8
        gc_x, beta_x, e_x, begc_x, eglgc_x, egl_x = [], [], [], [], [], []
        for ci in range(ach):
            base = ci * per_chunk
            gx = ex[base:base + c, :]
            glx = ex[base + c:base + c + 1, :]
            bx = ex[base + c + 8:base + per_chunk, :]
            ee = jnp.exp(gx)
            gc_x.append(gx)
            beta_x.append(bx)
            e_x.append(ee)
            begc_x.append(bx * ee)
            eglgc_x.append(jnp.exp(glx - gx))
            egl_x.append(jnp.exp(glx))

        cps = [(ci, p) for ci in range(ach) for p in range(n_pairs)]
        chains = [(ci, d, p) for ci in range(ach) for d in range(N_DIRS) for p in range(n_pairs)]

        def pslice(arr, ci, p):
            return arr[ci * c:(ci + 1) * c, p * pw:(p + 1) * pw]

        def xs(vals, ci, d, p):
            lo = d * C_WIDTH + p * pw
            return vals[ci][:, lo:lo + pw]

        kp16 = {key: pslice(k_all, *key).astype(BF16) for key in cps}
        k_bd = {key: block_diag(kp16[key]) for key in cps}
        kq = {key: _dot_nt(jnp.concatenate([kp16[key], pslice(q_all, *key).astype(BF16)], axis=0), k_bd[key])
              for key in cps}
        kk = {key: kq[key][0:c] for key in cps}
        qk = {key: kq[key][c:2 * c] for key in cps}

        decay, a_mat = {}, {}
        for ci, d, p in chains:
            g0 = d * C_HEADS + 2 * p
            grow_pair = jnp.concatenate([grow[ci][g0:g0 + 1, :], grow[ci][g0 + 1:g0 + 2, :]], axis=1)
            dec = jnp.where(incl_pw[d], jnp.exp(xs(gc_x, ci, d, p) - grow_pair), 0.0)
            decay[ci, d, p] = dec
            a_mat[ci, d, p] = jnp.where(strict_pw[d], kk[ci, p] * xs(beta_x, ci, d, p) * dec, 0.0)

        t_mat = {ch: jnp.where(eye_pw, 1.0, 0.0) - jnp.where(level_pw[0], a_mat[ch], 0.0) for ch in chains}
        for mask in level_pw[1:]:
            tb = {ch: t_mat[ch].astype(BF16) for ch in chains}
            t_bd = {ch: block_diag(tb[ch]) for ch in chains}
            et = {ch: _dot(jnp.where(mask, a_mat[ch], 0.0).astype(BF16), t_bd[ch]).astype(BF16) for ch in chains}
            t_mat = {ch: t_mat[ch] - _dot(tb[ch], block_diag(et[ch])) for ch in chains}

        uw = {}
        for ci, d, p in chains:
            vb = (pslice(v_all, ci, p) * xs(beta_x, ci, d, p)).astype(BF16)
            kbe = (pslice(k_all, ci, p) * xs(begc_x, ci, d, p)).astype(BF16)
            rhs = jnp.concatenate([block_diag(vb), block_diag(kbe)], axis=1)
            uw[ci, d, p] = _dot(t_mat[ci, d, p].astype(BF16), rhs)

        for ci in range(ach):
            m = (r0 // c) + ci
            q_c = q_all[ci * c:(ci + 1) * c, :]
            k_c = k_all[ci * c:(ci + 1) * c, :]
            for d in range(N_DIRS):
                dl = slice(d * C_WIDTH, (d + 1) * C_WIDTH)
                wq_scr[m, d, c:2 * c, :] = (q_c * e_x[ci][:, dl]).astype(BF16)
                kd_scr[m, d] = (k_c * eglgc_x[ci][:, dl]).astype(BF16)
                egl_scr[m, d] = egl_x[ci][:, dl]
                for p in range(n_pairs):
                    pl_ = slice(p * pw, (p + 1) * pw)
                    u_scr[m, d, :, pl_] = uw[ci, d, p][:, 0:pw].astype(BF16)
                    wq_scr[m, d, 0:c, pl_] = uw[ci, d, p][:, pw:2 * pw].astype(BF16)
                    aqk_scr[m, d, :, pl_] = (qk[ci, p] * decay[ci, d, p]).astype(BF16)

    n_steps = nch // ach
    if n_steps == 1:
        a_step(0, True, True)
    else:
        a_step(0, True, False)
        if n_steps > 2:
            def a_loop(j, carry):
                a_step(j, False, False)
                return carry
            lax.fori_loop(1, n_steps - 1, a_loop, 0)
        a_step(n_steps - 1, False, True)

    zeros_h = jnp.zeros((hd, hd), F32)
    for d in range(N_DIRS):
        for p in range(n_pairs):
            if has_s0:
                top = jnp.concatenate([s0_ref[d, 2 * p], zeros_h], axis=1)
                bot = jnp.concatenate([zeros_h, s0_ref[d, 2 * p + 1]], axis=1)
                s_scr[d, p] = jnp.concatenate([top, bot], axis=0)
            else:
                s_scr[d, p] = jnp.zeros((pw, pw), F32)

    prow = lax.broadcasted_iota(jnp.int32, (pw, pw), 0)
    pcol = lax.broadcasted_iota(jnp.int32, (pw, pw), 1)
    same_head = (prow >= hd) == (pcol >= hd)
    pairs = [(d, p) for d in range(N_DIRS) for p in range(n_pairs)]

    def b_state(n):
        ms = (n, nch - 1 - n)
        ls = {(d, p): slice(p * pw, (p + 1) * pw) for d, p in pairs}
        state = {(d, p): s_scr[d, p] for d, p in pairs}
        r = {(d, p): _dot(wq_scr[ms[d], d, :, ls[d, p]], state[d, p].astype(BF16)) for d, p in pairs}
        vnb = {}
        for d, p in pairs:
            v_new = u_scr[ms[d], d, :, ls[d, p]].astype(F32) - r[d, p][0:c, :]
            vnb[d, p] = v_new.astype(BF16)
        for d, p in pairs:
            kv = _dot_tn(kd_scr[ms[d], d, :, ls[d, p]], vnb[d, p])
            s_scr[d, p] = state[d, p] * egl_scr[ms[d], d, :, ls[d, p]] + jnp.where(same_head, kv, 0.0)
        return ms, ls, r, vnb

    def b_output(ms, ls, r, vnb):
        for d, p in pairs:
            o = r[d, p][c:2 * c, :] + _dot(aqk_scr[ms[d], d, :, ls[d, p]], block_diag(vnb[d, p]))
            u_scr[ms[d], d, :, ls[d, p]] = o.astype(BF16)

    def b_step(i, carry):
        first = b_state(2 * i)
        second = b_state(2 * i + 1)
        b_output(*first)
        b_output(*second)
        return carry

    assert nch % 2 == 0
    lax.fori_loop(0, nch // 2, b_step, 0)

    inv_n = 1.0 / hd

    c_chunks = 4 if nch % 4 == 0 else 1

    def c_step(i, carry):
        m0 = i * c_chunks
        o = jnp.concatenate([u_scr[m0 + ci, 0].astype(F32) + u_scr[m0 + ci, 1].astype(F32)
                             for ci in range(c_chunks)], axis=0)
        r0 = pl.multiple_of(m0 * c, c_chunks * c)
        ms = _seg_sum(o * o, seg, terms=2) * inv_n
        on = o * lax.rsqrt(ms + EPS) * normg_ref[...]
        oc_ref[pl.ds(r0, c_chunks * c), :] = (on * scg_ref[pl.ds(r0, c_chunks * c), :].astype(F32)).astype(BF16)
        return carry

    lax.fori_loop(0, nch // c_chunks, c_step, 0)

    if emit_state:
        for d in range(N_DIRS):
            for p in range(n_pairs):
                s_pair = s_scr[d, p]
                sfin_ref[d, 2 * p] = s_pair[0:hd, 0:hd]
                sfin_ref[d, 2 * p + 1] = s_pair[hd:pw, hd:pw]


def _gdn(cqkv, ab, scg, convw, alog, dtb, normg, seg, expand, s0, layer, *, emit_state, ach):
    b, t, _ = cqkv.shape
    nch = t // GDN_CHUNK
    assert nch % ach == 0
    has_s0 = s0 is not None
    const2 = lambda bi: (0, 0)
    in_specs = [
        pl.BlockSpec((None, t, 3 * C_WIDTH), lambda bi: (bi, 0, 0)),
        pl.BlockSpec((None, t, 2 * N_GATES), lambda bi: (bi, 0, 0)),
        pl.BlockSpec((None, t, C_WIDTH), lambda bi: (bi, 0, 0)),
        _layer_spec(convw, layer),
        _layer_spec(alog, layer),
        _layer_spec(dtb, layer),
        _layer_spec(normg, layer),
        pl.BlockSpec(seg.shape, const2),
        pl.BlockSpec(expand.shape, const2),
    ]
    args = [cqkv, ab, scg, convw, alog, dtb, normg, seg, expand]
    state_spec = pl.BlockSpec((None, N_DIRS, C_HEADS, C_HEAD_DIM, C_HEAD_DIM), lambda bi: (bi, 0, 0, 0, 0))
    if has_s0:
        in_specs.append(pl.BlockSpec((None, None, N_DIRS, C_HEADS, C_HEAD_DIM, C_HEAD_DIM),
                                     lambda bi: (bi, layer, 0, 0, 0, 0)))
        args.append(s0)
    out_specs = [pl.BlockSpec((None, t, C_WIDTH), lambda bi: (bi, 0, 0))]
    out_shape = [jax.ShapeDtypeStruct((b, t, C_WIDTH), BF16)]
    if emit_state:
        out_specs.append(state_spec)
        out_shape.append(jax.ShapeDtypeStruct((b, N_DIRS, C_HEADS, C_HEAD_DIM, C_HEAD_DIM), F32))
    pair_w = 2 * C_HEAD_DIM
    scratch = [
        pltpu.VMEM((nch, N_DIRS, 2 * GDN_CHUNK, C_WIDTH), BF16),
        pltpu.VMEM((nch, N_DIRS, GDN_CHUNK, C_WIDTH), BF16),
        pltpu.VMEM((nch, N_DIRS, GDN_CHUNK, C_WIDTH), BF16),
        pltpu.VMEM((nch, N_DIRS, GDN_CHUNK, C_WIDTH), BF16),
        pltpu.VMEM((nch, N_DIRS, 1, C_WIDTH), F32),
        pltpu.VMEM((N_DIRS, C_HEADS // 2, pair_w, pair_w), F32),
        pltpu.VMEM((ach * GDN_CHUNK + 2 * BF16_SUBLANES, 3 * C_WIDTH), F32),
    ]
    return pl.pallas_call(
        functools.partial(_gdn_body, t=t, has_s0=has_s0, emit_state=emit_state, ach=ach),
        grid=(b,),
        in_specs=in_specs,
        out_specs=out_specs,
        out_shape=out_shape,
        scratch_shapes=scratch,
        compiler_params=pltpu.CompilerParams(dimension_semantics=("arbitrary",),
                                             vmem_limit_bytes=VMEM_LIMIT_BYTES),
        name="gdn_state" if emit_state else "gdn",
    )(*args)


def _attn_body(*refs, banded, nseq, tq, nblk_total):
    it = iter(refs)
    x_ref, q_ref, kv_ref = (next(it) for _ in range(3))
    if banded:
        kvl_ref, kvr_ref, band_ref = (next(it) for _ in range(3))
        ck_ref, cv_ref = (next(it) for _ in range(2))
    sink_ref, sag_ref, ob_ref, oc_ref, wout_ref, gpost_ref, mod_ref, y_ref = (next(it) for _ in range(8))

    nb = tq // ATTN_BLOCK
    rows_q = A_GROUP * ATTN_BLOCK
    lane = lax.broadcasted_iota(jnp.int32, (1, A_KV_WIDTH), 1)
    head_mask = [lane < HEAD_DIM, lane >= HEAD_DIM]
    sum_lane = [HEAD_DIM, 0]

    def split_heads(a, spare=0.0):
        fill = [jnp.where(lane == sum_lane[h], spare, 0.0).astype(a.dtype) for h in range(A_KV_HEADS)]
        return [jnp.where(head_mask[h], a, fill[h]) for h in range(A_KV_HEADS)]

    if banded:
        kv_ext = jnp.concatenate([kvl_ref[0], kv_ref[0], kvr_ref[0]], axis=0)
        k_own = split_heads(kv_ext[:, 0:A_KV_WIDTH])
        v_own = split_heads(kv_ext[:, A_KV_WIDTH:2 * A_KV_WIDTH], spare=1.0)
        k_ctx = [split_heads(ck_ref[...].astype(BF16))]
        v_ctx = [split_heads(cv_ref[...].astype(BF16), spare=1.0)]
        first_blk = pl.program_id(1) * nb
        vis_l, vis_r = [], []
        for j in range(nb):
            thr_l = jnp.where(first_blk + j > 0, 0.5, 2.0)
            thr_r = jnp.where(first_blk + j < nblk_total - 1, 0.5, 2.0)
            vis_l.append(band_ref[:, 0:ATTN_BLOCK] > thr_l)
            vis_r.append(band_ref[:, 2 * ATTN_BLOCK:3 * ATTN_BLOCK] > thr_r)
    else:
        k_ctx = [split_heads(kv_ref[s, :, 0:A_KV_WIDTH]) for s in range(nseq)]
        v_ctx = [split_heads(kv_ref[s, :, A_KV_WIDTH:2 * A_KV_WIDTH], spare=1.0) for s in range(nseq)]

    units = [(s, j, h) for s in range(nseq) for j in range(nb) for h in range(A_KV_HEADS)]
    q2 = {(s, j): jnp.concatenate([q_ref[s, j * ATTN_BLOCK:(j + 1) * ATTN_BLOCK, g * A_KV_WIDTH:(g + 1) * A_KV_WIDTH]
                                   for g in range(A_GROUP)], axis=0) for s in range(nseq) for j in range(nb)}

    def scores(s, j, h):
        s_ctx = _dot_nt(q2[s, j], k_ctx[s][h])
        if not banded:
            return s_ctx, None
        s_own = _dot_nt(q2[s, j], k_own[h][j * ATTN_BLOCK:(j + 3) * ATTN_BLOCK])
        left = jnp.where(vis_l[j], s_own[:, 0:ATTN_BLOCK], NEG_INF)
        right = jnp.where(vis_r[j], s_own[:, 2 * ATTN_BLOCK:3 * ATTN_BLOCK], NEG_INF)
        return s_ctx, jnp.concatenate([left, s_own[:, ATTN_BLOCK:2 * ATTN_BLOCK], right], axis=1)

    o_acc = {(s, j): jnp.zeros((rows_q, A_KV_WIDTH), F32) for s in range(nseq) for j in range(nb)}
    pending = scores(*units[0])
    for ui, (s, j, h) in enumerate(units):
        s_ctx, s_own = pending
        if ui + 1 < len(units):
            pending = scores(*units[ui + 1])
        sink = sink_ref[h] * LOG2_E
        s_all = jnp.concatenate([s_ctx, s_own], axis=1) if banded else s_ctx
        m = jnp.maximum(jnp.max(s_all, axis=-1, keepdims=True), sink)
        pb = jnp.exp2(s_all - m).astype(BF16)
        n_ctx = s_ctx.shape[1]
        acc = _dot(pb[:, 0:n_ctx], v_ctx[s][h])
        if banded:
            acc = acc + _dot(pb[:, n_ctx:], v_own[h][j * ATTN_BLOCK:(j + 3) * ATTN_BLOCK])
        denom = acc[:, sum_lane[h]:sum_lane[h] + 1] + jnp.exp2(sink - m)
        o_acc[s, j] = o_acc[s, j] + jnp.where(head_mask[h], acc, 0.0) * (1.0 / denom)
    o_blocks = [jnp.concatenate([o_acc[s, j][g * ATTN_BLOCK:(g + 1) * ATTN_BLOCK] for g in range(A_GROUP)], axis=1)
                for s in range(nseq) for j in range(nb)]
    o_a = jnp.concatenate(o_blocks, axis=0) if len(o_blocks) > 1 else o_blocks[0]

    def rows_of(ref):
        return jnp.concatenate([ref[s] for s in range(nseq)], axis=0) if nseq > 1 else ref[0]

    mix_a = (o_a * rows_of(sag_ref).astype(F32)).astype(BF16)
    mix = (_dot(mix_a, wout_ref[0:A_WIDTH, :])
           + _dot(rows_of(ob_ref), wout_ref[A_WIDTH:A_WIDTH + B_WIDTH, :])
           + _dot(rows_of(oc_ref), wout_ref[A_WIDTH + B_WIDTH:A_WIDTH + B_WIDTH + C_WIDTH, :]))
    ms = jnp.mean(mix * mix, axis=-1, keepdims=True)
    y = rows_of(x_ref) + mod_ref[2:3, :] * (mix * lax.rsqrt(ms + EPS) * gpost_ref[...])
    for s in range(nseq):
        y_ref[s] = y[s * tq:(s + 1) * tq]


def _attention_out(x, q, kv, sag, ob, oc, sink_rows, wout, gpost, mod, band, ctx_k, ctx_v, layer, *, mod_row0,
                   per_batch, tq, nseq):
    b, t, d = x.shape
    banded = band is not None
    assert b % nseq == 0 and (nseq == 1 or not per_batch)
    nblk_total = t // ATTN_BLOCK
    per_q = tq // ATTN_BLOCK
    const2 = lambda bi, i: (0, 0)

    def tok(width):
        return pl.BlockSpec((nseq, tq, width), lambda bi, i: (bi, i, 0))

    in_specs = [tok(d), tok(A_WIDTH)]
    args = [x, q]
    if banded:
        in_specs.append(tok(2 * A_KV_WIDTH))
        args.append(kv)
        in_specs.append(pl.BlockSpec((1, ATTN_BLOCK, 2 * A_KV_WIDTH),
                                     lambda bi, i: (bi, jnp.maximum(i * per_q - 1, 0), 0)))
        in_specs.append(pl.BlockSpec((1, ATTN_BLOCK, 2 * A_KV_WIDTH),
                                     lambda bi, i: (bi, jnp.minimum((i + 1) * per_q, nblk_total - 1), 0)))
        in_specs.append(pl.BlockSpec(band.shape, const2))
        args += [kv, kv, band]
        past = ctx_k.shape[2]
        in_specs += [pl.BlockSpec((None, None, past, A_KV_WIDTH), lambda bi, i: (bi, layer, 0, 0))] * 2
        args += [ctx_k, ctx_v]
    else:
        in_specs.append(pl.BlockSpec((nseq, t, 2 * A_KV_WIDTH), lambda bi, i: (bi, 0, 0)))
        args.append(kv)
    in_specs += [
        _layer_spec(sink_rows, layer),
        tok(A_WIDTH), tok(B_WIDTH), tok(C_WIDTH),
        _layer_spec(wout, layer),
        _layer_spec(gpost, layer),
        _mod_spec(mod, layer, mod_row0, per_batch),
    ]
    args += [sink_rows, sag, ob, oc, wout, gpost, mod]
    return pl.pallas_call(
        functools.partial(_attn_body, banded=banded, nseq=nseq, tq=tq, nblk_total=nblk_total),
        grid=(b // nseq, t // tq),
        in_specs=in_specs,
        out_specs=tok(d),
        out_shape=jax.ShapeDtypeStruct((b, t, d), F32),
        compiler_params=pltpu.CompilerParams(dimension_semantics=("arbitrary", "arbitrary"),
                                             vmem_limit_bytes=VMEM_LIMIT_BYTES),
        name="attention_banded" if banded else "attention_full",
    )(*args)


def _block_ones(width, block):
    idx = np.arange(width) // block
    return jnp.asarray((idx[:, None] == idx[None, :]).astype(np.float32), dtype=BF16)


def _gate_expansion():
    lane_gate = np.arange(N_GATES * C_HEAD_DIM) // C_HEAD_DIM
    return jnp.asarray((np.arange(N_GATES)[:, None] == lane_gate[None, :]).astype(np.float32), dtype=BF16)


def _rope_tables(n):
    half = HEAD_DIM // 2
    nf = half // 2
    pos = np.arange(n)
    row = (pos // GRID_W).astype(np.float32)
    colp = (pos % GRID_W).astype(np.float32)
    inv_freq = np.float32(ROPE_BASE) ** (-np.arange(nf, dtype=np.float32) / np.float32(nf))
    lane = np.arange(LANES) % HEAD_DIM
    freq_idx = lane % nf
    quarter = lane // nf
    p = np.where((quarter < 2)[None, :], row[:, None], colp[:, None])
    ang = (p * inv_freq[freq_idx][None, :]).astype(np.float32)
    cos, sin = np.cos(ang), np.sin(ang)
    first = (quarter % 2 == 0)[None, :]
    sa = np.where(first, -sin, 0.0)
    sb = np.where(first, 0.0, sin)
    return tuple(jnp.asarray(a, dtype=F32) for a in (cos, sa, sb))


def _band_table():
    q_rel = np.arange(ATTN_BLOCK)
    k_rel = np.arange(3 * ATTN_BLOCK) - ATTN_BLOCK
    band = (np.abs(q_rel[:, None] - k_rel[None, :]) <= ATTN_BLOCK).astype(np.float32)
    return jnp.asarray(np.tile(band, (A_GROUP, 1)))


def _group_major(a, axis):
    shape = a.shape
    split = shape[:axis] + (A_KV_HEADS, A_GROUP, HEAD_DIM) + shape[axis + 1:]
    return jnp.swapaxes(a.reshape(split), axis, axis + 1).reshape(shape)


def _layout_w_in(w_in):
    o = 0
    sec = {}
    for name, width in (("aq", A_WIDTH), ("ak", A_KV_WIDTH), ("av", A_KV_WIDTH), ("ag", A_WIDTH),
                        ("bu", B_WIDTH), ("bv", B_WIDTH), ("bg", B_WIDTH),
                        ("cq", C_WIDTH), ("ck", C_WIDTH), ("cv", C_WIDTH), ("ca", N_GATES), ("cb", N_GATES),
                        ("cg", C_WIDTH)):
        sec[name] = w_in[:, :, o:o + width].astype(BF16)
        o += width
    pad = jnp.zeros(w_in.shape[:2] + (LANES - 2 * N_GATES,), BF16)
    cols = [_group_major(sec["aq"], 2), sec["ak"], sec["av"], _group_major(sec["ag"], 2), sec["bu"], sec["bv"],
            sec["bg"], sec["cq"], sec["ck"], sec["cv"], sec["cg"], sec["ca"], sec["cb"], pad]
    return jnp.concatenate(cols, axis=2)


def _row_tile(t, target):
    tile = min(t, target)
    while t % tile:
        tile -= ATTN_BLOCK
    return tile


def kernel(x_prompt, x_sample, cache_k, cache_v, state_delta, c, c_ctx, w_mod, b_mod, g_pre, w_in, attn_sink,
           sgu_ln_g, sgu_ln_b, sgu_w, sgu_b, gdn_conv_w, gdn_a_log, gdn_dt_bias, gdn_norm_g, g_post, w_out):
    depth, d, _ = w_mod.shape
    b_ctx, t_ctx, _ = x_prompt.shape
    b_lat, t_lat, _ = x_sample.shape
    past = cache_k.shape[2]
    assert t_ctx % SGU_CHUNK == 0 and t_lat % SGU_CHUNK == 0 and t_lat % GRID_W == 0
    assert t_ctx // GDN_CHUNK >= 2 and t_lat // GDN_CHUNK >= 2
    assert w_in.shape[2] == 2 * A_WIDTH + 2 * A_KV_WIDTH + 3 * B_WIDTH + 4 * C_WIDTH + 2 * N_GATES

    n_cond = b_lat + 1
    rows = -(-n_cond // 8) * 8
    conds = jnp.concatenate([c, c_ctx[None, :], jnp.zeros((rows - n_cond, d), c.dtype)], axis=0)
    mod = _modulation(conds, w_mod, b_mod).reshape(depth, rows, 3, d)

    seg_b = _block_ones(B_WIDTH, B_GROUP_DIM)
    seg_c = _block_ones(C_WIDTH, C_HEAD_DIM)
    expand_c = _gate_expansion()
    rope_tabs = _rope_tables(t_lat)
    band = _band_table()
    ck = cache_k.reshape(b_lat, depth, past, A_KV_WIDTH)
    cv = cache_v.reshape(b_lat, depth, past, A_KV_WIDTH)
    tm_ctx, tm_lat = _row_tile(b_ctx * t_ctx, PROJ_ROWS_CTX), _row_tile(t_lat, PROJ_ROWS_LAT)

    w1 = _layout_w_in(w_in)
    assert w1.shape[2] == _W1_COLS
    wout_b = w_out.astype(BF16)
    wout = jnp.concatenate([_group_major(wout_b[:, 0:A_WIDTH], 1), wout_b[:, A_WIDTH:]], axis=1)
    gpre = g_pre.reshape(depth, 1, d)
    gpost = g_post.reshape(depth, 1, d)
    lng = sgu_ln_g.reshape(depth, 1, B_WIDTH)
    lnb = sgu_ln_b.reshape(depth, 1, B_WIDTH)
    sw = sgu_w.astype(BF16)
    sbias = jnp.repeat(jnp.swapaxes(sgu_b, 1, 2), B_GROUP_DIM, axis=2)
    sink_rows = jnp.repeat(attn_sink.reshape(depth, A_KV_HEADS, A_GROUP), ATTN_BLOCK, axis=2)[..., None]
    alog = gdn_a_log.reshape(depth, 1, N_GATES)
    dtb = gdn_dt_bias.reshape(depth, 1, N_GATES)
    normg = jnp.tile(gdn_norm_g, (1, C_HEADS)).reshape(depth, 1, C_WIDTH)
    ctx_nseq = max(1, min(b_ctx, ATTN_ROWS_CTX // t_ctx))
    while b_ctx % ctx_nseq:
        ctx_nseq -= 1

    xp, xs = x_prompt, x_sample
    new_k, new_v, new_s = [], [], []
    for l in range(depth):
        flat = _projection(xp.reshape(1, b_ctx * t_ctx, d), mod, gpre, w1, seg_b, lng, lnb, sw, sbias, None, l,
                           mod_row0=b_lat, per_batch=False, keep_f32_kv=True, tm=tm_ctx)
        q, kv, sag, ob, cqkv, scg, ab, kf, vf = (a.reshape(b_ctx, t_ctx, a.shape[-1]) for a in flat)
        oc, s_fin = _gdn(cqkv, ab, scg, gdn_conv_w, alog, dtb, normg, seg_c, expand_c, None, l, emit_state=True,
                         ach=min(GDN_A_CHUNKS, t_ctx // GDN_CHUNK))
        xp = _attention_out(xp, q, kv, sag, ob, oc, sink_rows, wout, gpost, mod, None, None, None, l,
                            mod_row0=b_lat, per_batch=False, tq=t_ctx, nseq=ctx_nseq)
        new_k.append(kf.reshape(b_ctx, t_ctx, A_KV_HEADS, HEAD_DIM))
        new_v.append(vf.reshape(b_ctx, t_ctx, A_KV_HEADS, HEAD_DIM))
        new_s.append(s_fin)

        q, kv, sag, ob, cqkv, scg, ab = _projection(
            xs, mod, gpre, w1, seg_b, lng, lnb, sw, sbias, rope_tabs, l, mod_row0=0, per_batch=True,
            keep_f32_kv=False, tm=tm_lat)
        (oc,) = _gdn(cqkv, ab, scg, gdn_conv_w, alog, dtb, normg, seg_c, expand_c, state_delta, l, emit_state=False,
                     ach=min(GDN_A_CHUNKS, t_lat // GDN_CHUNK))
        xs = _attention_out(xs, q, kv, sag, ob, oc, sink_rows, wout, gpost, mod, band, ck, cv, l,
                            mod_row0=0, per_batch=True, tq=_row_tile(t_lat, ATTN_ROWS_LAT), nseq=1)

    return (xp, xs, jnp.stack(new_k, axis=1), jnp.stack(new_v, axis=1), jnp.stack(new_s, axis=1))
```

```python
import functools

import numpy as np
import jax
import jax.numpy as jnp
from jax import lax
from jax.experimental import pallas as pl
from jax.experimental.pallas import tpu as pltpu

F32 = jnp.float32
BF16 = jnp.bfloat16

GRID_W = 64
HEAD_DIM = 64
A_HEADS = 8
A_KV_HEADS = 2
A_GROUP = A_HEADS // A_KV_HEADS
A_WIDTH = A_HEADS * HEAD_DIM
A_KV_WIDTH = A_KV_HEADS * HEAD_DIM
ATTN_BLOCK = 128
ROPE_BASE = 10000.0
B_GROUPS = 4
B_GROUP_DIM = 64
B_WIDTH = B_GROUPS * B_GROUP_DIM
SGU_CHUNK = 128
C_HEADS = 4
C_HEAD_DIM = 64
C_WIDTH = C_HEADS * C_HEAD_DIM
CONV_K = 5
GDN_CHUNK = 64
N_DIRS = 2
N_GATES = N_DIRS * C_HEADS
EPS = 1e-6
NEG_INF = -1e30
LOG2_E = 1.4426950408889634

LANES = 128
BF16_SUBLANES = 16
VMEM_LIMIT_BYTES = 56 * 1024 * 1024

GDN_A_CHUNKS = 4

PROJ_ROWS_CTX = 1024
PROJ_ROWS_LAT = 1024
ATTN_ROWS_CTX = 1024
ATTN_ROWS_LAT = 1024

_OFF_Q = 0
_OFF_K = _OFF_Q + A_WIDTH
_OFF_V = _OFF_K + A_KV_WIDTH
_OFF_AG = _OFF_V + A_KV_WIDTH
_OFF_BU = _OFF_AG + A_WIDTH
_OFF_BV = _OFF_BU + B_WIDTH
_OFF_BG = _OFF_BV + B_WIDTH
_OFF_CQKV = _OFF_BG + B_WIDTH
_OFF_CG = _OFF_CQKV + 3 * C_WIDTH
_OFF_AB = _OFF_CG + C_WIDTH
_W1_COLS = _OFF_AB + LANES

def _dot(a, b):
    return jnp.dot(a, b, preferred_element_type=F32)


def _layer_spec(arr, layer):
    zeros = (0,) * (arr.ndim - 1)
    return pl.BlockSpec((None,) + tuple(arr.shape[1:]), lambda *_: (layer,) + zeros, pipeline_mode=pl.Buffered(1))


def _mod_spec(mod, layer, row0, per_batch):
    return pl.BlockSpec((None, None) + tuple(mod.shape[2:]),
                        lambda bi, i: (layer, row0 + (bi if per_batch else 0), 0, 0))


def _dot_nt(a, b):
    return lax.dot_general(a, b, (((1,), (1,)), ((), ())), preferred_element_type=F32)


def _dot_tn(a, b):
    return lax.dot_general(a, b, (((0,), (0,)), ((), ())), preferred_element_type=F32)


def _split3(x):
    hi = x.astype(BF16)
    r1 = x - hi.astype(F32)
    mid = r1.astype(BF16)
    lo = (r1 - mid.astype(F32)).astype(BF16)
    return hi, mid, lo


def _seg_sum(x, seg, terms=3):
    hi, mid, lo = _split3(x)
    out = _dot(hi, seg) + _dot(mid, seg)
    return out + _dot(lo, seg) if terms == 3 else out


def _tri_sum(tri, x):
    hi, mid, lo = _split3(x)
    return _dot(tri, hi) + _dot(tri, mid) + _dot(tri, lo)


def _sigmoid(x):
    return 1.0 / (1.0 + jnp.exp(-x))


def _silu(x):
    return x * _sigmoid(x)


def _softplus(x):
    return jnp.maximum(x, 0.0) + jnp.log1p(jnp.exp(-jnp.abs(x)))


def _mod_body(cond_ref, w_ref, b_ref, o_ref):
    a = _silu(cond_ref[...]).astype(BF16)
    o_ref[...] = _dot(a, w_ref[...].astype(BF16)) + b_ref[...]


def _modulation(conds, w_mod, b_mod):
    depth, d, d3 = w_mod.shape
    rows = conds.shape[0]
    tn = d
    return pl.pallas_call(
        _mod_body,
        grid=(depth, d3 // tn),
        in_specs=[
            pl.BlockSpec((rows, d), lambda l, j: (0, 0)),
            pl.BlockSpec((None, d, tn), lambda l, j: (l, 0, j)),
            pl.BlockSpec((None, 1, tn), lambda l, j: (l, 0, j)),
        ],
        out_specs=pl.BlockSpec((None, rows, tn), lambda l, j: (l, 0, j)),
        out_shape=jax.ShapeDtypeStruct((depth, rows, d3), F32),
        compiler_params=pltpu.CompilerParams(dimension_semantics=("arbitrary", "arbitrary"),
                                             vmem_limit_bytes=VMEM_LIMIT_BYTES),
        name="modulation",
    )(conds, w_mod, b_mod.reshape(depth, 1, d3))


def _rope(z, c, sa, sb):
    w = z.shape[1]
    reps = w // LANES
    if reps > 1:
        c = jnp.concatenate([c] * reps, axis=1)
        sa = jnp.concatenate([sa] * reps, axis=1)
        sb = jnp.concatenate([sb] * reps, axis=1)
    half_pair = HEAD_DIM // 4
    return z * c + pltpu.roll(z, w - half_pair, 1) * sa + pltpu.roll(z, half_pair, 1) * sb


def _proj_body(*refs, rope, keep_f32_kv, tm):
    it = iter(refs)
    x_ref, mod_ref, gpre_ref, w_ref, seg_ref, lng_ref, lnb_ref, sw_ref, sbias_ref = (next(it) for _ in range(9))
    if rope:
        rc_ref, rsa_ref, rsb_ref = (next(it) for _ in range(3))
    q_ref, kv_ref, sag_ref, ob_ref, cqkv_ref, scg_ref, ab_ref = (next(it) for _ in range(7))
    if keep_f32_kv:
        kf_ref, vf_ref = (next(it) for _ in range(2))

    x = x_ref[...]
    ms = jnp.mean(x * x, axis=-1, keepdims=True)
    h = x * lax.rsqrt(ms + EPS) * gpre_ref[...]
    h = h * (1.0 + mod_ref[1:2, :]) + mod_ref[0:1, :]
    hb = h.astype(BF16)

    def proj(lo, width):
        return _dot(hb, w_ref[:, lo:lo + width])

    zq = proj(_OFF_Q, A_WIDTH)
    zk = proj(_OFF_K, A_KV_WIDTH)
    zv = proj(_OFF_V, A_KV_WIDTH)
    if keep_f32_kv:
        kf_ref[...] = zk
        vf_ref[...] = zv
    if rope:
        c, sa, sb = rc_ref[...], rsa_ref[...], rsb_ref[...]
        zq = _rope(zq, c, sa, sb)
        zk = _rope(zk, c, sa, sb)
    q_ref[...] = (zq * (HEAD_DIM ** -0.5 * LOG2_E)).astype(BF16)
    kv_ref[:, 0:A_KV_WIDTH] = zk.astype(BF16)
    kv_ref[:, A_KV_WIDTH:2 * A_KV_WIDTH] = zv.astype(BF16)
    sag_ref[...] = _silu(proj(_OFF_AG, A_WIDTH)).astype(BF16)

    zu = proj(_OFF_BU, B_WIDTH)
    zbv = proj(_OFF_BV, B_WIDTH)
    zbg = proj(_OFF_BG, B_WIDTH)
    seg = seg_ref[...]
    inv_n = 1.0 / B_GROUP_DIM
    mean = _seg_sum(zbv, seg) * inv_n
    xc = zbv - mean
    var = _seg_sum(xc * xc, seg) * inv_n
    vg = (xc * lax.rsqrt(var + EPS) * lng_ref[...] + lnb_ref[...]).astype(BF16)
    sbias = sbias_ref[...]
    mixed_chunks = []
    for ci in range(tm // SGU_CHUNK):
        rows = slice(ci * SGU_CHUNK, (ci + 1) * SGU_CHUNK)
        per_group = [_dot(sw_ref[g], vg[rows, g * B_GROUP_DIM:(g + 1) * B_GROUP_DIM]) for g in range(B_GROUPS)]
        mixed_chunks.append(jnp.concatenate(per_group, axis=1) + sbias)
    mixed = jnp.concatenate(mixed_chunks, axis=0) if len(mixed_chunks) > 1 else mixed_chunks[0]
    ob_ref[...] = (zu * mixed * _silu(zbg)).astype(BF16)

    cqkv_ref[...] = proj(_OFF_CQKV, 3 * C_WIDTH).astype(BF16)
    scg_ref[...] = _silu(proj(_OFF_CG, C_WIDTH)).astype(BF16)
    ab_ref[...] = proj(_OFF_AB, LANES)[:, 0:2 * N_GATES]


def _projection(x, mod, gpre, w1, seg, lng, lnb, sw, sbias, rope_tabs, layer, *, mod_row0, per_batch, keep_f32_kv,
                tm):
    b, t, d = x.shape
    rope = rope_tabs is not None
    const2 = lambda bi, i: (0, 0)
    in_specs = [
        pl.BlockSpec((None, tm, d), lambda bi, i: (bi, i, 0)),
        _mod_spec(mod, layer, mod_row0, per_batch),
        _layer_spec(gpre, layer),
        _layer_spec(w1, layer),
        pl.BlockSpec(seg.shape, const2),
        _layer_spec(lng, layer),
        _layer_spec(lnb, layer),
        _layer_spec(sw, layer),
        _layer_spec(sbias, layer),
    ]
    args = [x, mod, gpre, w1, seg, lng, lnb, sw, sbias]
    if rope:
        in_specs += [pl.BlockSpec((tm, LANES), lambda bi, i: (i, 0))] * 3
        args += list(rope_tabs)

    def tok(width, dtype):
        return (pl.BlockSpec((None, tm, width), lambda bi, i: (bi, i, 0)),
                jax.ShapeDtypeStruct((b, t, width), dtype))

    outs = [tok(A_WIDTH, BF16), tok(2 * A_KV_WIDTH, BF16), tok(A_WIDTH, BF16), tok(B_WIDTH, BF16),
            tok(3 * C_WIDTH, BF16), tok(C_WIDTH, BF16), tok(2 * N_GATES, F32)]
    if keep_f32_kv:
        outs += [tok(A_KV_WIDTH, F32), tok(A_KV_WIDTH, F32)]
    return pl.pallas_call(
        functools.partial(_proj_body, rope=rope, keep_f32_kv=keep_f32_kv, tm=tm),
        grid=(b, t // tm),
        in_specs=in_specs,
        out_specs=[o[0] for o in outs],
        out_shape=[o[1] for o in outs],
        compiler_params=pltpu.CompilerParams(dimension_semantics=("arbitrary", "arbitrary"),
                                             vmem_limit_bytes=VMEM_LIMIT_BYTES),
        name="projection_rope" if rope else "projection",
    )(*args)


def _tri_masks(n, reps):
    rows = lax.broadcasted_iota(jnp.int32, (n, reps * n), 0)
    cols = lax.broadcasted_iota(jnp.int32, (n, reps * n), 1) & (n - 1)
    masks = []
    level = 0
    while (1 << level) < n:
        same_pair = (rows >> (level + 1)) == (cols >> (level + 1))
        other_half = (rows >> level) != (cols >> level)
        masks.append(same_pair & other_half)
        level += 1
    return rows, cols, rows == cols, masks


def _gdn_body(*refs, t, has_s0, emit_state, ach):
    it = iter(refs)
    cqkv_ref, ab_ref, scg_ref, convw_ref, alog_ref, dtb_ref, normg_ref, seg_ref, expand_ref = (
        next(it) for _ in range(9))
    s0_ref = next(it) if has_s0 else None
    oc_ref = next(it)
    sfin_ref = next(it) if emit_state else None
    wq_scr, u_scr, aqk_scr, kd_scr, egl_scr, s_scr, win_scr = (next(it) for _ in range(7))

    c = GDN_CHUNK
    hd = C_HEAD_DIM
    pw = 2 * hd
    n_pairs = C_HEADS // 2
    nch = t // c
    span = ach * c
    halo = BF16_SUBLANES
    rows, cols, _, _ = _tri_masks(c, 1)
    tri_lu = jnp.concatenate([jnp.where(rows >= cols, 1.0, 0.0), jnp.where(rows <= cols, 1.0, 0.0)],
                             axis=0).astype(BF16)
    prow_, pcol_, eye_pw, level_pw = _tri_masks(c, 2)
    incl_pw = (prow_ >= pcol_, prow_ <= pcol_)
    strict_pw = (prow_ > pcol_, prow_ < pcol_)
    gate_lane = lax.broadcasted_iota(jnp.int32, (1, N_GATES), 1)
    is_fwd = gate_lane < C_HEADS
    seg = seg_ref[...]
    convw = convw_ref[...]
    neg_rate = -jnp.exp(alog_ref[...])
    dtb = dtb_ref[...]

    lane_pw = lax.broadcasted_iota(jnp.int32, (1, pw), 1)
    first_lanes = lane_pw < hd

    def block_diag(xb):
        zero = jnp.zeros_like(xb)
        return jnp.concatenate([jnp.where(first_lanes, xb, zero), jnp.where(first_lanes, zero, xb)], axis=0)

    expand = expand_ref[...]

    def a_step(j, first, last):
        zeros_halo = jnp.zeros((halo, 3 * C_WIDTH), F32)
        if first:
            r0 = 0
            win_scr[0:halo, :] = zeros_halo
            n_in = span + (0 if last else halo)
            win_scr[halo:halo + n_in, :] = cqkv_ref[0:n_in, :].astype(F32)
            if last:
                win_scr[halo + span:2 * halo + span, :] = zeros_halo
        elif last:
            r0 = t - span
            win_scr[0:halo + span, :] = cqkv_ref[t - span - halo:t, :].astype(F32)
            win_scr[halo + span:2 * halo + span, :] = zeros_halo
        else:
            r0 = pl.multiple_of(j * span, span)
            win_scr[...] = cqkv_ref[pl.ds(r0 - halo, span + 2 * halo), :].astype(F32)
        y = win_scr[halo:halo + span, :] * convw[CONV_K // 2:CONV_K // 2 + 1, :]
        for tap in range(CONV_K):
            if tap != CONV_K // 2:
                lo = halo - CONV_K // 2 + tap
                y = y + win_scr[lo:lo + span, :] * convw[tap:tap + 1, :]
        y = _silu(y)
        q_all = y[:, 0:C_WIDTH]
        k_all = y[:, C_WIDTH:2 * C_WIDTH]
        v_all = y[:, 2 * C_WIDTH:3 * C_WIDTH]
        sumsq = _seg_sum(jnp.concatenate([q_all * q_all, k_all * k_all], axis=0), seg, terms=2)
        q_all = q_all * lax.rsqrt(sumsq[0:span] + EPS) * (hd ** -0.5)
        k_all = k_all * lax.rsqrt(sumsq[span:2 * span] + EPS)

        ab = ab_ref[pl.ds(r0, span), :]
        g_all = neg_rate * _softplus(ab[:, 0:N_GATES] + dtb)
        beta_all = _sigmoid(ab[:, N_GATES:2 * N_GATES])

        grow, stacked = [], []
        for ci in range(ach):
            rs = slice(ci * c, (ci + 1) * c)
            gc_fb = _tri_sum(tri_lu, g_all[rs])
            gcc = jnp.where(is_fwd, gc_fb[0:c], gc_fb[c:2 * c])
            gl = jnp.where(is_fwd, gc_fb[c - 1:c, :], gc_fb[c:c + 1, :])
            grow.append(gcc.T)
            stacked += [gcc, jnp.broadcast_to(gl, (8, N_GATES)), beta_all[rs]]
        hi, mid, lo = _split3(jnp.concatenate(stacked, axis=0))
        ex = _dot(hi, expand) + _dot(mid, expand) + _dot(lo, expand)
        per_chunk = 2 * c + 8
        gc_x, beta_x, e_x, begc_x, eglgc_x, egl_x = [], [], [], [], [], []
        for ci in range(ach):
            base = ci * per_chunk
            gx = ex[base:base + c, :]
            glx = ex[base + c:base + c + 1, :]
            bx = ex[base + c + 8:base + per_chunk, :]
            ee = jnp.exp(gx)
            gc_x.append(gx)
            beta_x.append(bx)
            e_x.append(ee)
            begc_x.append(bx * ee)
            eglgc_x.append(jnp.exp(glx - gx))
            egl_x.append(jnp.exp(glx))

        cps = [(ci, p) for ci in range(ach) for p in range(n_pairs)]
        chains = [(ci, d, p) for ci in range(ach) for d in range(N_DIRS) for p in range(n_pairs)]

        def pslice(arr, ci, p):
            return arr[ci * c:(ci + 1) * c, p * pw:(p + 1) * pw]

        def xs(vals, ci, d, p):
            lo = d * C_WIDTH + p * pw
            return vals[ci][:, lo:lo + pw]

        kp16 = {key: pslice(k_all, *key).astype(BF16) for key in cps}
        k_bd = {key: block_diag(kp16[key]) for key in cps}
        kq = {key: _dot_nt(jnp.concatenate([kp16[key], pslice(q_all, *key).astype(BF16)], axis=0), k_bd[key])
              for key in cps}
        kk = {key: kq[key][0:c] for key in cps}
        qk = {key: kq[key][c:2 * c] for key in cps}

        decay, a_mat = {}, {}
        for ci, d, p in chains:
            g0 = d * C_HEADS + 2 * p
            grow_pair = jnp.concatenate([grow[ci][g0:g0 + 1, :], grow[ci][g0 + 1:g0 + 2, :]], axis=1)
            dec = jnp.where(incl_pw[d], jnp.exp(xs(gc_x, ci, d, p) - grow_pair), 0.0)
            decay[ci, d, p] = dec
            a_mat[ci, d, p] = jnp.where(strict_pw[d], kk[ci, p] * xs(beta_x, ci, d, p) * dec, 0.0)

        t_mat = {ch: jnp.where(eye_pw, 1.0, 0.0) - jnp.where(level_pw[0], a_mat[ch], 0.0) for ch in chains}
        for mask in level_pw[1:]:
            tb = {ch: t_mat[ch].astype(BF16) for ch in chains}
            t_bd = {ch: block_diag(tb[ch]) for ch in chains}
            et = {ch: _dot(jnp.where(mask, a_mat[ch], 0.0).astype(BF16), t_bd[ch]).astype(BF16) for ch in chains}
            t_mat = {ch: t_mat[ch] - _dot(tb[ch], block_diag(et[ch])) for ch in chains}

        uw = {}
        for ci, d, p in chains:
            vb = (pslice(v_all, ci, p) * xs(beta_x, ci, d, p)).astype(BF16)
            kbe = (pslice(k_all, ci, p) * xs(begc_x, ci, d, p)).astype(BF16)
            rhs = jnp.concatenate([block_diag(vb), block_diag(kbe)], axis=1)
            uw[ci, d, p] = _dot(t_mat[ci, d, p].astype(BF16), rhs)

        for ci in range(ach):
            m = (r0 // c) + ci
            q_c = q_all[ci * c:(ci + 1) * c, :]
            k_c = k_all[ci * c:(ci + 1) * c, :]
            for d in range(N_DIRS):
                dl = slice(d * C_WIDTH, (d + 1) * C_WIDTH)
                wq_scr[m, d, c:2 * c, :] = (q_c * e_x[ci][:, dl]).astype(BF16)
                kd_scr[m, d] = (k_c * eglgc_x[ci][:, dl]).astype(BF16)
                egl_scr[m, d] = egl_x[ci][:, dl]
                for p in range(n_pairs):
                    pl_ = slice(p * pw, (p + 1) * pw)
                    u_scr[m, d, :, pl_] = uw[ci, d, p][:, 0:pw].astype(BF16)
                    wq_scr[m, d, 0:c, pl_] = uw[ci, d, p][:, pw:2 * pw].astype(BF16)
                    aqk_scr[m, d, :, pl_] = (qk[ci, p] * decay[ci, d, p]).astype(BF16)

    n_steps = nch // ach
    if n_steps == 1:
        a_step(0, True, True)
    else:
        a_step(0, True, False)
        if n_steps > 2:
            def a_loop(j, carry):
                a_step(j, False, False)
                return carry
            lax.fori_loop(1, n_steps - 1, a_loop, 0)
        a_step(n_steps - 1, False, True)

    zeros_h = jnp.zeros((hd, hd), F32)
    for d in range(N_DIRS):
        for p in range(n_pairs):
            if has_s0:
                top = jnp.concatenate([s0_ref[d, 2 * p], zeros_h], axis=1)
                bot = jnp.concatenate([zeros_h, s0_ref[d, 2 * p + 1]], axis=1)
                s_scr[d, p] = jnp.concatenate([top, bot], axis=0)
            else:
                s_scr[d, p] = jnp.zeros((pw, pw), F32)

    prow = lax.broadcasted_iota(jnp.int32, (pw, pw), 0)
    pcol = lax.broadcasted_iota(jnp.int32, (pw, pw), 1)
    same_head = (prow >= hd) == (pcol >= hd)
    pairs = [(d, p) for d in range(N_DIRS) for p in range(n_pairs)]

    def b_state(n):
        ms = (n, nch - 1 - n)
        ls = {(d, p): slice(p * pw, (p + 1) * pw) for d, p in pairs}
        state = {(d, p): s_scr[d, p] for d, p in pairs}
        r = {(d, p): _dot(wq_scr[ms[d], d, :, ls[d, p]], state[d, p].astype(BF16)) for d, p in pairs}
        vnb = {}
        for d, p in pairs:
            v_new = u_scr[ms[d], d, :, ls[d, p]].astype(F32) - r[d, p][0:c, :]
            vnb[d, p] = v_new.astype(BF16)
        for d, p in pairs:
            kv = _dot_tn(kd_scr[ms[d], d, :, ls[d, p]], vnb[d, p])
            s_scr[d, p] = state[d, p] * egl_scr[ms[d], d, :, ls[d, p]] + jnp.where(same_head, kv, 0.0)
        return ms, ls, r, vnb

    def b_output(ms, ls, r, vnb):
        for d, p in pairs:
            o = r[d, p][c:2 * c, :] + _dot(aqk_scr[ms[d], d, :, ls[d, p]], block_diag(vnb[d, p]))
            u_scr[ms[d], d, :, ls[d, p]] = o.astype(BF16)

    def b_step(i, carry):
        first = b_state(2 * i)
        second = b_state(2 * i + 1)
        b_output(*first)
        b_output(*second)
        return carry

    assert nch % 2 == 0
    lax.fori_loop(0, nch // 2, b_step, 0)

    inv_n = 1.0 / hd

    c_chunks = 4 if nch % 4 == 0 else 1

    def c_step(i, carry):
        m0 = i * c_chunks
        o = jnp.concatenate([u_scr[m0 + ci, 0].astype(F32) + u_scr[m0 + ci, 1].astype(F32)
                             for ci in range(c_chunks)], axis=0)
        r0 = pl.multiple_of(m0 * c, c_chunks * c)
        ms = _seg_sum(o * o, seg, terms=2) * inv_n
        on = o * lax.rsqrt(ms + EPS) * normg_ref[...]
        oc_ref[pl.ds(r0, c_chunks * c), :] = (on * scg_ref[pl.ds(r0, c_chunks * c), :].astype(F32)).astype(BF16)
        return carry

    lax.fori_loop(0, nch // c_chunks, c_step, 0)

    if emit_state:
        for d in range(N_DIRS):
            for p in range(n_pairs):
                s_pair = s_scr[d, p]
                sfin_ref[d, 2 * p] = s_pair[0:hd, 0:hd]
                sfin_ref[d, 2 * p + 1] = s_pair[hd:pw, hd:pw]


def _gdn(cqkv, ab, scg, convw, alog, dtb, normg, seg, expand, s0, layer, *, emit_state, ach):
    b, t, _ = cqkv.shape
    nch = t // GDN_CHUNK
    assert nch % ach == 0
    has_s0 = s0 is not None
    const2 = lambda bi: (0, 0)
    in_specs = [
        pl.BlockSpec((None, t, 3 * C_WIDTH), lambda bi: (bi, 0, 0)),
        pl.BlockSpec((None, t, 2 * N_GATES), lambda bi: (bi, 0, 0)),
        pl.BlockSpec((None, t, C_WIDTH), lambda bi: (bi, 0, 0)),
        _layer_spec(convw, layer),
        _layer_spec(alog, layer),
        _layer_spec(dtb, layer),
        _layer_spec(normg, layer),
        pl.BlockSpec(seg.shape, const2),
        pl.BlockSpec(expand.shape, const2),
    ]
    args = [cqkv, ab, scg, convw, alog, dtb, normg, seg, expand]
    state_spec = pl.BlockSpec((None, N_DIRS, C_HEADS, C_HEAD_DIM, C_HEAD_DIM), lambda bi: (bi, 0, 0, 0, 0))
    if has_s0:
        in_specs.append(pl.BlockSpec((None, None, N_DIRS, C_HEADS, C_HEAD_DIM, C_HEAD_DIM),
                                     lambda bi: (bi, layer, 0, 0, 0, 0)))
        args.append(s0)
    out_specs = [pl.BlockSpec((None, t, C_WIDTH), lambda bi: (bi, 0, 0))]
    out_shape = [jax.ShapeDtypeStruct((b, t, C_WIDTH), BF16)]
    if emit_state:
        out_specs.append(state_spec)
        out_shape.append(jax.ShapeDtypeStruct((b, N_DIRS, C_HEADS, C_HEAD_DIM, C_HEAD_DIM), F32))
    pair_w = 2 * C_HEAD_DIM
    scratch = [
        pltpu.VMEM((nch, N_DIRS, 2 * GDN_CHUNK, C_WIDTH), BF16),
        pltpu.VMEM((nch, N_DIRS, GDN_CHUNK, C_WIDTH), BF16),
        pltpu.VMEM((nch, N_DIRS, GDN_CHUNK, C_WIDTH), BF16),
        pltpu.VMEM((nch, N_DIRS, GDN_CHUNK, C_WIDTH), BF16),
        pltpu.VMEM((nch, N_DIRS, 1, C_WIDTH), F32),
        pltpu.VMEM((N_DIRS, C_HEADS // 2, pair_w, pair_w), F32),
        pltpu.VMEM((ach * GDN_CHUNK + 2 * BF16_SUBLANES, 3 * C_WIDTH), F32),
    ]
    return pl.pallas_call(
        functools.partial(_gdn_body, t=t, has_s0=has_s0, emit_state=emit_state, ach=ach),
        grid=(b,),
        in_specs=in_specs,
        out_specs=out_specs,
        out_shape=out_shape,
        scratch_shapes=scratch,
        compiler_params=pltpu.CompilerParams(dimension_semantics=("arbitrary",),
                                             vmem_limit_bytes=VMEM_LIMIT_BYTES),
        name="gdn_state" if emit_state else "gdn",
    )(*args)


def _attn_body(*refs, banded, nseq, tq, nblk_total):
    it = iter(refs)
    x_ref, q_ref, kv_ref = (next(it) for _ in range(3))
    if banded:
        kvl_ref, kvr_ref, band_ref = (next(it) for _ in range(3))
        ck_ref, cv_ref = (next(it) for _ in range(2))
    sink_ref, sag_ref, ob_ref, oc_ref, wout_ref, gpost_ref, mod_ref, y_ref = (next(it) for _ in range(8))

    nb = tq // ATTN_BLOCK
    rows_q = A_GROUP * ATTN_BLOCK
    lane = lax.broadcasted_iota(jnp.int32, (1, A_KV_WIDTH), 1)
    head_mask = [lane < HEAD_DIM, lane >= HEAD_DIM]
    sum_lane = [HEAD_DIM, 0]

    def split_heads(a, spare=0.0):
        fill = [jnp.where(lane == sum_lane[h], spare, 0.0).astype(a.dtype) for h in range(A_KV_HEADS)]
        return [jnp.where(head_mask[h], a, fill[h]) for h in range(A_KV_HEADS)]

    if banded:
        kv_ext = jnp.concatenate([kvl_ref[0], kv_ref[0], kvr_ref[0]], axis=0)
        k_own = split_heads(kv_ext[:, 0:A_KV_WIDTH])
        v_own = split_heads(kv_ext[:, A_KV_WIDTH:2 * A_KV_WIDTH], spare=1.0)
        k_ctx = [split_heads(ck_ref[...].astype(BF16))]
        v_ctx = [split_heads(cv_ref[...].astype(BF16), spare=1.0)]
        first_blk = pl.program_id(1) * nb
        vis_l, vis_r = [], []
        for j in range(nb):
            thr_l = jnp.where(first_blk + j > 0, 0.5, 2.0)
            thr_r = jnp.where(first_blk + j < nblk_total - 1, 0.5, 2.0)
            vis_l.append(band_ref[:, 0:ATTN_BLOCK] > thr_l)
            vis_r.append(band_ref[:, 2 * ATTN_BLOCK:3 * ATTN_BLOCK] > thr_r)
    else:
        k_ctx = [split_heads(kv_ref[s, :, 0:A_KV_WIDTH]) for s in range(nseq)]
        v_ctx = [split_heads(kv_ref[s, :, A_KV_WIDTH:2 * A_KV_WIDTH], spare=1.0) for s in range(nseq)]

    units = [(s, j, h) for s in range(nseq) for j in range(nb) for h in range(A_KV_HEADS)]
    q2 = {(s, j): jnp.concatenate([q_ref[s, j * ATTN_BLOCK:(j + 1) * ATTN_BLOCK, g * A_KV_WIDTH:(g + 1) * A_KV_WIDTH]
                                   for g in range(A_GROUP)], axis=0) for s in range(nseq) for j in range(nb)}

    def scores(s, j, h):
        s_ctx = _dot_nt(q2[s, j], k_ctx[s][h])
        if not banded:
            return s_ctx, None
        s_own = _dot_nt(q2[s, j], k_own[h][j * ATTN_BLOCK:(j + 3) * ATTN_BLOCK])
        left = jnp.where(vis_l[j], s_own[:, 0:ATTN_BLOCK], NEG_INF)
        right = jnp.where(vis_r[j], s_own[:, 2 * ATTN_BLOCK:3 * ATTN_BLOCK], NEG_INF)
        return s_ctx, jnp.concatenate([left, s_own[:, ATTN_BLOCK:2 * ATTN_BLOCK], right], axis=1)

    o_acc = {(s, j): jnp.zeros((rows_q, A_KV_WIDTH), F32) for s in range(nseq) for j in range(nb)}
    pending = scores(*units[0])
    for ui, (s, j, h) in enumerate(units):
        s_ctx, s_own = pending
        if ui + 1 < len(units):
            pending = scores(*units[ui + 1])
        sink = sink_ref[h] * LOG2_E
        s_all = jnp.concatenate([s_ctx, s_own], axis=1) if banded else s_ctx
        m = jnp.maximum(jnp.max(s_all, axis=-1, keepdims=True), sink)
        pb = jnp.exp2(s_all - m).astype(BF16)
        n_ctx = s_ctx.shape[1]
        acc = _dot(pb[:, 0:n_ctx], v_ctx[s][h])
        if banded:
            acc = acc + _dot(pb[:, n_ctx:], v_own[h][j * ATTN_BLOCK:(j + 3) * ATTN_BLOCK])
        denom = acc[:, sum_lane[h]:sum_lane[h] + 1] + jnp.exp2(sink - m)
        o_acc[s, j] = o_acc[s, j] + jnp.where(head_mask[h], acc, 0.0) * (1.0 / denom)
    o_blocks = [jnp.concatenate([o_acc[s, j][g * ATTN_BLOCK:(g + 1) * ATTN_BLOCK] for g in range(A_GROUP)], axis=1)
                for s in range(nseq) for j in range(nb)]
    o_a = jnp.concatenate(o_blocks, axis=0) if len(o_blocks) > 1 else o_blocks[0]

    def rows_of(ref):
        return jnp.concatenate([ref[s] for s in range(nseq)], axis=0) if nseq > 1 else ref[0]

    mix_a = (o_a * rows_of(sag_ref).astype(F32)).astype(BF16)
    mix = (_dot(mix_a, wout_ref[0:A_WIDTH, :])
           + _dot(rows_of(ob_ref), wout_ref[A_WIDTH:A_WIDTH + B_WIDTH, :])
           + _dot(rows_of(oc_ref), wout_ref[A_WIDTH + B_WIDTH:A_WIDTH + B_WIDTH + C_WIDTH, :]))
    ms = jnp.mean(mix * mix, axis=-1, keepdims=True)
    y = rows_of(x_ref) + mod_ref[2:3, :] * (mix * lax.rsqrt(ms + EPS) * gpost_ref[...])
    for s in range(nseq):
        y_ref[s] = y[s * tq:(s + 1) * tq]


def _attention_out(x, q, kv, sag, ob, oc, sink_rows, wout, gpost, mod, band, ctx_k, ctx_v, layer, *, mod_row0,
                   per_batch, tq, nseq):
    b, t, d = x.shape
    banded = band is not None
    assert b % nseq == 0 and (nseq == 1 or not per_batch)
    nblk_total = t // ATTN_BLOCK
    per_q = tq // ATTN_BLOCK
    const2 = lambda bi, i: (0, 0)

    def tok(width):
        return pl.BlockSpec((nseq, tq, width), lambda bi, i: (bi, i, 0))

    in_specs = [tok(d), tok(A_WIDTH)]
    args = [x, q]
    if banded:
        in_specs.append(tok(2 * A_KV_WIDTH))
        args.append(kv)
        in_specs.append(pl.BlockSpec((1, ATTN_BLOCK, 2 * A_KV_WIDTH),
                                     lambda bi, i: (bi, jnp.maximum(i * per_q - 1, 0), 0)))
        in_specs.append(pl.BlockSpec((1, ATTN_BLOCK, 2 * A_KV_WIDTH),
                                     lambda bi, i: (bi, jnp.minimum((i + 1) * per_q, nblk_total - 1), 0)))
        in_specs.append(pl.BlockSpec(band.shape, const2))
        args += [kv, kv, band]
        past = ctx_k.shape[2]
        in_specs += [pl.BlockSpec((None, None, past, A_KV_WIDTH), lambda bi, i: (bi, layer, 0, 0))] * 2
        args += [ctx_k, ctx_v]
    else:
        in_specs.append(pl.BlockSpec((nseq, t, 2 * A_KV_WIDTH), lambda bi, i: (bi, 0, 0)))
        args.append(kv)
    in_specs += [
        _layer_spec(sink_rows, layer),
        tok(A_WIDTH), tok(B_WIDTH), tok(C_WIDTH),
        _layer_spec(wout, layer),
        _layer_spec(gpost, layer),
        _mod_spec(mod, layer, mod_row0, per_batch),
    ]
    args += [sink_rows, sag, ob, oc, wout, gpost, mod]
    return pl.pallas_call(
        functools.partial(_attn_body, banded=banded, nseq=nseq, tq=tq, nblk_total=nblk_total),
        grid=(b // nseq, t // tq),
        in_specs=in_specs,
        out_specs=tok(d),
        out_shape=jax.ShapeDtypeStruct((b, t, d), F32),
        compiler_params=pltpu.CompilerParams(dimension_semantics=("arbitrary", "arbitrary"),
                                             vmem_limit_bytes=VMEM_LIMIT_BYTES),
        name="attention_banded" if banded else "attention_full",
    )(*args)


def _block_ones(width, block):
    idx = np.arange(width) // block
    return jnp.asarray((idx[:, None] == idx[None, :]).astype(np.float32), dtype=BF16)


def _gate_expansion():
    lane_gate = np.arange(N_GATES * C_HEAD_DIM) // C_HEAD_DIM
    return jnp.asarray((np.arange(N_GATES)[:, None] == lane_gate[None, :]).astype(np.float32), dtype=BF16)


def _rope_tables(n):
    half = HEAD_DIM // 2
    nf = half // 2
    pos = np.arange(n)
    row = (pos // GRID_W).astype(np.float32)
    colp = (pos % GRID_W).astype(np.float32)
    inv_freq = np.float32(ROPE_BASE) ** (-np.arange(nf, dtype=np.float32) / np.float32(nf))
    lane = np.arange(LANES) % HEAD_DIM
    freq_idx = lane % nf
    quarter = lane // nf
    p = np.where((quarter < 2)[None, :], row[:, None], colp[:, None])
    ang = (p * inv_freq[freq_idx][None, :]).astype(np.float32)
    cos, sin = np.cos(ang), np.sin(ang)
    first = (quarter % 2 == 0)[None, :]
    sa = np.where(first, -sin, 0.0)
    sb = np.where(first, 0.0, sin)
    return tuple(jnp.asarray(a, dtype=F32) for a in (cos, sa, sb))


def _band_table():
    q_rel = np.arange(ATTN_BLOCK)
    k_rel = np.arange(3 * ATTN_BLOCK) - ATTN_BLOCK
    band = (np.abs(q_rel[:, None] - k_rel[None, :]) <= ATTN_BLOCK).astype(np.float32)
    return jnp.asarray(np.tile(band, (A_GROUP, 1)))


def _group_major(a, axis):
    shape = a.shape
    split = shape[:axis] + (A_KV_HEADS, A_GROUP, HEAD_DIM) + shape[axis + 1:]
    return jnp.swapaxes(a.reshape(split), axis, axis + 1).reshape(shape)


def _layout_w_in(w_in):
    o = 0
    sec = {}
    for name, width in (("aq", A_WIDTH), ("ak", A_KV_WIDTH), ("av", A_KV_WIDTH), ("ag", A_WIDTH),
                        ("bu", B_WIDTH), ("bv", B_WIDTH), ("bg", B_WIDTH),
                        ("cq", C_WIDTH), ("ck", C_WIDTH), ("cv", C_WIDTH), ("ca", N_GATES), ("cb", N_GATES),
                        ("cg", C_WIDTH)):
        sec[name] = w_in[:, :, o:o + width].astype(BF16)
        o += width
    pad = jnp.zeros(w_in.shape[:2] + (LANES - 2 * N_GATES,), BF16)
    cols = [_group_major(sec["aq"], 2), sec["ak"], sec["av"], _group_major(sec["ag"], 2), sec["bu"], sec["bv"],
            sec["bg"], sec["cq"], sec["ck"], sec["cv"], sec["cg"], sec["ca"], sec["cb"], pad]
    return jnp.concatenate(cols, axis=2)


def _row_tile(t, target):
    tile = min(t, target)
    while t % tile:
        tile -= ATTN_BLOCK
    return tile


def kernel(x_prompt, x_sample, cache_k, cache_v, state_delta, c, c_ctx, w_mod, b_mod, g_pre, w_in, attn_sink,
           sgu_ln_g, sgu_ln_b, sgu_w, sgu_b, gdn_conv_w, gdn_a_log, gdn_dt_bias, gdn_norm_g, g_post, w_out):
    depth, d, _ = w_mod.shape
    b_ctx, t_ctx, _ = x_prompt.shape
    b_lat, t_lat, _ = x_sample.shape
    past = cache_k.shape[2]
    assert t_ctx % SGU_CHUNK == 0 and t_lat % SGU_CHUNK == 0 and t_lat % GRID_W == 0
    assert t_ctx // GDN_CHUNK >= 2 and t_lat // GDN_CHUNK >= 2
    assert w_in.shape[2] == 2 * A_WIDTH + 2 * A_KV_WIDTH + 3 * B_WIDTH + 4 * C_WIDTH + 2 * N_GATES

    n_cond = b_lat + 1
    rows = -(-n_cond // 8) * 8
    conds = jnp.concatenate([c, c_ctx[None, :], jnp.zeros((rows - n_cond, d), c.dtype)], axis=0)
    mod = _modulation(conds, w_mod, b_mod).reshape(depth, rows, 3, d)

    seg_b = _block_ones(B_WIDTH, B_GROUP_DIM)
    seg_c = _block_ones(C_WIDTH, C_HEAD_DIM)
    expand_c = _gate_expansion()
    rope_tabs = _rope_tables(t_lat)
    band = _band_table()
    ck = cache_k.reshape(b_lat, depth, past, A_KV_WIDTH)
    cv = cache_v.reshape(b_lat, depth, past, A_KV_WIDTH)
    tm_ctx, tm_lat = _row_tile(b_ctx * t_ctx, PROJ_ROWS_CTX), _row_tile(t_lat, PROJ_ROWS_LAT)

    w1 = _layout_w_in(w_in)
    assert w1.shape[2] == _W1_COLS
    wout_b = w_out.astype(BF16)
    wout = jnp.concatenate([_group_major(wout_b[:, 0:A_WIDTH], 1), wout_b[:, A_WIDTH:]], axis=1)
    gpre = g_pre.reshape(depth, 1, d)
    gpost = g_post.reshape(depth, 1, d)
    lng = sgu_ln_g.reshape(depth, 1, B_WIDTH)
    lnb = sgu_ln_b.reshape(depth, 1, B_WIDTH)
    sw = sgu_w.astype(BF16)
    sbias = jnp.repeat(jnp.swapaxes(sgu_b, 1, 2), B_GROUP_DIM, axis=2)
    sink_rows = jnp.repeat(attn_sink.reshape(depth, A_KV_HEADS, A_GROUP), ATTN_BLOCK, axis=2)[..., None]
    alog = gdn_a_log.reshape(depth, 1, N_GATES)
    dtb = gdn_dt_bias.reshape(depth, 1, N_GATES)
    normg = jnp.tile(gdn_norm_g, (1, C_HEADS)).reshape(depth, 1, C_WIDTH)
    ctx_nseq = max(1, min(b_ctx, ATTN_ROWS_CTX // t_ctx))
    while b_ctx % ctx_nseq:
        ctx_nseq -= 1

    xp, xs = x_prompt, x_sample
    new_k, new_v, new_s = [], [], []
    for l in range(depth):
        flat = _projection(xp.reshape(1, b_ctx * t_ctx, d), mod, gpre, w1, seg_b, lng, lnb, sw, sbias, None, l,
                           mod_row0=b_lat, per_batch=False, keep_f32_kv=True, tm=tm_ctx)
        q, kv, sag, ob, cqkv, scg, ab, kf, vf = (a.reshape(b_ctx, t_ctx, a.shape[-1]) for a in flat)
        oc, s_fin = _gdn(cqkv, ab, scg, gdn_conv_w, alog, dtb, normg, seg_c, expand_c, None, l, emit_state=True,
                         ach=min(GDN_A_CHUNKS, t_ctx // GDN_CHUNK))
        xp = _attention_out(xp, q, kv, sag, ob, oc, sink_rows, wout, gpost, mod, None, None, None, l,
                            mod_row0=b_lat, per_batch=False, tq=t_ctx, nseq=ctx_nseq)
        new_k.append(kf.reshape(b_ctx, t_ctx, A_KV_HEADS, HEAD_DIM))
        new_v.append(vf.reshape(b_ctx, t_ctx, A_KV_HEADS, HEAD_DIM))
        new_s.append(s_fin)

        q, kv, sag, ob, cqkv, scg, ab = _projection(
            xs, mod, gpre, w1, seg_b, lng, lnb, sw, sbias, rope_tabs, l, mod_row0=0, per_batch=True,
            keep_f32_kv=False, tm=tm_lat)
        (oc,) = _gdn(cqkv, ab, scg, gdn_conv_w, alog, dtb, normg, seg_c, expand_c, state_delta, l, emit_state=False,
                     ach=min(GDN_A_CHUNKS, t_lat // GDN_CHUNK))
        xs = _attention_out(xs, q, kv, sag, ob, oc, sink_rows, wout, gpost, mod, band, ck, cv, l,
                            mod_row0=0, per_batch=True, tq=_row_tile(t_lat, ATTN_ROWS_LAT), nseq=1)

    return (xp, xs, jnp.stack(new_k, axis=1), jnp.stack(new_v, axis=1), jnp.stack(new_s, axis=1))
```
